```python
import math, functools
import jax, jax.numpy as jnp
from jax import lax
import numpy as np

D_MODEL = 2048
BATCH = 2
SEQ = 4096
DEPTH = 1
DEC_BATCH = 32
DEC_SEQ = 4
PAST_LEN = 8192
PAGE_SIZE = 128

MIX_W = D_MODEL
H_SB = 8
D_SB = 128
SB_W = H_SB * D_SB
Q_BLOCK = 128
H_ML = 4
DQK_ML = 128
DV_ML = 256
ML_QK_W = H_ML * DQK_ML
ML_V_W = H_ML * DV_ML
ML_CHUNK = 64
PROJ_W = 3 * SB_W + 2 * ML_QK_W + 2 * ML_V_W + 2 * H_ML
N_EXPERTS = 32
TOP_K = 4
D_FF = D_MODEL
SWIGLU_LIMIT = 7.0
SWIGLU_ALPHA = 1.702
MOE_BLOCK = 128
N_MOD = 6
EPS = 1e-6

kernel_name = 'sb_mlstm_parallel_heads_moe_step'


def rms_norm(x, g):
    xf = x.astype(jnp.float32)
    xf = xf * lax.rsqrt(jnp.mean(xf * xf, axis=-1, keepdims=True) + EPS)
    return (xf * g.astype(jnp.float32)).astype(x.dtype)


def modulate(x, g, shift, scale):
    return rms_norm(x, g) * (1.0 + scale[:, None, :]) + shift[:, None, :]


def sb_combine(z, mask, v, carry):
    log_beta = jax.nn.log_sigmoid(z)
    log_1m = jnp.where(mask, jax.nn.log_sigmoid(-z), 0.0)
    later = lax.cumsum(log_1m, axis=3, reverse=True) - log_1m
    a = jnp.where(mask, jnp.exp(log_beta + later + carry[..., None]), 0.0)
    o = jnp.einsum('bhqk,bkhd->bqhd', a.astype(v.dtype), v, preferred_element_type=jnp.float32)
    return o, carry + log_1m.sum(axis=-1)


def sb_logits(q, k, bias):
    scale = q.shape[-1] ** -0.5
    z = jnp.einsum('bqhd,bkhd->bhqk', q, k, preferred_element_type=jnp.float32) * scale
    return z + bias.astype(jnp.float32)[None, :, None, None]


def sb_attend_prompt(q, k, v, bias):
    bsz, seq, nh, hd = q.shape
    nb = seq // Q_BLOCK
    q_blocks = q.reshape(bsz, nb, Q_BLOCK, nh, hd).swapaxes(0, 1)
    key_pos = jnp.arange(seq)

    def one_block(args):
        q_blk, blk = args
        q_pos = blk * Q_BLOCK + jnp.arange(Q_BLOCK)
        z = sb_logits(q_blk, k, bias)
        mask = key_pos[None, :] < q_pos[:, None]
        o, _ = sb_combine(z, mask, v, jnp.zeros(z.shape[:-1], jnp.float32))
        return o

    o = lax.map(one_block, (q_blocks, jnp.arange(nb)))
    return o.swapaxes(0, 1).reshape(bsz, seq, nh, hd)


def sb_attend_paged(q, k, v, bias, cache_k, cache_v, page_table):
    tq = q.shape[1]
    z = sb_logits(q, k, bias)
    mask = jnp.arange(tq)[None, :] < jnp.arange(tq)[:, None]
    o, carry = sb_combine(z, mask, v, jnp.zeros(z.shape[:-1], jnp.float32))

    def page_step(state, pidx):
        o_acc, carry = state
        kp = cache_k[pidx]
        vp = cache_v[pidx]
        zp = sb_logits(q, kp, bias)
        op, carry = sb_combine(zp, True, vp, carry)
        return (o_acc + op, carry), None

    (o, _), _ = lax.scan(page_step, (o, carry), page_table[:, ::-1].T)
    return o


def mlstm_chunk_step(carry, chunk):
    c0, n0, m0 = carry
    q, k, v, i_pre, log_f = chunk
    L = q.shape[1]
    b_cum = jnp.cumsum(log_f, axis=1)
    causal = jnp.tril(jnp.ones((L, L), bool))
    log_w = b_cum[:, :, None, :] - b_cum[:, None, :, :] + i_pre[:, None, :, :]
    log_w = jnp.where(causal[None, :, :, None], log_w, -jnp.inf)
    m_state = b_cum + m0[:, None, :]
    m_new = jnp.maximum(m_state, log_w.max(axis=2))
    w = jnp.exp(log_w - m_new[:, :, None, :])
    g = jnp.exp(m_state - m_new)
    s = w * jnp.einsum('bthk,bshk->btsh', q, k)
    num = jnp.einsum('btsh,bshv->bthv', s, v) + g[..., None] * jnp.einsum('bhvk,bthk->bthv', c0, q)
    den = s.sum(axis=2) + g * jnp.einsum('bhk,bthk->bth', n0, q)
    h = num / jnp.maximum(jnp.abs(den), jnp.exp(-m_new))[..., None]
    m_end = m_new[:, -1]
    w_end = jnp.exp(b_cum[:, -1:, :] - b_cum + i_pre - m_end[:, None, :])
    g_end = jnp.exp(b_cum[:, -1] + m0 - m_end)
    c1 = g_end[..., None, None] * c0 + jnp.einsum('bsh,bshv,bshk->bhvk', w_end, v, k)
    n1 = g_end[..., None] * n0 + jnp.einsum('bsh,bshk->bhk', w_end, k)
    return (c1, n1, m_end), h


def mlstm_chunkwise(q, k, v, i_pre, log_f, c0, n0, m0, chunk):
    bsz, seq = q.shape[:2]
    nc = seq // chunk

    def to_chunks(a):
        return a.reshape(bsz, nc, chunk, *a.shape[2:]).swapaxes(0, 1)

    init = (c0.astype(jnp.float32), n0.astype(jnp.float32), m0.astype(jnp.float32))
    (c1, n1, m1), h = lax.scan(mlstm_chunk_step, init,
                               (to_chunks(q), to_chunks(k), to_chunks(v), to_chunks(i_pre), to_chunks(log_f)))
    h = h.swapaxes(0, 1).reshape(bsz, seq, *h.shape[3:])
    return h, c1, n1, m1


def moe_ffn(x, w_router, b_router, w_up, b_up, w_down, b_down):
    n_tok = x.shape[0]
    logits = (x @ w_router).astype(jnp.float32) + b_router
    top_val, top_idx = lax.top_k(logits, TOP_K)
    gate = jax.nn.softmax(top_val, axis=-1)
    n_assign = n_tok * TOP_K
    flat_e = top_idx.reshape(-1)
    flat_tok = jnp.arange(n_assign, dtype=jnp.int32) // TOP_K
    order = jnp.argsort(flat_e)
    sorted_e = flat_e[order]
    counts = jnp.bincount(flat_e, length=N_EXPERTS)
    padded = (counts + MOE_BLOCK - 1) // MOE_BLOCK * MOE_BLOCK
    pad_end = jnp.cumsum(padded)
    start = jnp.cumsum(counts) - counts
    dest = (pad_end - padded)[sorted_e] + jnp.arange(n_assign) - start[sorted_e]
    n_blocks = -(-n_assign // MOE_BLOCK) + N_EXPERTS
    n_rows = n_blocks * MOE_BLOCK
    row_tok = jnp.full((n_rows,), n_tok, jnp.int32).at[dest].set(flat_tok[order])
    row_gate = jnp.zeros((n_rows,), jnp.float32).at[dest].set(gate.reshape(-1)[order])
    block_e = jnp.minimum(jnp.searchsorted(pad_end, jnp.arange(n_blocks) * MOE_BLOCK, side='right'),
                          N_EXPERTS - 1)
    x_pad = jnp.concatenate([x, jnp.zeros((1, x.shape[1]), x.dtype)], axis=0)
    x_rows = x_pad[row_tok].reshape(n_blocks, MOE_BLOCK, x.shape[1])

    def expert_block(args):
        xb, e = args
        hu = xb @ w_up[e] + b_up[e]
        g = jnp.minimum(hu[..., 0::2], SWIGLU_LIMIT)
        u = jnp.clip(hu[..., 1::2], -SWIGLU_LIMIT, SWIGLU_LIMIT)
        return ((u + 1.0) * g * jax.nn.sigmoid(SWIGLU_ALPHA * g)) @ w_down[e] + b_down[e]

    y_rows = lax.map(expert_block, (x_rows, block_e)).reshape(n_rows, -1)
    y = jax.ops.segment_sum(y_rows * row_gate[:, None].astype(y_rows.dtype), row_tok,
                            num_segments=n_tok + 1)
    return y[:n_tok].astype(x.dtype)


def hybrid_layer(x, c, sb_attend, c0, n0, m0, ml_chunk,
                 w_ada, b_ada, norm_mix, norm_ffn, w_in, b_gates, sb_bias, norm_sb_out, norm_ml_out,
                 w_out, w_router, b_router, w_up, b_up, w_down, b_down):
    bsz, seq, _ = x.shape
    mod = jax.nn.silu(c) @ w_ada + b_ada
    sh_a, sc_a, gt_a, sh_f, sc_f, gt_f = jnp.split(mod, N_MOD, axis=-1)

    xm = modulate(x, norm_mix, sh_a, sc_a)
    proj = xm @ w_in
    cuts = [int(v) for v in np.cumsum([SB_W, SB_W, SB_W, ML_QK_W, ML_QK_W, ML_V_W, ML_V_W])]
    q_sb, k_sb, v_sb, q_ml, k_ml, v_ml, o_ml, gates = jnp.split(proj, cuts, axis=-1)
    q_sb = q_sb.reshape(bsz, seq, H_SB, D_SB)
    k_sb = k_sb.reshape(bsz, seq, H_SB, D_SB)
    v_sb = v_sb.reshape(bsz, seq, H_SB, D_SB)

    o_sb = sb_attend(q_sb, k_sb, v_sb, sb_bias)
    o_sb = rms_norm(o_sb, norm_sb_out.reshape(H_SB, D_SB)).reshape(bsz, seq, SB_W)

    gates = gates.astype(jnp.float32) + b_gates
    i_pre = gates[..., :H_ML]
    log_f = jax.nn.log_sigmoid(gates[..., H_ML:])
    qm = q_ml.reshape(bsz, seq, H_ML, DQK_ML).astype(jnp.float32)
    km = k_ml.reshape(bsz, seq, H_ML, DQK_ML).astype(jnp.float32) * (DQK_ML ** -0.5)
    vm = v_ml.reshape(bsz, seq, H_ML, DV_ML).astype(jnp.float32)
    h_ml, c1, n1, m1 = mlstm_chunkwise(qm, km, vm, i_pre, log_f, c0, n0, m0, ml_chunk)
    h_ml = rms_norm(h_ml, norm_ml_out.reshape(H_ML, DV_ML)).reshape(bsz, seq, ML_V_W)
    h_ml = h_ml * jax.nn.sigmoid(o_ml.astype(jnp.float32))

    mix = jnp.concatenate([o_sb, h_ml], axis=-1).astype(x.dtype) @ w_out
    h = x + gt_a[:, None, :] * mix

    xf = modulate(h, norm_ffn, sh_f, sc_f)
    ffn = moe_ffn(xf.reshape(bsz * seq, D_MODEL), w_router, b_router, w_up, b_up, w_down, b_down)
    y = h + gt_f[:, None, :] * ffn.reshape(bsz, seq, D_MODEL)
    return y, k_sb, v_sb, c1, n1, m1


def setup_inputs(seed: int = 0) -> dict:
    key = jax.random.key(seed)
    ks = jax.random.split(key, 24)
    f32 = jnp.float32

    def nrm(k, shape, s):
        return jax.random.normal(k, shape, f32) * s

    n_pages = PAST_LEN // PAGE_SIZE
    n_used = DEC_BATCH * n_pages
    n_pool = n_used + n_used // 4
    page_table = jax.random.permutation(ks[6], n_pool)[:n_used].reshape(DEC_BATCH, n_pages).astype(jnp.int32)
    f_bias = 3.0 + 3.0 * jnp.linspace(0.0, 1.0, H_ML)
    b_gates = jnp.concatenate([nrm(ks[12], (DEPTH, H_ML), 0.1),
                               f_bias[None, :] + nrm(ks[13], (DEPTH, H_ML), 0.1)], axis=-1)
    sb_bias = jnp.linspace(-10.0, -6.0, H_SB)[None, :] + nrm(jax.random.fold_in(ks[12], 3), (DEPTH, H_SB), 0.1)
    return {
        'x_prompt': nrm(ks[0], (BATCH, SEQ, D_MODEL), 1.0),
        'x_sample': nrm(ks[1], (DEC_BATCH, DEC_SEQ, D_MODEL), 1.0),
        'c_prompt': nrm(ks[2], (BATCH, D_MODEL), 1.0),
        'c_sample': nrm(ks[3], (DEC_BATCH, D_MODEL), 1.0),
        'cache_k': nrm(ks[4], (DEPTH, n_pool, PAGE_SIZE, H_SB, D_SB), 1.0),
        'cache_v': nrm(ks[5], (DEPTH, n_pool, PAGE_SIZE, H_SB, D_SB), 1.0),
        'page_table': page_table,
        'state_C': nrm(ks[7], (DEPTH, DEC_BATCH, H_ML, DV_ML, DQK_ML), 0.3),
        'state_n': nrm(ks[8], (DEPTH, DEC_BATCH, H_ML, DQK_ML), 0.3),
        'state_m': nrm(ks[9], (DEPTH, DEC_BATCH, H_ML), 1.0),
        'w_ada': nrm(ks[10], (DEPTH, D_MODEL, N_MOD * D_MODEL), 0.5 * D_MODEL ** -0.5),
        'b_ada': nrm(ks[11], (DEPTH, N_MOD * D_MODEL), 0.02),
        'norm_mix': 1.0 + nrm(ks[14], (DEPTH, D_MODEL), 0.02),
        'norm_ffn': 1.0 + nrm(ks[15], (DEPTH, D_MODEL), 0.02),
        'w_in': nrm(ks[16], (DEPTH, D_MODEL, PROJ_W), D_MODEL ** -0.5),
        'b_gates': b_gates,
        'sb_bias': sb_bias,
        'norm_sb_out': 1.0 + nrm(ks[17], (DEPTH, SB_W), 0.02),
        'norm_ml_out': 1.0 + nrm(ks[18], (DEPTH, ML_V_W), 0.02),
        'w_out': nrm(ks[19], (DEPTH, MIX_W, D_MODEL), MIX_W ** -0.5),
        'w_router': nrm(ks[20], (DEPTH, D_MODEL, N_EXPERTS), D_MODEL ** -0.5),
        'b_router': nrm(ks[21], (DEPTH, N_EXPERTS), 0.01),
        'w_up': nrm(ks[22], (DEPTH, N_EXPERTS, D_MODEL, 2 * D_FF), D_MODEL ** -0.5),
        'b_up': nrm(jax.random.fold_in(ks[22], 1), (DEPTH, N_EXPERTS, 2 * D_FF), 0.02),
        'w_down': nrm(ks[23], (DEPTH, N_EXPERTS, D_FF, D_MODEL), D_FF ** -0.5),
        'b_down': nrm(jax.random.fold_in(ks[23], 1), (DEPTH, N_EXPERTS, D_MODEL), 0.02),
        'norm_final': 1.0 + nrm(jax.random.fold_in(ks[0], 7), (D_MODEL,), 0.02),
    }


def reference(x_prompt, x_sample, c_prompt, c_sample, cache_k, cache_v, page_table,
              state_C, state_n, state_m, w_ada, b_ada, norm_mix, norm_ffn, w_in, b_gates, sb_bias,
              norm_sb_out, norm_ml_out, w_out, w_router, b_router, w_up, b_up, w_down, b_down,
              norm_final):
    bsz, seq = x_prompt.shape[:2]
    dec_seq = x_sample.shape[1]
    prompt_chunk = math.gcd(ML_CHUNK, seq)
    yp, ys = x_prompt, x_sample
    kp_l, vp_l, ks_l, vs_l = [], [], [], []
    cp_l, np_l, mp_l, cs_l, ns_l, ms_l = [], [], [], [], [], []
    for layer in range(DEPTH):
        weights = (w_ada[layer], b_ada[layer], norm_mix[layer], norm_ffn[layer], w_in[layer],
                   b_gates[layer], sb_bias[layer], norm_sb_out[layer], norm_ml_out[layer], w_out[layer],
                   w_router[layer], b_router[layer], w_up[layer], b_up[layer], w_down[layer],
                   b_down[layer])
        c0 = jnp.zeros((bsz, H_ML, DV_ML, DQK_ML), jnp.float32)
        n0 = jnp.zeros((bsz, H_ML, DQK_ML), jnp.float32)
        m0 = jnp.zeros((bsz, H_ML), jnp.float32)
        yp, kp, vp, cp, nq, mp = hybrid_layer(yp, c_prompt, sb_attend_prompt, c0, n0, m0,
                                              prompt_chunk, *weights)
        paged = functools.partial(sb_attend_paged, cache_k=cache_k[layer], cache_v=cache_v[layer],
                                  page_table=page_table)
        ys, kq, vq, cs, ns, ms = hybrid_layer(ys, c_sample, paged, state_C[layer], state_n[layer],
                                              state_m[layer], dec_seq, *weights)
        kp_l.append(kp); vp_l.append(vp); ks_l.append(kq); vs_l.append(vq)
        cp_l.append(cp); np_l.append(nq); mp_l.append(mp)
        cs_l.append(cs); ns_l.append(ns); ms_l.append(ms)
    y_prompt = rms_norm(yp, norm_final)
    y_sample = rms_norm(ys, norm_final)
    return (y_prompt, y_sample,
            jnp.stack(kp_l), jnp.stack(vp_l), jnp.stack(ks_l), jnp.stack(vs_l),
            jnp.stack(cp_l), jnp.stack(np_l), jnp.stack(mp_l),
            jnp.stack(cs_l), jnp.stack(ns_l), jnp.stack(ms_l))
```

```python
import functools

import jax
import jax.numpy as jnp
from jax import lax
from jax.experimental import pallas as pl
from jax.experimental.pallas import tpu as pltpu

F32 = jnp.float32
BF16 = jnp.bfloat16
I32 = jnp.int32

H_SB = 8
D_SB = 128
SB_W = H_SB * D_SB
H_ML = 4
DQK_ML = 128
DV_ML = 256
ML_QK_W = H_ML * DQK_ML
ML_V_W = H_ML * DV_ML
PROJ_MAIN_W = 3 * SB_W + 2 * ML_QK_W + 2 * ML_V_W
N_GATES = 2 * H_ML
N_EXPERTS = 32
TOP_K = 4
N_MOD = 6
SWIGLU_LIMIT = 7.0
SWIGLU_ALPHA = 1.702
EPS = 1e-6

LANES = 128
SUBLANES = 8
BF16_SUBLANES = 16
MIB = 1024 * 1024

MOE_TILE = 128
MOE_ITEM_TILES = 16
MOE_FF_TILE = 256
ML_CHUNK = 128
SB_BLOCK = 256
PAGES_PER_STEP = 8
COMBINE_TOK = 64

NT_DIMS = (((1,), (1,)), ((), ()))
TN_DIMS = (((0,), (0,)), ((), ()))


def _cparams(sem, vmem_mib):
    return pltpu.CompilerParams(dimension_semantics=sem, vmem_limit_bytes=vmem_mib * MIB)


def _log_sigmoid_pair(z):
    t = jnp.log1p(jnp.exp(-jnp.abs(z)))
    return jnp.minimum(z, 0.0) - t, -jnp.maximum(z, 0.0) - t


def _split_bf16(x):
    hi = x.astype(BF16)
    lo = (x - hi.astype(F32)).astype(BF16)
    return hi, lo


def _dot3(x, w):
    xh, xl = _split_bf16(x)
    wh, wl = _split_bf16(w)
    return (jnp.dot(xh, wh, preferred_element_type=F32)
            + (jnp.dot(xl, wh, preferred_element_type=F32)
               + jnp.dot(xh, wl, preferred_element_type=F32)))


def _mod_spec(arr, tile, rows_per_group, col_tile=None):
    per_group = arr.shape[1] == 1
    width = arr.shape[2] if col_tile is None else col_tile

    def index_map(i, *rest):
        col = 0 if col_tile is None else rest[0]
        if per_group:
            return ((i * tile) // rows_per_group, 0, col)
        return (0, i, col)

    return pl.BlockSpec((None, 1 if per_group else tile, width), index_map)


def _ada_kernel(c_ref, w_ref, b_ref, o_ref):
    c = c_ref[...]
    o_ref[...] = _dot3(c * jax.nn.sigmoid(c), w_ref[...]) + b_ref[...]


def _ada(c_all, w_ada, b_ada):
    n, d = c_all.shape
    w_out = w_ada.shape[1]
    tn = 1024
    return pl.pallas_call(
        _ada_kernel,
        out_shape=jax.ShapeDtypeStruct((n, w_out), F32),
        grid=(w_out // tn,),
        in_specs=[pl.BlockSpec((n, d), lambda j: (0, 0)),
                  pl.BlockSpec((d, tn), lambda j: (0, j)),
                  pl.BlockSpec((1, tn), lambda j: (0, j))],
        out_specs=pl.BlockSpec((n, tn), lambda j: (0, j)),
        compiler_params=_cparams(("arbitrary",), 40),
        name="ada",
    )(c_all, w_ada, b_ada.reshape(1, w_out))


def _inproj_kernel(x_ref, sh_ref, sc_ref, g_ref, w_ref, wg_ref, bg_ref,
                   proj_ref, gates_ref, xm_scr):
    @pl.when(pl.program_id(1) == 0)
    def _():
        x = x_ref[...]
        xn = x * lax.rsqrt(jnp.mean(x * x, axis=-1, keepdims=True) + EPS) * g_ref[...]
        xm = (xn * (1.0 + sc_ref[...]) + sh_ref[...]).astype(BF16)
        xm_scr[...] = xm
        gates_ref[...] = jnp.dot(xm, wg_ref[...].astype(BF16),
                                 preferred_element_type=F32) + bg_ref[...]

    proj_ref[...] = jnp.dot(xm_scr[...], w_ref[...].astype(BF16),
                            preferred_element_type=F32)


def _inproj(x2d, shift, scale, norm_w, w_in, wg_pad, bg_pad, *, tm, rows_per_group):
    t, d = x2d.shape
    tn = 512
    mod_spec = _mod_spec(shift, tm, rows_per_group)
    return pl.pallas_call(
        _inproj_kernel,
        out_shape=(jax.ShapeDtypeStruct((t, PROJ_MAIN_W), F32),
                   jax.ShapeDtypeStruct((t, LANES), F32)),
        grid=(t // tm, PROJ_MAIN_W // tn),
        in_specs=[pl.BlockSpec((tm, d), lambda i, j: (i, 0)),
                  mod_spec, mod_spec,
                  pl.BlockSpec((1, d), lambda i, j: (0, 0)),
                  pl.BlockSpec((d, tn), lambda i, j: (0, j)),
                  pl.BlockSpec((d, LANES), lambda i, j: (0, 0)),
                  pl.BlockSpec((1, LANES), lambda i, j: (0, 0))],
        out_specs=(pl.BlockSpec((tm, tn), lambda i, j: (i, j)),
                   pl.BlockSpec((tm, LANES), lambda i, j: (i, 0))),
        scratch_shapes=[pltpu.VMEM((tm, d), BF16)],
        compiler_params=_cparams(("arbitrary", "arbitrary"), 48),
        name="inproj",
    )(x2d, shift, scale, norm_w.reshape(1, d), w_in, wg_pad, bg_pad)


def _sb_block(z, mask, later_u, v_bf, carry):
    log_beta, log_1m = _log_sigmoid_pair(z)
    if mask is not None:
        log_1m = jnp.where(mask, log_1m, 0.0)
    hi, lo = _split_bf16(log_1m)
    later = (jnp.dot(hi, later_u, preferred_element_type=F32)
             + jnp.dot(lo, later_u, preferred_element_type=F32))
    a = jnp.exp(log_beta + later + carry)
    if mask is not None:
        a = jnp.where(mask, a, 0.0)
    o = jnp.dot(a.astype(BF16), v_bf, preferred_element_type=F32)
    return o, carry + jnp.sum(log_1m, axis=-1, keepdims=True)


def _later_matrix(n):
    j = lax.broadcasted_iota(I32, (n, n), 0)
    s = lax.broadcasted_iota(I32, (n, n), 1)
    return jnp.where(j > s, 1.0, 0.0).astype(BF16)


def _sbp_kernel(bias_ref, q_ref, k_ref, v_ref, g_ref, o_ref, *, blk, scale):
    h = pl.program_id(1)
    qi = pl.program_id(2)
    bias = bias_ref[h]
    q = q_ref[...].astype(BF16)
    later_u = _later_matrix(blk)

    def logits(kb):
        start = pl.multiple_of(kb * blk, blk)
        k_bf = k_ref[pl.ds(start, blk), :].astype(BF16)
        v_bf = v_ref[pl.ds(start, blk), :].astype(BF16)
        z = lax.dot_general(q, k_bf, NT_DIMS, preferred_element_type=F32) * scale + bias
        return z, v_bf

    t = lax.broadcasted_iota(I32, (blk, blk), 0)
    s = lax.broadcasted_iota(I32, (blk, blk), 1)
    z, v_bf = logits(qi)
    o, carry = _sb_block(z, s < t, later_u, v_bf, jnp.zeros((blk, 1), F32))

    def older(n, state):
        o, carry = state
        z, v_bf = logits(qi - 1 - n)
        o_blk, carry = _sb_block(z, None, later_u, v_bf, carry)
        return o + o_blk, carry

    o, _ = lax.fori_loop(0, qi, older, (o, carry))
    o = o * lax.rsqrt(jnp.mean(o * o, axis=-1, keepdims=True) + EPS) * g_ref[...]
    o_ref[...] = o.astype(o_ref.dtype)


def _sb_prompt(proj, sb_bias, norm_sb_out, *, bsz, seq):
    blk = min(SB_BLOCK, seq)
    nq = seq // blk
    kern = functools.partial(_sbp_kernel, blk=blk, scale=D_SB ** -0.5)
    return pl.pallas_call(
        kern,
        out_shape=jax.ShapeDtypeStruct((bsz * seq, SB_W), BF16),
        grid=(bsz, H_SB, nq),
        in_specs=[pl.BlockSpec(memory_space=pltpu.SMEM),
                  pl.BlockSpec((blk, D_SB), lambda b, h, i: (b * nq + i, h)),
                  pl.BlockSpec((seq, D_SB), lambda b, h, i: (b, H_SB + h)),
                  pl.BlockSpec((seq, D_SB), lambda b, h, i: (b, 2 * H_SB + h)),
                  pl.BlockSpec((None, 1, D_SB), lambda b, h, i: (h, 0, 0))],
        out_specs=pl.BlockSpec((blk, D_SB), lambda b, h, i: (b * nq + i, h)),
        compiler_params=_cparams(("arbitrary", "arbitrary", "arbitrary"), 40),
        name="sb_prompt",
    )(sb_bias, proj, proj, proj, norm_sb_out.reshape(H_SB, 1, D_SB))


def _head_block_mask(rows, dec_seq):
    r = lax.broadcasted_iota(I32, (rows, SB_W), 0)
    c = lax.broadcasted_iota(I32, (rows, SB_W), 1)
    return (r // dec_seq) == (c // D_SB)


def _page_heads(page_ref):
    n_keys = page_ref.shape[0] // H_SB
    cols = [page_ref[pl.ds(h, n_keys, stride=H_SB), :] for h in range(H_SB)]
    return jnp.concatenate(cols, axis=1).astype(BF16)


def _sbd_kernel(pt_ref, qbd_ref, kn_ref, vn_ref, bias_ref, g_ref, *rest,
                dec_seq, page, n_steps, scale):
    del pt_ref
    pages = rest[:2 * PAGES_PER_STEP]
    o_ref = rest[2 * PAGES_PER_STEP]
    o_scr, carry_scr, new_scr = rest[2 * PAGES_PER_STEP + 1:]
    rows = H_SB * dec_seq
    j = pl.program_id(1)
    later_u = _later_matrix(page)
    bias = bias_ref[...]

    def sweep(k_bf, v_bf, mask):
        z = lax.dot_general(qbd_ref[...], k_bf, NT_DIMS, preferred_element_type=F32) * scale + bias
        o_blk, carry = _sb_block(z, mask, later_u, v_bf, carry_scr[:, 0:1])
        o_scr[...] += o_blk
        carry_scr[...] = jnp.broadcast_to(carry, carry_scr.shape)

    @pl.when(j == 0)
    def _():
        o_scr[...] = jnp.zeros_like(o_scr)
        carry_scr[...] = jnp.zeros_like(carry_scr)
        new_scr[...] = jnp.zeros_like(new_scr)
        new_scr[0, 0:dec_seq, :] = kn_ref[...]
        new_scr[1, 0:dec_seq, :] = vn_ref[...]
        r = lax.broadcasted_iota(I32, (rows, page), 0)
        s = lax.broadcasted_iota(I32, (rows, page), 1)
        sweep(new_scr[0].astype(BF16), new_scr[1].astype(BF16), s < (r % dec_seq))

    for i in range(PAGES_PER_STEP):
        sweep(_page_heads(pages[i]), _page_heads(pages[PAGES_PER_STEP + i]), None)

    @pl.when(j == n_steps - 1)
    def _():
        o_full = jnp.where(_head_block_mask(rows, dec_seq), o_scr[...], 0.0)
        o = o_full[:, 0:D_SB]
        for h in range(1, H_SB):
            o = o + o_full[:, h * D_SB:(h + 1) * D_SB]
        o = o * lax.rsqrt(jnp.mean(o * o, axis=-1, keepdims=True) + EPS) * g_ref[...]
        o_ref[...] = o


def _sb_paged(q_s, k_new, v_new, cache_k, cache_v, page_table, sb_bias, norm_sb_out):
    bsz, dec_seq, _ = q_s.shape
    n_pool, page = cache_k.shape[0], cache_k.shape[1]
    n_pages = page_table.shape[1]
    assert n_pages % PAGES_PER_STEP == 0
    n_steps = n_pages // PAGES_PER_STEP
    rows = H_SB * dec_seq
    ck = cache_k.reshape(n_pool, page * H_SB, D_SB)
    cv = cache_v.reshape(n_pool, page * H_SB, D_SB)
    bias_rows = jnp.broadcast_to(jnp.repeat(sb_bias, dec_seq)[:, None], (rows, page)).astype(F32)
    g_rows = jnp.repeat(norm_sb_out.reshape(H_SB, D_SB), dec_seq, axis=0)
    q_heads = q_s.reshape(bsz, dec_seq, H_SB, D_SB).swapaxes(1, 2)
    eye = jnp.eye(H_SB, dtype=F32)
    qbd = (q_heads[:, :, :, None, :] * eye[None, :, None, :, None]).reshape(bsz, rows, SB_W).astype(BF16)

    def page_spec(i):
        return pl.BlockSpec(
            (None, page * H_SB, D_SB),
            lambda b, j, pt: (pt[b, n_pages - 1 - (j * PAGES_PER_STEP + i)], 0, 0))

    tok_spec = pl.BlockSpec((None, dec_seq, SB_W), lambda b, j, pt: (b, 0, 0))
    kern = functools.partial(_sbd_kernel, dec_seq=dec_seq, page=page, n_steps=n_steps,
                             scale=D_SB ** -0.5)
    grid_spec = pltpu.PrefetchScalarGridSpec(
        num_scalar_prefetch=1,
        grid=(bsz, n_steps),
        in_specs=[pl.BlockSpec((None, rows, SB_W), lambda b, j, pt: (b, 0, 0)),
                  tok_spec, tok_spec,
                  pl.BlockSpec((rows, page), lambda b, j, pt: (0, 0)),
                  pl.BlockSpec((rows, D_SB), lambda b, j, pt: (0, 0))]
                 + [page_spec(i) for i in range(PAGES_PER_STEP)] * 2,
        out_specs=pl.BlockSpec((None, rows, D_SB), lambda b, j, pt: (b, 0, 0)),
        scratch_shapes=[pltpu.VMEM((rows, SB_W), F32),
                        pltpu.VMEM((rows, LANES), F32),
                        pltpu.VMEM((2, page, SB_W), F32)],
    )
    return pl.pallas_call(
        kern,
        out_shape=jax.ShapeDtypeStruct((bsz, rows, D_SB), F32),
        grid_spec=grid_spec,
        compiler_params=_cparams(("arbitrary", "arbitrary"), 48),
        name="sb_paged",
    )(page_table, qbd, k_new, v_new, bias_rows, g_rows,
      *([ck] * PAGES_PER_STEP), *([cv] * PAGES_PER_STEP))


def _mlstm_kernel(q_ref, k_ref, v_ref, og_ref, gcol_ref, grow_ref, c0_ref, n0_ref, m0_ref, gn_ref,
                  h_ref, c_out, n_out, m_out, c_scr, n_scr, m_scr, *, chunk, valid, n_chunks):
    ci = pl.program_id(1)

    @pl.when(ci == 0)
    def _():
        c_scr[...] = c0_ref[...]
        n_scr[...] = n0_ref[...]
        m_scr[...] = m0_ref[...]

    t_idx = lax.broadcasted_iota(I32, (chunk, chunk), 0)
    s_idx = lax.broadcasted_iota(I32, (chunk, chunk), 1)
    causal = s_idx <= t_idx
    col_valid = lax.broadcasted_iota(I32, (chunk, 1), 0) < valid
    row_valid = lax.broadcasted_iota(I32, (1, chunk), 1) < valid
    gcol = gcol_ref[...]
    grow = grow_ref[...]
    neg_inf = -jnp.inf

    for h in range(H_ML):
        q = q_ref[:, h * DQK_ML:(h + 1) * DQK_ML]
        ks = k_ref[:, h * DQK_ML:(h + 1) * DQK_ML] * (DQK_ML ** -0.5)
        v = v_ref[:, h * DV_ML:(h + 1) * DV_ML]
        q_bf, ks_bf, v_bf = q.astype(BF16), ks.astype(BF16), v.astype(BF16)

        i_col = jnp.where(col_valid, gcol[:, h:h + 1], neg_inf)
        i_row = jnp.where(row_valid, grow[h:h + 1, :], neg_inf)
        lf_col = jnp.where(col_valid, _log_sigmoid_pair(gcol[:, H_ML + h:H_ML + h + 1])[0], 0.0)
        lf_row = jnp.where(row_valid, _log_sigmoid_pair(grow[H_ML + h:H_ML + h + 1, :])[0], 0.0)
        bcum_col = jnp.sum(jnp.where(causal, lf_row, 0.0), axis=1, keepdims=True)
        bcum_row = jnp.sum(jnp.where(t_idx <= s_idx, lf_col, 0.0), axis=0, keepdims=True)

        log_w = jnp.where(causal, bcum_col - bcum_row + i_row, neg_inf)
        m0 = m_scr[h:h + 1, 0:1]
        m_state = bcum_col + m0
        m_new = jnp.maximum(m_state, jnp.max(log_w, axis=1, keepdims=True))
        w = jnp.exp(log_w - m_new)
        g = jnp.exp(m_state - m_new)

        s_mat = w * lax.dot_general(q_bf, ks_bf, NT_DIMS, preferred_element_type=F32)
        c0 = c_scr[h]
        n0 = n_scr[h:h + 1, :]
        num = (jnp.dot(s_mat.astype(BF16), v_bf, preferred_element_type=F32)
               + g * lax.dot_general(q_bf, c0.astype(BF16), NT_DIMS, preferred_element_type=F32))
        den = (jnp.sum(s_mat, axis=1, keepdims=True)
               + g * jnp.sum(q * n0, axis=1, keepdims=True))
        hh = num / jnp.maximum(jnp.abs(den), jnp.exp(-m_new))

        m_end = m_new[chunk - 1:chunk, :]
        b_last = bcum_col[chunk - 1:chunk, :]
        w_end = jnp.exp(b_last - bcum_col + i_col - m_end)
        g_end = jnp.exp(b_last + m0 - m_end)
        vw_bf = (v * w_end).astype(BF16)
        c_scr[h] = g_end * c0 + lax.dot_general(vw_bf, ks_bf, TN_DIMS, preferred_element_type=F32)
        n_scr[h:h + 1, :] = g_end * n0 + jnp.sum(ks * w_end, axis=0, keepdims=True)
        m_scr[h:h + 1, :] = jnp.broadcast_to(m_end, (1, LANES))

        hn = (hh * lax.rsqrt(jnp.mean(hh * hh, axis=-1, keepdims=True) + EPS)
              * gn_ref[:, h * DV_ML:(h + 1) * DV_ML])
        out = hn * jax.nn.sigmoid(og_ref[:, h * DV_ML:(h + 1) * DV_ML])
        h_ref[:, h * DV_ML:(h + 1) * DV_ML] = out.astype(h_ref.dtype)

    @pl.when(ci == n_chunks - 1)
    def _():
        c_out[...] = c_scr[...]
        n_out[...] = n_scr[...]
        m_out[...] = m_scr[...]


def _mlstm(proj, gates, c0, n0, m0, norm_ml_out, *, bsz, n_chunks, chunk, valid):
    rows = bsz * n_chunks * chunk
    grow = gates[:, :N_GATES].reshape(bsz, n_chunks, chunk, N_GATES).swapaxes(2, 3)
    m0_b = jnp.broadcast_to(m0[:, :, None], (bsz, H_ML, LANES)).astype(F32)
    qk_blk = ML_QK_W // LANES
    row_map = lambda b, c: (b * n_chunks + c)
    kern = functools.partial(_mlstm_kernel, chunk=chunk, valid=valid, n_chunks=n_chunks)
    state_specs = [pl.BlockSpec((None, H_ML, DV_ML, DQK_ML), lambda b, c: (b, 0, 0, 0)),
                   pl.BlockSpec((None, H_ML, DQK_ML), lambda b, c: (b, 0, 0)),
                   pl.BlockSpec((None, H_ML, LANES), lambda b, c: (b, 0, 0))]
    q_col = (3 * SB_W) // ML_QK_W
    v_col = (3 * SB_W + 2 * ML_QK_W) // ML_V_W
    h, c1, n1, m1 = pl.pallas_call(
        kern,
        out_shape=(jax.ShapeDtypeStruct((rows, ML_V_W), BF16),
                   jax.ShapeDtypeStruct((bsz, H_ML, DV_ML, DQK_ML), F32),
                   jax.ShapeDtypeStruct((bsz, H_ML, DQK_ML), F32),
                   jax.ShapeDtypeStruct((bsz, H_ML, LANES), F32)),
        grid=(bsz, n_chunks),
        in_specs=[pl.BlockSpec((chunk, ML_QK_W), lambda b, c: (row_map(b, c), q_col)),
                  pl.BlockSpec((chunk, ML_QK_W), lambda b, c: (row_map(b, c), q_col + 1)),
                  pl.BlockSpec((chunk, ML_V_W), lambda b, c: (row_map(b, c), v_col)),
                  pl.BlockSpec((chunk, ML_V_W), lambda b, c: (row_map(b, c), v_col + 1)),
                  pl.BlockSpec((chunk, LANES), lambda b, c: (row_map(b, c), 0)),
                  pl.BlockSpec((None, None, N_GATES, chunk), lambda b, c: (b, c, 0, 0))]
                 + state_specs
                 + [pl.BlockSpec((1, ML_V_W), lambda b, c: (0, 0))],
        out_specs=(pl.BlockSpec((chunk, ML_V_W), lambda b, c: (row_map(b, c), 0)),) + tuple(state_specs),
        scratch_shapes=[pltpu.VMEM((H_ML, DV_ML, DQK_ML), F32),
                        pltpu.VMEM((H_ML, DQK_ML), F32),
                        pltpu.VMEM((H_ML, LANES), F32)],
        compiler_params=_cparams(("arbitrary", "arbitrary"), 40),
        name="mlstm",
    )(proj, proj, proj, proj, gates, grow, c0, n0, m0_b, norm_ml_out.reshape(1, ML_V_W))
    del qk_blk
    return h, c1, n1, m1[:, :, 0]


def _outproj_kernel(osb_ref, hml_ref, w_ref, x_ref, gt_ref, sh_ref, sc_ref, g_ref, wr_ref, br_ref,
                    *rest, tn, n_col):
    h_ref, xf_ref, idx_ref, gate_ref, cat_scr, h_scr = rest[-6:]
    j = pl.program_id(1)

    @pl.when(j == 0)
    def _():
        cat_scr[:, 0:SB_W] = osb_ref[...]
        cat_scr[:, SB_W:SB_W + ML_V_W] = hml_ref[...]

    mix = jnp.dot(cat_scr[...], w_ref[...].astype(BF16), preferred_element_type=F32)
    h_blk = x_ref[...] + gt_ref[...] * mix
    h_ref[...] = h_blk
    for jj in range(n_col):
        @pl.when(j == jj)
        def _(jj=jj):
            h_scr[:, jj * tn:(jj + 1) * tn] = h_blk

    @pl.when(j == n_col - 1)
    def _():
        hf = h_scr[...]
        xn = hf * lax.rsqrt(jnp.mean(hf * hf, axis=-1, keepdims=True) + EPS) * g_ref[...]
        xf = xn * (1.0 + sc_ref[...]) + sh_ref[...]
        xf_ref[...] = xf
        logits = _dot3(xf, wr_ref[...]) + br_ref[...]
        lane = lax.broadcasted_iota(I32, logits.shape, 1)
        lane_f = lane.astype(F32)
        neg_inf = -jnp.inf
        lg = jnp.where(lane < N_EXPERTS, logits, neg_inf)
        vals, idxs = [], []
        for _k in range(TOP_K):
            mx = jnp.max(lg, axis=1, keepdims=True)
            ix = jnp.min(jnp.where(lg == mx, lane_f, float(LANES)), axis=1, keepdims=True)
            vals.append(mx)
            idxs.append(ix)
            lg = jnp.where(lane_f == ix, neg_inf, lg)
        es = [jnp.exp(vk - vals[0]) for vk in vals]
        tot = es[0] + es[1] + es[2] + es[3]
        idx_out = jnp.zeros(logits.shape, F32)
        gate_out = jnp.zeros(logits.shape, F32)
        for k in range(TOP_K):
            idx_out = jnp.where(lane == k, idxs[k], idx_out)
            gate_out = jnp.where(lane == k, es[k] / tot, gate_out)
        idx_ref[...] = idx_out.astype(I32)
        gate_ref[...] = gate_out


def _outproj(o_sb, h_ml, w_out, x2d, gate_a, shift_f, scale_f, norm_ffn, wr_pad, br_pad,
             bufs, *, n_tok, tm, rows_per_group, row_off):
    t, d = x2d.shape
    tn = 512
    n_col = d // tn
    off = row_off // tm
    kern = functools.partial(_outproj_kernel, tn=tn, n_col=n_col)
    n_fixed = 10
    alias_specs, alias_args, aliases = [], (), {}
    if bufs is not None:
        alias_specs = [pl.BlockSpec(memory_space=pl.ANY)] * 2
        alias_args = tuple(bufs)
        aliases = {n_fixed: 0, n_fixed + 1: 1}
    return pl.pallas_call(
        kern,
        out_shape=(jax.ShapeDtypeStruct((n_tok, d), F32),
                   jax.ShapeDtypeStruct((n_tok, d), F32),
                   jax.ShapeDtypeStruct((t, LANES), I32),
                   jax.ShapeDtypeStruct((t, LANES), F32)),
        grid=(t // tm, n_col),
        in_specs=[pl.BlockSpec((tm, SB_W), lambda i, j: (i, 0)),
                  pl.BlockSpec((tm, ML_V_W), lambda i, j: (i, 0)),
                  pl.BlockSpec((SB_W + ML_V_W, tn), lambda i, j: (0, j)),
                  pl.BlockSpec((tm, tn), lambda i, j: (i, j)),
                  _mod_spec(gate_a, tm, rows_per_group, col_tile=tn),
                  _mod_spec(shift_f, tm, rows_per_group),
                  _mod_spec(scale_f, tm, rows_per_group),
                  pl.BlockSpec((1, d), lambda i, j: (0, 0)),
                  pl.BlockSpec((d, LANES), lambda i, j: (0, 0)),
                  pl.BlockSpec((1, LANES), lambda i, j: (0, 0))] + alias_specs,
        out_specs=(pl.BlockSpec((tm, tn), lambda i, j: (off + i, j)),
                   pl.BlockSpec((tm, d), lambda i, j: (off + i, 0)),
                   pl.BlockSpec((tm, LANES), lambda i, j: (i, 0)),
                   pl.BlockSpec((tm, LANES), lambda i, j: (i, 0))),
        scratch_shapes=[pltpu.VMEM((tm, SB_W + ML_V_W), BF16),
                        pltpu.VMEM((tm, d), F32)],
        input_output_aliases=aliases,
        compiler_params=_cparams(("arbitrary", "arbitrary"), 48),
        name="outproj",
    )(o_sb, h_ml, w_out, x2d, gate_a, shift_f, scale_f, norm_ffn.reshape(1, d), wr_pad, br_pad,
      *alias_args)


def _gather_kernel(tok_ref, x_hbm, o_ref, buf, sem):
    base = pl.program_id(0) * MOE_TILE

    def issue(r, carry):
        tok = tok_ref[base + r]
        pltpu.make_async_copy(x_hbm.at[pl.ds(tok, 1), :], buf.at[pl.ds(r, 1), :], sem).start()
        return carry

    lax.fori_loop(0, MOE_TILE, issue, 0, unroll=8)
    pltpu.make_async_copy(x_hbm.at[pl.ds(0, MOE_TILE), :], buf, sem).wait()
    o_ref[...] = buf[...].astype(o_ref.dtype)


def _gather_rows(row_tok, xf_buf):
    n_rows = row_tok.shape[0]
    d = xf_buf.shape[1]
    grid_spec = pltpu.PrefetchScalarGridSpec(
        num_scalar_prefetch=1,
        grid=(n_rows // MOE_TILE,),
        in_specs=[pl.BlockSpec(memory_space=pl.ANY)],
        out_specs=pl.BlockSpec((MOE_TILE, d), lambda i, tok: (i, 0)),
        scratch_shapes=[pltpu.VMEM((MOE_TILE, d), F32), pltpu.SemaphoreType.DMA],
    )
    return pl.pallas_call(
        _gather_kernel,
        out_shape=jax.ShapeDtypeStruct((n_rows, d), BF16),
        grid_spec=grid_spec,
        compiler_params=_cparams(("arbitrary",), 32),
        name="moe_gather",
    )(row_tok, xf_buf)


def _moe_kernel(e_ref, start_ref, tiles_ref, x_hbm, wu_ref, bu_ref, wd_ref, bd_ref, y_hbm,
                x_buf, acc, wu_bf, wd_perm, wd_bf, sem_in, sem_out, *, n_ff, d_model):
    del e_ref
    it = pl.program_id(0)
    f = pl.program_id(1)
    n_tiles = tiles_ref[it]
    row0 = pl.multiple_of(start_ref[it] * MOE_TILE, MOE_TILE)
    item_rows = MOE_ITEM_TILES * MOE_TILE
    ff2 = 2 * MOE_FF_TILE
    half = LANES // 2

    @pl.when(n_tiles > 0)
    def _():
        @pl.when(f == 0)
        def _():
            cp = pltpu.make_async_copy(x_hbm.at[pl.ds(row0, item_rows), :], x_buf, sem_in)
            cp.start()
            cp.wait()

        wu_bf[...] = wu_ref[...].astype(BF16)
        for c in range(MOE_FF_TILE // LANES):
            for s in range(d_model // LANES):
                for par in range(2):
                    src = wd_ref[c * LANES + par * half:c * LANES + (par + 1) * half,
                                 s * LANES:(s + 1) * LANES]
                    wd_perm[s, pl.ds(c * LANES + par, half, stride=2), :] = src
        for s in range(d_model // LANES):
            wd_bf[:, s * LANES:(s + 1) * LANES] = wd_perm[s].astype(BF16)

        even = (lax.broadcasted_iota(I32, (MOE_TILE, LANES), 1) % 2) == 0
        b_up = bu_ref[...]
        b_dn = bd_ref[...]

        def row_tile(t, carry):
            r0 = pl.multiple_of(t * MOE_TILE, MOE_TILE)
            xt = x_buf[pl.ds(r0, MOE_TILE), :]
            hu = jnp.dot(xt, wu_bf[...], preferred_element_type=F32) + b_up
            acts = []
            for c in range(MOE_FF_TILE // LANES):
                a_blk = hu[:, (2 * c) * LANES:(2 * c + 1) * LANES]
                b_blk = hu[:, (2 * c + 1) * LANES:(2 * c + 2) * LANES]
                gate = jnp.where(even, a_blk, pltpu.roll(b_blk, 1, 1))
                up = jnp.where(even, pltpu.roll(a_blk, LANES - 1, 1), b_blk)
                gate = jnp.minimum(gate, SWIGLU_LIMIT)
                up = jnp.clip(up, -SWIGLU_LIMIT, SWIGLU_LIMIT)
                acts.append((up + 1.0) * gate * jax.nn.sigmoid(SWIGLU_ALPHA * gate))
            act = jnp.concatenate(acts, axis=1).astype(BF16)
            contrib = jnp.dot(act, wd_bf[...], preferred_element_type=F32)

            @pl.when(f == 0)
            def _():
                acc[pl.ds(r0, MOE_TILE), :] = contrib + b_dn

            @pl.when(f != 0)
            def _():
                acc[pl.ds(r0, MOE_TILE), :] += contrib

            return carry

        lax.fori_loop(0, n_tiles, row_tile, 0)

        @pl.when(f == n_ff - 1)
        def _():
            def out_copy(t):
                r0 = pl.multiple_of(t * MOE_TILE, MOE_TILE)
                return pltpu.make_async_copy(acc.at[pl.ds(r0, MOE_TILE), :],
                                             y_hbm.at[pl.ds(row0 + r0, MOE_TILE), :], sem_out)

            def start(t, carry):
                out_copy(t).start()
                return carry

            def wait(t, carry):
                out_copy(t).wait()
                return carry

            lax.fori_loop(0, n_tiles, start, 0)
            lax.fori_loop(0, n_tiles, wait, 0)


def _moe(item_e, item_start, item_tiles, x_sorted, w_up, b_up, w_down, b_down):
    n_rows, d = x_sorted.shape
    n_exp, _, ff2_total = w_up.shape
    d_ff = ff2_total // 2
    n_ff = d_ff // MOE_FF_TILE
    n_items = item_e.shape[0]
    item_rows = MOE_ITEM_TILES * MOE_TILE
    kern = functools.partial(_moe_kernel, n_ff=n_ff, d_model=d)

    def ff_tile(i, f, n):
        return jnp.where(n[i] > 0, f, n_ff - 1)

    grid_spec = pltpu.PrefetchScalarGridSpec(
        num_scalar_prefetch=3,
        grid=(n_items, n_ff),
        in_specs=[pl.BlockSpec(memory_space=pl.ANY),
                  pl.BlockSpec((None, d, 2 * MOE_FF_TILE),
                               lambda i, f, e, s, n: (e[i], 0, ff_tile(i, f, n))),
                  pl.BlockSpec((None, 1, 2 * MOE_FF_TILE),
                               lambda i, f, e, s, n: (e[i], 0, ff_tile(i, f, n))),
                  pl.BlockSpec((None, MOE_FF_TILE, d),
                               lambda i, f, e, s, n: (e[i], ff_tile(i, f, n), 0)),
                  pl.BlockSpec((None, 1, d), lambda i, f, e, s, n: (e[i], 0, 0))],
        out_specs=pl.BlockSpec(memory_space=pl.ANY),
        scratch_shapes=[pltpu.VMEM((item_rows, d), BF16),
                        pltpu.VMEM((item_rows, d), F32),
                        pltpu.VMEM((d, 2 * MOE_FF_TILE), BF16),
                        pltpu.VMEM((d // LANES, MOE_FF_TILE, LANES), F32),
                        pltpu.VMEM((MOE_FF_TILE, d), BF16),
                        pltpu.SemaphoreType.DMA,
                        pltpu.SemaphoreType.DMA],
    )
    return pl.pallas_call(
        kern,
        out_shape=jax.ShapeDtypeStruct((n_rows, d), F32),
        grid_spec=grid_spec,
        compiler_params=_cparams(("arbitrary", "arbitrary"), 56),
        name="moe_experts",
    )(item_e, item_start, item_tiles, x_sorted, w_up,
      b_up.reshape(n_exp, 1, ff2_total), w_down, b_down.reshape(n_exp, 1, d))


def _combine_kernel(pos_ref, y_hbm, h_ref, gate_ref, gt_ref, g_ref, o_ref, buf, sem, *, tok_off):
    base = (pl.program_id(0) * COMBINE_TOK + tok_off) * TOP_K

    def issue(tok, carry):
        for k in range(TOP_K):
            p = pos_ref[base + tok * TOP_K + k]
            pltpu.make_async_copy(y_hbm.at[pl.ds(p, 1), :],
                                  buf.at[pl.ds(k * COMBINE_TOK + tok, 1), :], sem).start()
        return carry

    lax.fori_loop(0, COMBINE_TOK, issue, 0, unroll=2)
    pltpu.make_async_copy(y_hbm.at[pl.ds(0, COMBINE_TOK * TOP_K), :], buf, sem).wait()
    gates = gate_ref[...]
    ffn = jnp.zeros(h_ref.shape, F32)
    for k in range(TOP_K):
        ffn = ffn + gates[:, k:k + 1] * buf[k * COMBINE_TOK:(k + 1) * COMBINE_TOK, :]
    y = h_ref[...] + gt_ref[...] * ffn
    o_ref[...] = y * lax.rsqrt(jnp.mean(y * y, axis=-1, keepdims=True) + EPS) * g_ref[...]


def _combine(pos, y_rows, h_buf, gate, gate_f, norm_final, *, n_tok, tok_off, rows_per_group):
    d = h_buf.shape[1]
    off = tok_off // COMBINE_TOK
    kern = functools.partial(_combine_kernel, tok_off=tok_off)
    grid_spec = pltpu.PrefetchScalarGridSpec(
        num_scalar_prefetch=1,
        grid=(n_tok // COMBINE_TOK,),
        in_specs=[pl.BlockSpec(memory_space=pl.ANY),
                  pl.BlockSpec((COMBINE_TOK, d), lambda i, p: (off + i, 0)),
                  pl.BlockSpec((COMBINE_TOK, LANES), lambda i, p: (off + i, 0)),
                  _mod_spec(gate_f, COMBINE_TOK, rows_per_group),
                  pl.BlockSpec((1, d), lambda i, p: (0, 0))],
        out_specs=pl.BlockSpec((COMBINE_TOK, d), lambda i, p: (i, 0)),
        scratch_shapes=[pltpu.VMEM((COMBINE_TOK * TOP_K, d), F32), pltpu.SemaphoreType.DMA],
    )
    return pl.pallas_call(
        kern,
        out_shape=jax.ShapeDtypeStruct((n_tok, d), F32),
        grid_spec=grid_spec,
        compiler_params=_cparams(("arbitrary",), 32),
        name="moe_combine",
    )(pos, y_rows, h_buf, gate, gate_f, norm_final.reshape(1, d))


def _routing_tables(top_idx):
    n_tok = top_idx.shape[0]
    n_assign = n_tok * TOP_K
    flat_e = top_idx.reshape(-1)
    onehot = (flat_e[:, None] == jnp.arange(N_EXPERTS, dtype=I32)[None, :]).astype(I32)
    running = jnp.cumsum(onehot, axis=0)
    rank = jnp.take_along_axis(running, flat_e[:, None], axis=1)[:, 0] - 1
    counts = running[-1]
    tiles_e = (counts + MOE_TILE - 1) // MOE_TILE
    tile_start_e = jnp.cumsum(tiles_e) - tiles_e
    pos = tile_start_e[flat_e] * MOE_TILE + rank

    n_tiles_max = -(-n_assign // MOE_TILE) + N_EXPERTS
    n_rows = (n_tiles_max + MOE_ITEM_TILES) * MOE_TILE
    row_tok = jnp.zeros((n_rows,), I32).at[pos].set(jnp.arange(n_assign, dtype=I32) // TOP_K)

    n_items_max = N_EXPERTS + n_tiles_max // MOE_ITEM_TILES
    items_e = (tiles_e + MOE_ITEM_TILES - 1) // MOE_ITEM_TILES
    items_end = jnp.cumsum(items_e)
    it = jnp.arange(n_items_max, dtype=I32)
    e_of = jnp.minimum(jnp.searchsorted(items_end, it, side="right"), N_EXPERTS - 1).astype(I32)
    local = it - (items_end[e_of] - items_e[e_of])
    valid = it < items_end[-1]
    tiles_left = tiles_e[e_of] - local * MOE_ITEM_TILES
    item_tiles = jnp.where(valid, jnp.clip(tiles_left, 0, MOE_ITEM_TILES), 0).astype(I32)
    item_start = jnp.where(valid, tile_start_e[e_of] + local * MOE_ITEM_TILES, 0).astype(I32)
    last_e = e_of[jnp.maximum(items_end[-1] - 1, 0)]
    item_e = jnp.where(valid, e_of, last_e).astype(I32)
    return pos.astype(I32), row_tok, item_e, item_start, item_tiles


def _pick_tile(n, pref):
    t = min(n, pref)
    while n % t:
        t //= 2
    return t


def kernel(x_prompt, x_sample, c_prompt, c_sample, cache_k, cache_v, page_table, state_C, state_n, state_m, w_ada, b_ada, norm_mix, norm_ffn, w_in, b_gates, sb_bias, norm_sb_out, norm_ml_out, w_out, w_router, b_router, w_up, b_up, w_down, b_down, norm_final):
    bsz, seq, d = x_prompt.shape
    dbs, dec_seq, _ = x_sample.shape
    depth = w_ada.shape[0]
    assert depth == 1
    n_p, n_s = bsz * seq, dbs * dec_seq
    n_tok = n_p + n_s

    (w_ada, b_ada, norm_mix, norm_ffn, w_in, b_gates, sb_bias, norm_sb_out, norm_ml_out, w_out,
     w_router, b_router, w_up, b_up, w_down, b_down) = [
        a[0] for a in (w_ada, b_ada, norm_mix, norm_ffn, w_in, b_gates, sb_bias, norm_sb_out,
                       norm_ml_out, w_out, w_router, b_router, w_up, b_up, w_down, b_down)]

    n_c = bsz + dbs
    n_c_pad = -(-n_c // SUBLANES) * SUBLANES
    c_all = jnp.concatenate([c_prompt, c_sample, jnp.zeros((n_c_pad - n_c, d), F32)], axis=0)
    mod = _ada(c_all, w_ada, b_ada)
    mods_p = [mod[:bsz, i * d:(i + 1) * d].reshape(bsz, 1, d) for i in range(N_MOD)]
    mods_s = [jnp.repeat(mod[bsz:n_c, i * d:(i + 1) * d], dec_seq, axis=0).reshape(1, n_s, d)
              for i in range(N_MOD)]

    wg_pad = jnp.pad(w_in[:, PROJ_MAIN_W:], ((0, 0), (0, LANES - N_GATES)))
    bg_pad = jnp.pad(b_gates, (0, LANES - N_GATES)).reshape(1, LANES)
    wr_pad = jnp.pad(w_router, ((0, 0), (0, LANES - N_EXPERTS)))
    br_pad = jnp.pad(b_router, (0, LANES - N_EXPERTS)).reshape(1, LANES)

    xp2 = x_prompt.reshape(n_p, d)
    xs2 = x_sample.reshape(n_s, d)
    tm_p = _pick_tile(seq, 1024)

    proj_p, gates_p = _inproj(xp2, mods_p[0], mods_p[1], norm_mix, w_in, wg_pad, bg_pad,
                              tm=tm_p, rows_per_group=seq)
    osb_p = _sb_prompt(proj_p, sb_bias, norm_sb_out, bsz=bsz, seq=seq)
    chunk_p = _pick_tile(seq, ML_CHUNK)
    hml_p, c_p, nn_p, m_p = _mlstm(
        proj_p, gates_p,
        jnp.zeros((bsz, H_ML, DV_ML, DQK_ML), F32), jnp.zeros((bsz, H_ML, DQK_ML), F32),
        jnp.zeros((bsz, H_ML), F32), norm_ml_out,
        bsz=bsz, n_chunks=seq // chunk_p, chunk=chunk_p, valid=chunk_p)

    proj_s, gates_s = _inproj(xs2, mods_s[0], mods_s[1], norm_mix, w_in, wg_pad, bg_pad,
                              tm=n_s, rows_per_group=n_s)
    qkv_s = proj_s[:, :3 * SB_W].reshape(dbs, dec_seq, 3, SB_W)
    osb_s = _sb_paged(qkv_s[:, :, 0], qkv_s[:, :, 1], qkv_s[:, :, 2], cache_k[0], cache_v[0],
                      page_table, sb_bias, norm_sb_out)
    osb_s = (osb_s.reshape(dbs, H_SB, dec_seq, D_SB).swapaxes(1, 2)
             .reshape(n_s, SB_W).astype(BF16))
    chunk_s = -(-dec_seq // BF16_SUBLANES) * BF16_SUBLANES
    pad_rows = lambda a: jnp.pad(a.reshape(dbs, dec_seq, a.shape[-1]),
                                 ((0, 0), (0, chunk_s - dec_seq), (0, 0))).reshape(dbs * chunk_s, a.shape[-1])
    hml_s, c_s, nn_s, m_s = _mlstm(
        pad_rows(proj_s), pad_rows(gates_s), state_C[0], state_n[0], state_m[0], norm_ml_out,
        bsz=dbs, n_chunks=1, chunk=chunk_s, valid=dec_seq)
    hml_s = hml_s.reshape(dbs, chunk_s, ML_V_W)[:, :dec_seq].reshape(n_s, ML_V_W)

    tm_o = _pick_tile(seq, 512)
    h_buf, xf_buf, idx_p, gate_p = _outproj(
        osb_p, hml_p, w_out, xp2, mods_p[2], mods_p[3], mods_p[4], norm_ffn, wr_pad, br_pad,
        None, n_tok=n_tok, tm=tm_o, rows_per_group=seq, row_off=0)
    h_buf, xf_buf, idx_s, gate_s = _outproj(
        osb_s, hml_s, w_out, xs2, mods_s[2], mods_s[3], mods_s[4], norm_ffn, wr_pad, br_pad,
        (h_buf, xf_buf), n_tok=n_tok, tm=n_s, rows_per_group=n_s, row_off=n_p)
    top_idx = jnp.concatenate([idx_p[:, :TOP_K], idx_s[:, :TOP_K]], axis=0)
    gate_all = jnp.concatenate([gate_p, gate_s], axis=0)

    pos, row_tok, item_e, item_start, item_tiles = _routing_tables(top_idx)
    x_sorted = _gather_rows(row_tok, xf_buf)
    y_rows = _moe(item_e, item_start, item_tiles, x_sorted, w_up, b_up, w_down, b_down)
    y_p = _combine(pos, y_rows, h_buf, gate_all, mods_p[5], norm_final,
                   n_tok=n_p, tok_off=0, rows_per_group=seq)
    y_s = _combine(pos, y_rows, h_buf, gate_all, mods_s[5], norm_final,
                   n_tok=n_s, tok_off=n_p, rows_per_group=n_s)

    k_p = proj_p[:, SB_W:2 * SB_W].reshape(1, bsz, seq, H_SB, D_SB)
    v_p = proj_p[:, 2 * SB_W:3 * SB_W].reshape(1, bsz, seq, H_SB, D_SB)
    k_s = qkv_s[:, :, 1].reshape(1, dbs, dec_seq, H_SB, D_SB)
    v_s = qkv_s[:, :, 2].reshape(1, dbs, dec_seq, H_SB, D_SB)
    return (y_p.reshape(bsz, seq, d), y_s.reshape(dbs, dec_seq, d),
            k_p, v_p, k_s, v_s,
            c_p[None], nn_p[None], m_p[None], c_s[None], nn_s[None], m_s[None])
```

```python
import functools

import jax
import jax.numpy as jnp
from jax import lax
from jax.experimental import pallas as pl
from jax.experimental.pallas import tpu as pltpu

F32 = jnp.float32
BF16 = jnp.bfloat16
I32 = jnp.int32

H_SB = 8
D_SB = 128
SB_W = H_SB * D_SB
H_ML = 4
DQK_ML = 128
DV_ML = 256
ML_QK_W = H_ML * DQK_ML
ML_V_W = H_ML * DV_ML
PROJ_MAIN_W = 3 * SB_W + 2 * ML_QK_W + 2 * ML_V_W
N_GATES = 2 * H_ML
N_EXPERTS = 32
TOP_K = 4
N_MOD = 6
SWIGLU_LIMIT = 7.0
SWIGLU_ALPHA = 1.702
EPS = 1e-6

LANES = 128
SUBLANES = 8
BF16_SUBLANES = 16
MIB = 1024 * 1024

MOE_TILE = 128
MOE_ITEM_TILES = 12
MOE_FF_TILE = 256
ML_CHUNK = 128
SB_BLOCK = 512
SB_SUB = 256
PAGES_PER_STEP = 8
COMBINE_TOK = 64

NT_DIMS = (((1,), (1,)), ((), ()))
TN_DIMS = (((0,), (0,)), ((), ()))


def _cparams(sem, vmem_mib):
    return pltpu.CompilerParams(dimension_semantics=sem, vmem_limit_bytes=vmem_mib * MIB)


def _log_sigmoid_pair(z):
    t = jnp.log(1.0 + jnp.exp(-jnp.abs(z)))
    return jnp.minimum(z, 0.0) - t, -jnp.maximum(z, 0.0) - t


def _split_bf16(x):
    hi = x.astype(BF16)
    lo = (x - hi.astype(F32)).astype(BF16)
    return hi, lo


def _dot3(x, w):
    xh, xl = _split_bf16(x)
    wh, wl = _split_bf16(w)
    return (jnp.dot(xh, wh, preferred_element_type=F32)
            + (jnp.dot(xl, wh, preferred_element_type=F32)
               + jnp.dot(xh, wl, preferred_element_type=F32)))


def _mod_spec(arr, tile, rows_per_group, col_tile=None):
    per_group = arr.shape[1] == 1
    width = arr.shape[2] if col_tile is None else col_tile

    def index_map(i, *rest):
        col = 0 if col_tile is None else rest[0]
        if per_group:
            return ((i * tile) // rows_per_group, 0, col)
        return (0, i, col)

    return pl.BlockSpec((None, 1 if per_group else tile, width), index_map)


def _ada_kernel(c_ref, w_ref, b_ref, o_ref):
    c = c_ref[...]
    o_ref[...] = _dot3(c * jax.nn.sigmoid(c), w_ref[...]) + b_ref[...]


def _ada(c_all, w_ada, b_ada):
    n, d = c_all.shape
    w_out = w_ada.shape[1]
    tn = 1024
    return pl.pallas_call(
        _ada_kernel,
        out_shape=jax.ShapeDtypeStruct((n, w_out), F32),
        grid=(w_out // tn,),
        in_specs=[pl.BlockSpec((n, d), lambda j: (0, 0)),
                  pl.BlockSpec((d, tn), lambda j: (0, j)),
                  pl.BlockSpec((1, tn), lambda j: (0, j))],
        out_specs=pl.BlockSpec((n, tn), lambda j: (0, j)),
        compiler_params=_cparams(("arbitrary",), 40),
        name="ada",
    )(c_all, w_ada, b_ada.reshape(1, w_out))


def _inproj_kernel(x_ref, sh_ref, sc_ref, g_ref, w_ref, wg_ref, bg_ref,
                   proj_ref, gates_ref, xm_scr):
    @pl.when(pl.program_id(1) == 0)
    def _():
        x = x_ref[...]
        xn = x * lax.rsqrt(jnp.mean(x * x, axis=-1, keepdims=True) + EPS) * g_ref[...]
        xm = (xn * (1.0 + sc_ref[...]) + sh_ref[...]).astype(BF16)
        xm_scr[...] = xm
        gates_ref[...] = jnp.dot(xm, wg_ref[...].astype(BF16),
                                 preferred_element_type=F32) + bg_ref[...]

    proj_ref[...] = jnp.dot(xm_scr[...], w_ref[...].astype(BF16),
                            preferred_element_type=F32)


def _inproj(x2d, shift, scale, norm_w, w_in, wg_pad, bg_pad, *, tm, rows_per_group):
    t, d = x2d.shape
    tn = 512
    mod_spec = _mod_spec(shift, tm, rows_per_group)
    return pl.pallas_call(
        _inproj_kernel,
        out_shape=(jax.ShapeDtypeStruct((t, PROJ_MAIN_W), F32),
                   jax.ShapeDtypeStruct((t, LANES), F32)),
        grid=(t // tm, PROJ_MAIN_W // tn),
        in_specs=[pl.BlockSpec((tm, d), lambda i, j: (i, 0)),
                  mod_spec, mod_spec,
                  pl.BlockSpec((1, d), lambda i, j: (0, 0)),
                  pl.BlockSpec((d, tn), lambda i, j: (0, j)),
                  pl.BlockSpec((d, LANES), lambda i, j: (0, 0)),
                  pl.BlockSpec((1, LANES), lambda i, j: (0, 0))],
        out_specs=(pl.BlockSpec((tm, tn), lambda i, j: (i, j)),
                   pl.BlockSpec((tm, LANES), lambda i, j: (i, 0))),
        scratch_shapes=[pltpu.VMEM((tm, d), BF16)],
        compiler_params=_cparams(("arbitrary", "arbitrary"), 48),
        name="inproj",
    )(x2d, shift, scale, norm_w.reshape(1, d), w_in, wg_pad, bg_pad)


def _sb_block(z, mask, later_u, v_bf, carry):
    log_beta, log_1m = _log_sigmoid_pair(z)
    if mask is not None:
        log_1m = jnp.where(mask, log_1m, 0.0)
    hi, lo = _split_bf16(log_1m)
    later = (jnp.dot(hi, later_u, preferred_element_type=F32)
             + jnp.dot(lo, later_u, preferred_element_type=F32))
    a = jnp.exp(log_beta + later + carry)
    if mask is not None:
        a = jnp.where(mask, a, 0.0)
    o = jnp.dot(a.astype(BF16), v_bf, preferred_element_type=F32)
    return o, carry + jnp.sum(log_1m, axis=-1, keepdims=True)


def _later_matrix(n):
    j = lax.broadcasted_iota(I32, (n, n), 0)
    s = lax.broadcasted_iota(I32, (n, n), 1)
    return jnp.where(j > s, 1.0, 0.0).astype(BF16)


def _sbp_kernel(bias_ref, q_ref, k_ref, v_ref, g_ref, o_ref, o_scr, carry_scr, *, blk, sub, scale):
    h = pl.program_id(1)
    qi = pl.program_id(2)
    bias = bias_ref[h]
    q = q_ref[...].astype(BF16)
    later_u = _later_matrix(sub)

    def strip(k_start, width, diagonal):
        k_bf = k_ref[pl.ds(k_start, width), :].astype(BF16)
        v_bf = v_ref[pl.ds(k_start, width), :].astype(BF16)
        z = lax.dot_general(q, k_bf, NT_DIMS, preferred_element_type=F32) * scale + bias
        log_beta, log_1m = _log_sigmoid_pair(z)
        if diagonal:
            t = lax.broadcasted_iota(I32, (blk, width), 0)
            s = lax.broadcasted_iota(I32, (blk, width), 1)
            mask = s < t
            log_1m = jnp.where(mask, log_1m, 0.0)
        hi, lo = _split_bf16(log_1m)
        n_sub = width // sub
        laters, sums = [], []
        for j in range(n_sub):
            cols = slice(j * sub, (j + 1) * sub)
            laters.append(jnp.dot(hi[:, cols], later_u, preferred_element_type=F32)
                          + jnp.dot(lo[:, cols], later_u, preferred_element_type=F32))
            sums.append(jnp.sum(log_1m[:, cols], axis=-1, keepdims=True))
        carry = carry_scr[:, 0:1]
        parts = [None] * n_sub
        for j in reversed(range(n_sub)):
            cols = slice(j * sub, (j + 1) * sub)
            parts[j] = jnp.exp(log_beta[:, cols] + laters[j] + carry)
            carry = carry + sums[j]
        a = jnp.concatenate(parts, axis=1)
        if diagonal:
            a = jnp.where(mask, a, 0.0)
        o_scr[...] += jnp.dot(a.astype(BF16), v_bf, preferred_element_type=F32)
        carry_scr[...] = jnp.broadcast_to(carry, carry_scr.shape)

    o_scr[...] = jnp.zeros_like(o_scr)
    carry_scr[...] = jnp.zeros_like(carry_scr)
    strip(pl.multiple_of(qi * blk, blk), blk, True)

    @pl.when(qi % 2 == 1)
    def _():
        strip(pl.multiple_of((qi - 1) * blk, blk), blk, False)

    def older(n, carry):
        strip(pl.multiple_of((qi // 2 - 1 - n) * (2 * blk), 2 * blk), 2 * blk, False)
        return carry

    lax.fori_loop(0, qi // 2, older, 0)
    o = o_scr[...]
    o = o * lax.rsqrt(jnp.mean(o * o, axis=-1, keepdims=True) + EPS) * g_ref[...]
    o_ref[...] = o.astype(o_ref.dtype)


def _sb_prompt(proj, sb_bias, norm_sb_out, *, bsz, seq):
    blk = min(SB_BLOCK, seq)
    nq = seq // blk
    kern = functools.partial(_sbp_kernel, blk=blk, sub=min(SB_SUB, blk), scale=D_SB ** -0.5)
    return pl.pallas_call(
        kern,
        out_shape=jax.ShapeDtypeStruct((bsz * seq, SB_W), BF16),
        grid=(bsz, H_SB, nq),
        in_specs=[pl.BlockSpec(memory_space=pltpu.SMEM),
                  pl.BlockSpec((blk, D_SB), lambda b, h, i: (b * nq + i, h)),
                  pl.BlockSpec((seq, D_SB), lambda b, h, i: (b, H_SB + h)),
                  pl.BlockSpec((seq, D_SB), lambda b, h, i: (b, 2 * H_SB + h)),
                  pl.BlockSpec((None, 1, D_SB), lambda b, h, i: (h, 0, 0))],
        out_specs=pl.BlockSpec((blk, D_SB), lambda b, h, i: (b * nq + i, h)),
        scratch_shapes=[pltpu.VMEM((blk, D_SB), F32), pltpu.VMEM((blk, LANES), F32)],
        compiler_params=_cparams(("arbitrary", "arbitrary", "arbitrary"), 48),
        name="sb_prompt",
    )(sb_bias, proj, proj, proj, norm_sb_out.reshape(H_SB, 1, D_SB))


def _head_block_mask(rows, dec_seq):
    r = lax.broadcasted_iota(I32, (rows, SB_W), 0)
    c = lax.broadcasted_iota(I32, (rows, SB_W), 1)
    return (r // dec_seq) == (c // D_SB)


def _page_heads(page_ref):
    n_keys = page_ref.shape[0] // H_SB
    cols = [page_ref[pl.ds(h, n_keys, stride=H_SB), :] for h in range(H_SB)]
    return jnp.concatenate(cols, axis=1).astype(BF16)


def _sbd_kernel(pt_ref, qbd_ref, kn_ref, vn_ref, bias_ref, g_ref, *rest,
                dec_seq, page, n_steps, scale):
    del pt_ref
    pages = rest[:2 * PAGES_PER_STEP]
    o_ref = rest[2 * PAGES_PER_STEP]
    o_scr, carry_scr, new_scr = rest[2 * PAGES_PER_STEP + 1:]
    rows = H_SB * dec_seq
    j = pl.program_id(1)
    later_u = _later_matrix(page)
    bias = bias_ref[...]

    def sweep(k_bf, v_bf, mask):
        z = lax.dot_general(qbd_ref[...], k_bf, NT_DIMS, preferred_element_type=F32) * scale + bias
        o_blk, carry = _sb_block(z, mask, later_u, v_bf, carry_scr[:, 0:1])
        o_scr[...] += o_blk
        carry_scr[...] = jnp.broadcast_to(carry, carry_scr.shape)

    @pl.when(j == 0)
    def _():
        o_scr[...] = jnp.zeros_like(o_scr)
        carry_scr[...] = jnp.zeros_like(carry_scr)
        new_scr[...] = jnp.zeros_like(new_scr)
        new_scr[0, 0:dec_seq, :] = kn_ref[...]
        new_scr[1, 0:dec_seq, :] = vn_ref[...]
        r = lax.broadcasted_iota(I32, (rows, page), 0)
        s = lax.broadcasted_iota(I32, (rows, page), 1)
        sweep(new_scr[0].astype(BF16), new_scr[1].astype(BF16), s < (r % dec_seq))

    for i in range(PAGES_PER_STEP):
        sweep(_page_heads(pages[i]), _page_heads(pages[PAGES_PER_STEP + i]), None)

    @pl.when(j == n_steps - 1)
    def _():
        o_full = jnp.where(_head_block_mask(rows, dec_seq), o_scr[...], 0.0)
        o = o_full[:, 0:D_SB]
        for h in range(1, H_SB):
            o = o + o_full[:, h * D_SB:(h + 1) * D_SB]
        o = o * lax.rsqrt(jnp.mean(o * o, axis=-1, keepdims=True) + EPS) * g_ref[...]
        o_ref[...] = o


def _sb_paged(q_s, k_new, v_new, cache_k, cache_v, page_table, sb_bias, norm_sb_out):
    bsz, dec_seq, _ = q_s.shape
    n_pool, page = cache_k.shape[0], cache_k.shape[1]
    n_pages = page_table.shape[1]
    assert n_pages % PAGES_PER_STEP == 0
    n_steps = n_pages // PAGES_PER_STEP
    rows = H_SB * dec_seq
    ck = cache_k.reshape(n_pool, page * H_SB, D_SB)
    cv = cache_v.reshape(n_pool, page * H_SB, D_SB)
    bias_rows = jnp.broadcast_to(jnp.repeat(sb_bias, dec_seq)[:, None], (rows, page)).astype(F32)
    g_rows = jnp.repeat(norm_sb_out.reshape(H_SB, D_SB), dec_seq, axis=0)
    q_heads = q_s.reshape(bsz, dec_seq, H_SB, D_SB).swapaxes(1, 2)
    eye = jnp.eye(H_SB, dtype=F32)
    qbd = (q_heads[:, :, :, None, :] * eye[None, :, None, :, None]).reshape(bsz, rows, SB_W).astype(BF16)

    def page_spec(i):
        return pl.BlockSpec(
            (None, page * H_SB, D_SB),
            lambda b, j, pt: (pt[b, n_pages - 1 - (j * PAGES_PER_STEP + i)], 0, 0))

    tok_spec = pl.BlockSpec((None, dec_seq, SB_W), lambda b, j, pt: (b, 0, 0))
    kern = functools.partial(_sbd_kernel, dec_seq=dec_seq, page=page, n_steps=n_steps,
                             scale=D_SB ** -0.5)
    grid_spec = pltpu.PrefetchScalarGridSpec(
        num_scalar_prefetch=1,
        grid=(bsz, n_steps),
        in_specs=[pl.BlockSpec((None, rows, SB_W), lambda b, j, pt: (b, 0, 0)),
                  tok_spec, tok_spec,
                  pl.BlockSpec((rows, page), lambda b, j, pt: (0, 0)),
                  pl.BlockSpec((rows, D_SB), lambda b, j, pt: (0, 0))]
                 + [page_spec(i) for i in range(PAGES_PER_STEP)] * 2,
        out_specs=pl.BlockSpec((None, rows, D_SB), lambda b, j, pt: (b, 0, 0)),
        scratch_shapes=[pltpu.VMEM((rows, SB_W), F32),
                        pltpu.VMEM((rows, LANES), F32),
                        pltpu.VMEM((2, page, SB_W), F32)],
    )
    return pl.pallas_call(
        kern,
        out_shape=jax.ShapeDtypeStruct((bsz, rows, D_SB), F32),
        grid_spec=grid_spec,
        compiler_params=_cparams(("arbitrary", "arbitrary"), 48),
        name="sb_paged",
    )(page_table, qbd, k_new, v_new, bias_rows, g_rows,
      *([ck] * PAGES_PER_STEP), *([cv] * PAGES_PER_STEP))


def _mlstm_kernel(q_ref, k_ref, v_ref, og_ref, gcol_ref, grow_ref, c0_ref, n0_ref, m0_ref, gn_ref,
                  h_ref, c_out, n_out, m_out, c_scr, n_scr, m_scr, *, chunk, valid, n_chunks):
    ci = pl.program_id(1)

    @pl.when(ci == 0)
    def _():
        c_scr[...] = c0_ref[...]
        n_scr[...] = n0_ref[...]
        m_scr[...] = m0_ref[...]

    t_idx = lax.broadcasted_iota(I32, (chunk, chunk), 0)
    s_idx = lax.broadcasted_iota(I32, (chunk, chunk), 1)
    causal = s_idx <= t_idx
    col_valid = lax.broadcasted_iota(I32, (chunk, 1), 0) < valid
    row_valid = lax.broadcasted_iota(I32, (1, chunk), 1) < valid
    gcol = gcol_ref[...]
    grow = grow_ref[...]
    neg_inf = -jnp.inf

    for h in range(H_ML):
        q = q_ref[:, h * DQK_ML:(h + 1) * DQK_ML]
        ks = k_ref[:, h * DQK_ML:(h + 1) * DQK_ML] * (DQK_ML ** -0.5)
        v = v_ref[:, h * DV_ML:(h + 1) * DV_ML]
        q_bf, ks_bf, v_bf = q.astype(BF16), ks.astype(BF16), v.astype(BF16)

        i_col = jnp.where(col_valid, gcol[:, h:h + 1], neg_inf)
        i_row = jnp.where(row_valid, grow[h:h + 1, :], neg_inf)
        lf_col = jnp.where(col_valid, _log_sigmoid_pair(gcol[:, H_ML + h:H_ML + h + 1])[0], 0.0)
        lf_row = jnp.where(row_valid, _log_sigmoid_pair(grow[H_ML + h:H_ML + h + 1, :])[0], 0.0)
        bcum_col = jnp.sum(jnp.where(causal, lf_row, 0.0), axis=1, keepdims=True)
        bcum_row = jnp.sum(jnp.where(t_idx <= s_idx, lf_col, 0.0), axis=0, keepdims=True)

        log_w = jnp.where(causal, bcum_col - bcum_row + i_row, neg_inf)
        m0 = m_scr[h:h + 1, 0:1]
        m_state = bcum_col + m0
        m_new = jnp.maximum(m_state, jnp.max(log_w, axis=1, keepdims=True))
        w = jnp.exp(log_w - m_new)
        g = jnp.exp(m_state - m_new)

        s_mat = w * lax.dot_general(q_bf, ks_bf, NT_DIMS, preferred_element_type=F32)
        c0 = c_scr[h]
        n0 = n_scr[h:h + 1, :]
        num = (jnp.dot(s_mat.astype(BF16), v_bf, preferred_element_type=F32)
               + g * lax.dot_general(q_bf, c0.astype(BF16), NT_DIMS, preferred_element_type=F32))
        den = (jnp.sum(s_mat, axis=1, keepdims=True)
               + g * jnp.sum(q * n0, axis=1, keepdims=True))
        hh = num / jnp.maximum(jnp.abs(den), jnp.exp(-m_new))

        m_end = m_new[chunk - 1:chunk, :]
        b_last = bcum_col[chunk - 1:chunk, :]
        w_end = jnp.exp(b_last - bcum_col + i_col - m_end)
        g_end = jnp.exp(b_last + m0 - m_end)
        vw_bf = (v * w_end).astype(BF16)
        c_scr[h] = g_end * c0 + lax.dot_general(vw_bf, ks_bf, TN_DIMS, preferred_element_type=F32)
        n_scr[h:h + 1, :] = g_end * n0 + jnp.sum(ks * w_end, axis=0, keepdims=True)
        m_scr[h:h + 1, :] = jnp.broadcast_to(m_end, (1, LANES))

        hn = (hh * lax.rsqrt(jnp.mean(hh * hh, axis=-1, keepdims=True) + EPS)
              * gn_ref[:, h * DV_ML:(h + 1) * DV_ML])
        out = hn * jax.nn.sigmoid(og_ref[:, h * DV_ML:(h + 1) * DV_ML])
        h_ref[:, h * DV_ML:(h + 1) * DV_ML] = out.astype(h_ref.dtype)

    @pl.when(ci == n_chunks - 1)
    def _():
        c_out[...] = c_scr[...]
        n_out[...] = n_scr[...]
        m_out[...] = m_scr[...]


def _mlstm(proj, gates, c0, n0, m0, norm_ml_out, *, bsz, n_chunks, chunk, valid):
    rows = bsz * n_chunks * chunk
    grow = gates[:, :N_GATES].reshape(bsz, n_chunks, chunk, N_GATES).swapaxes(2, 3)
    m0_b = jnp.broadcast_to(m0[:, :, None], (bsz, H_ML, LANES)).astype(F32)
    qk_blk = ML_QK_W // LANES
    row_map = lambda b, c: (b * n_chunks + c)
    kern = functools.partial(_mlstm_kernel, chunk=chunk, valid=valid, n_chunks=n_chunks)
    state_specs = [pl.BlockSpec((None, H_ML, DV_ML, DQK_ML), lambda b, c: (b, 0, 0, 0)),
                   pl.BlockSpec((None, H_ML, DQK_ML), lambda b, c: (b, 0, 0)),
                   pl.BlockSpec((None, H_ML, LANES), lambda b, c: (b, 0, 0))]
    q_col = (3 * SB_W) // ML_QK_W
    v_col = (3 * SB_W + 2 * ML_QK_W) // ML_V_W
    h, c1, n1, m1 = pl.pallas_call(
        kern,
        out_shape=(jax.ShapeDtypeStruct((rows, ML_V_W), BF16),
                   jax.ShapeDtypeStruct((bsz, H_ML, DV_ML, DQK_ML), F32),
                   jax.ShapeDtypeStruct((bsz, H_ML, DQK_ML), F32),
                   jax.ShapeDtypeStruct((bsz, H_ML, LANES), F32)),
        grid=(bsz, n_chunks),
        in_specs=[pl.BlockSpec((chunk, ML_QK_W), lambda b, c: (row_map(b, c), q_col)),
                  pl.BlockSpec((chunk, ML_QK_W), lambda b, c: (row_map(b, c), q_col + 1)),
                  pl.BlockSpec((chunk, ML_V_W), lambda b, c: (row_map(b, c), v_col)),
                  pl.BlockSpec((chunk, ML_V_W), lambda b, c: (row_map(b, c), v_col + 1)),
                  pl.BlockSpec((chunk, LANES), lambda b, c: (row_map(b, c), 0)),
                  pl.BlockSpec((None, None, N_GATES, chunk), lambda b, c: (b, c, 0, 0))]
                 + state_specs
                 + [pl.BlockSpec((1, ML_V_W), lambda b, c: (0, 0))],
        out_specs=(pl.BlockSpec((chunk, ML_V_W), lambda b, c: (row_map(b, c), 0)),) + tuple(state_specs),
        scratch_shapes=[pltpu.VMEM((H_ML, DV_ML, DQK_ML), F32),
                        pltpu.VMEM((H_ML, DQK_ML), F32),
                        pltpu.VMEM((H_ML, LANES), F32)],
        compiler_params=_cparams(("arbitrary", "arbitrary"), 40),
        name="mlstm",
    )(proj, proj, proj, proj, gates, grow, c0, n0, m0_b, norm_ml_out.reshape(1, ML_V_W))
    del qk_blk
    return h, c1, n1, m1[:, :, 0]


def _outproj_kernel(osb_ref, hml_ref, w_ref, x_ref, gt_ref, sh_ref, sc_ref, g_ref, wr_ref, br_ref,
                    *rest, tn, n_col):
    h_ref, xf_ref, idx_ref, gate_ref, cat_scr, h_scr = rest[-6:]
    j = pl.program_id(1)

    @pl.when(j == 0)
    def _():
        cat_scr[:, 0:SB_W] = osb_ref[...]
        cat_scr[:, SB_W:SB_W + ML_V_W] = hml_ref[...]

    mix = jnp.dot(cat_scr[...], w_ref[...].astype(BF16), preferred_element_type=F32)
    h_blk = x_ref[...] + gt_ref[...] * mix
    h_ref[...] = h_blk
    for jj in range(n_col):
        @pl.when(j == jj)
        def _(jj=jj):
            h_scr[:, jj * tn:(jj + 1) * tn] = h_blk

    @pl.when(j == n_col - 1)
    def _():
        hf = h_scr[...]
        xn = hf * lax.rsqrt(jnp.mean(hf * hf, axis=-1, keepdims=True) + EPS) * g_ref[...]
        xf = xn * (1.0 + sc_ref[...]) + sh_ref[...]
        xf_ref[...] = xf
        logits = _dot3(xf, wr_ref[...]) + br_ref[...]
        lane = lax.broadcasted_iota(I32, logits.shape, 1)
        lane_f = lane.astype(F32)
        neg_inf = -jnp.inf
        lg = jnp.where(lane < N_EXPERTS, logits, neg_inf)
        vals, idxs = [], []
        for _k in range(TOP_K):
            mx = jnp.max(lg, axis=1, keepdims=True)
            ix = jnp.min(jnp.where(lg == mx, lane_f, float(LANES)), axis=1, keepdims=True)
            vals.append(mx)
            idxs.append(ix)
            lg = jnp.where(lane_f == ix, neg_inf, lg)
        es = [jnp.exp(vk - vals[0]) for vk in vals]
        tot = es[0] + es[1] + es[2] + es[3]
        idx_out = jnp.zeros(logits.shape, F32)
        gate_out = jnp.zeros(logits.shape, F32)
        for k in range(TOP_K):
            idx_out = jnp.where(lane == k, idxs[k], idx_out)
            gate_out = jnp.where(lane == k, es[k] / tot, gate_out)
        idx_ref[...] = idx_out.astype(I32)
        gate_ref[...] = gate_out


def _outproj(o_sb, h_ml, w_out, x2d, gate_a, shift_f, scale_f, norm_ffn, wr_pad, br_pad,
             bufs, *, n_tok, tm, rows_per_group, row_off):
    t, d = x2d.shape
    tn = 512
    n_col = d // tn
    off = row_off // tm
    kern = functools.partial(_outproj_kernel, tn=tn, n_col=n_col)
    n_fixed = 10
    alias_specs, alias_args, aliases = [], (), {}
    if bufs is not None:
        alias_specs = [pl.BlockSpec(memory_space=pl.ANY)] * 2
        alias_args = tuple(bufs)
        aliases = {n_fixed: 0, n_fixed + 1: 1}
    return pl.pallas_call(
        kern,
        out_shape=(jax.ShapeDtypeStruct((n_tok, d), F32),
                   jax.ShapeDtypeStruct((n_tok, d), F32),
                   jax.ShapeDtypeStruct((t, LANES), I32),
                   jax.ShapeDtypeStruct((t, LANES), F32)),
        grid=(t // tm, n_col),
        in_specs=[pl.BlockSpec((tm, SB_W), lambda i, j: (i, 0)),
                  pl.BlockSpec((tm, ML_V_W), lambda i, j: (i, 0)),
                  pl.BlockSpec((SB_W + ML_V_W, tn), lambda i, j: (0, j)),
                  pl.BlockSpec((tm, tn), lambda i, j: (i, j)),
                  _mod_spec(gate_a, tm, rows_per_group, col_tile=tn),
                  _mod_spec(shift_f, tm, rows_per_group),
                  _mod_spec(scale_f, tm, rows_per_group),
                  pl.BlockSpec((1, d), lambda i, j: (0, 0)),
                  pl.BlockSpec((d, LANES), lambda i, j: (0, 0)),
                  pl.BlockSpec((1, LANES), lambda i, j: (0, 0))] + alias_specs,
        out_specs=(pl.BlockSpec((tm, tn), lambda i, j: (off + i, j)),
                   pl.BlockSpec((tm, d), lambda i, j: (off + i, 0)),
                   pl.BlockSpec((tm, LANES), lambda i, j: (i, 0)),
                   pl.BlockSpec((tm, LANES), lambda i, j: (i, 0))),
        scratch_shapes=[pltpu.VMEM((tm, SB_W + ML_V_W), BF16),
                        pltpu.VMEM((tm, d), F32)],
        input_output_aliases=aliases,
        compiler_params=_cparams(("arbitrary", "arbitrary"), 48),
        name="outproj",
    )(o_sb, h_ml, w_out, x2d, gate_a, shift_f, scale_f, norm_ffn.reshape(1, d), wr_pad, br_pad,
      *alias_args)


def _gather_kernel(tok_ref, x_hbm, o_ref, buf, sem):
    i = pl.program_id(0)

    def issue(tile, slot):
        def row(r, carry):
            tok = tok_ref[tile * MOE_TILE + r]
            pltpu.make_async_copy(x_hbm.at[pl.ds(tok, 1), :], buf.at[slot, pl.ds(r, 1), :],
                                  sem.at[slot]).start()
            return carry

        lax.fori_loop(0, MOE_TILE, row, 0, unroll=8)

    @pl.when(i == 0)
    def _():
        issue(0, 0)

    @pl.when(i + 1 < pl.num_programs(0))
    def _():
        issue(i + 1, (i + 1) % 2)

    slot = i % 2
    pltpu.make_async_copy(x_hbm.at[pl.ds(0, MOE_TILE), :], buf.at[slot], sem.at[slot]).wait()
    o_ref[...] = buf[slot].astype(o_ref.dtype)


def _gather_rows(row_tok, xf_buf):
    n_rows = row_tok.shape[0]
    d = xf_buf.shape[1]
    grid_spec = pltpu.PrefetchScalarGridSpec(
        num_scalar_prefetch=1,
        grid=(n_rows // MOE_TILE,),
        in_specs=[pl.BlockSpec(memory_space=pl.ANY)],
        out_specs=pl.BlockSpec((MOE_TILE, d), lambda i, tok: (i, 0)),
        scratch_shapes=[pltpu.VMEM((2, MOE_TILE, d), F32), pltpu.SemaphoreType.DMA((2,))],
    )
    return pl.pallas_call(
        _gather_kernel,
        out_shape=jax.ShapeDtypeStruct((n_rows, d), BF16),
        grid_spec=grid_spec,
        compiler_params=_cparams(("arbitrary",), 32),
        name="moe_gather",
    )(row_tok, xf_buf)


def _moe_kernel(e_ref, start_ref, tiles_ref, x_hbm, wu_ref, bu_ref, wd_ref, bd_ref, y_hbm,
                x_buf, acc, wu_bf, wd_perm, wd_bf, sem_in, sem_out, *, n_ff, d_model, n_items):
    del e_ref
    it = pl.program_id(0)
    f = pl.program_id(1)
    n_tiles = tiles_ref[it]
    row0 = pl.multiple_of(start_ref[it] * MOE_TILE, MOE_TILE)
    half = LANES // 2
    slot = it % 2

    def tile_rows(t):
        return pl.ds(pl.multiple_of(t * MOE_TILE, MOE_TILE), MOE_TILE)

    def x_copy(item, t):
        src0 = pl.multiple_of(start_ref[item] * MOE_TILE, MOE_TILE)
        return pltpu.make_async_copy(x_hbm.at[pl.ds(src0 + t * MOE_TILE, MOE_TILE), :],
                                     x_buf.at[item % 2, tile_rows(t), :], sem_in.at[item % 2])

    def y_copy(t):
        return pltpu.make_async_copy(acc.at[tile_rows(t), :],
                                     y_hbm.at[pl.ds(row0 + t * MOE_TILE, MOE_TILE), :], sem_out)

    def for_tiles(count, fn):
        def body(t, carry):
            fn(t)
            return carry
        lax.fori_loop(0, count, body, 0)

    @pl.when(n_tiles > 0)
    def _():
        @pl.when(f == 0)
        def _():
            @pl.when(it == 0)
            def _():
                for_tiles(n_tiles, lambda t: x_copy(it, t).start())

            for_tiles(n_tiles, lambda t: x_copy(it, t).wait())
            nxt = jnp.minimum(it + 1, n_items - 1)
            n_next = jnp.where(it + 1 < n_items, tiles_ref[nxt], 0)
            for_tiles(n_next, lambda t: x_copy(nxt, t).start())
            b_dn = jnp.broadcast_to(bd_ref[...], (MOE_TILE, d_model))

            def init(t):
                acc[tile_rows(t), :] = b_dn

            for_tiles(n_tiles, init)

        wu_bf[...] = wu_ref[...].astype(BF16)
        for c in range(MOE_FF_TILE // LANES):
            for s in range(d_model // LANES):
                for par in range(2):
                    src = wd_ref[c * LANES + par * half:c * LANES + (par + 1) * half,
                                 s * LANES:(s + 1) * LANES]
                    wd_perm[s, pl.ds(c * LANES + par, half, stride=2), :] = src
        for s in range(d_model // LANES):
            wd_bf[:, s * LANES:(s + 1) * LANES] = wd_perm[s].astype(BF16)

        even = (lax.broadcasted_iota(I32, (MOE_TILE, LANES), 1) % 2) == 0
        b_up = bu_ref[...]

        def tile_compute(t):
            xt = x_buf[slot, tile_rows(t), :]
            hu = jnp.dot(xt, wu_bf[...], preferred_element_type=F32) + b_up
            acts = []
            for c in range(MOE_FF_TILE // LANES):
                a_blk = hu[:, (2 * c) * LANES:(2 * c + 1) * LANES]
                b_blk = hu[:, (2 * c + 1) * LANES:(2 * c + 2) * LANES]
                gate = jnp.where(even, a_blk, pltpu.roll(b_blk, 1, 1))
                up = jnp.where(even, pltpu.roll(a_blk, LANES - 1, 1), b_blk)
                gate = jnp.minimum(gate, SWIGLU_LIMIT)
                up = jnp.clip(up, -SWIGLU_LIMIT, SWIGLU_LIMIT)
                acts.append((up + 1.0) * gate * jax.nn.sigmoid(SWIGLU_ALPHA * gate))
            act = jnp.concatenate(acts, axis=1).astype(BF16)
            acc[tile_rows(t), :] += jnp.dot(act, wd_bf[...], preferred_element_type=F32)

        def run_tiles(after_tile):
            def pair(p, carry):
                tile_compute(2 * p)
                tile_compute(2 * p + 1)
                after_tile(2 * p)
                after_tile(2 * p + 1)
                return carry

            lax.fori_loop(0, n_tiles // 2, pair, 0)

            @pl.when(n_tiles % 2 == 1)
            def _():
                tile_compute(n_tiles - 1)
                after_tile(n_tiles - 1)

        @pl.when(f < n_ff - 1)
        def _():
            run_tiles(lambda t: None)

        @pl.when(f == n_ff - 1)
        def _():
            run_tiles(lambda t: y_copy(t).start())
            for_tiles(n_tiles, lambda t: y_copy(t).wait())


def _moe(item_e, item_start, item_tiles, x_sorted, w_up, b_up, w_down, b_down):
    n_rows, d = x_sorted.shape
    n_exp, _, ff2_total = w_up.shape
    d_ff = ff2_total // 2
    n_ff = d_ff // MOE_FF_TILE
    n_items = item_e.shape[0]
    item_rows = MOE_ITEM_TILES * MOE_TILE
    kern = functools.partial(_moe_kernel, n_ff=n_ff, d_model=d, n_items=n_items)

    def ff_tile(i, f, n):
        return jnp.where(n[i] > 0, f, n_ff - 1)

    grid_spec = pltpu.PrefetchScalarGridSpec(
        num_scalar_prefetch=3,
        grid=(n_items, n_ff),
        in_specs=[pl.BlockSpec(memory_space=pl.ANY),
                  pl.BlockSpec((None, d, 2 * MOE_FF_TILE),
                               lambda i, f, e, s, n: (e[i], 0, ff_tile(i, f, n))),
                  pl.BlockSpec((None, 1, 2 * MOE_FF_TILE),
                               lambda i, f, e, s, n: (e[i], 0, ff_tile(i, f, n))),
                  pl.BlockSpec((None, MOE_FF_TILE, d),
                               lambda i, f, e, s, n: (e[i], ff_tile(i, f, n), 0)),
                  pl.BlockSpec((None, 1, d), lambda i, f, e, s, n: (e[i], 0, 0))],
        out_specs=pl.BlockSpec(memory_space=pl.ANY),
        scratch_shapes=[pltpu.VMEM((2, item_rows, d), BF16),
                        pltpu.VMEM((item_rows, d), F32),
                        pltpu.VMEM((d, 2 * MOE_FF_TILE), BF16),
                        pltpu.VMEM((d // LANES, MOE_FF_TILE, LANES), F32),
                        pltpu.VMEM((MOE_FF_TILE, d), BF16),
                        pltpu.SemaphoreType.DMA((2,)),
                        pltpu.SemaphoreType.DMA],
    )
    return pl.pallas_call(
        kern,
        out_shape=jax.ShapeDtypeStruct((n_rows, d), F32),
        grid_spec=grid_spec,
        compiler_params=_cparams(("arbitrary", "arbitrary"), 56),
        name="moe_experts",
    )(item_e, item_start, item_tiles, x_sorted, w_up,
      b_up.reshape(n_exp, 1, ff2_total), w_down, b_down.reshape(n_exp, 1, d))


def _combine_kernel(pos_ref, y_hbm, h_ref, gate_ref, gt_ref, g_ref, o_ref, buf, sem, *, tok_off):
    i = pl.program_id(0)

    def issue(step, slot):
        base = (step * COMBINE_TOK + tok_off) * TOP_K

        def token(tok, carry):
            for k in range(TOP_K):
                p = pos_ref[base + tok * TOP_K + k]
                pltpu.make_async_copy(y_hbm.at[pl.ds(p, 1), :],
                                      buf.at[slot, pl.ds(k * COMBINE_TOK + tok, 1), :],
                                      sem.at[slot]).start()
            return carry

        lax.fori_loop(0, COMBINE_TOK, token, 0, unroll=2)

    @pl.when(i == 0)
    def _():
        issue(0, 0)

    @pl.when(i + 1 < pl.num_programs(0))
    def _():
        issue(i + 1, (i + 1) % 2)

    slot = i % 2
    pltpu.make_async_copy(y_hbm.at[pl.ds(0, COMBINE_TOK * TOP_K), :], buf.at[slot],
                          sem.at[slot]).wait()
    gates = gate_ref[...]
    ffn = jnp.zeros(h_ref.shape, F32)
    for k in range(TOP_K):
        ffn = ffn + gates[:, k:k + 1] * buf[slot, k * COMBINE_TOK:(k + 1) * COMBINE_TOK, :]
    y = h_ref[...] + gt_ref[...] * ffn
    o_ref[...] = y * lax.rsqrt(jnp.mean(y * y, axis=-1, keepdims=True) + EPS) * g_ref[...]


def _combine(pos, y_rows, h_buf, gate, gate_f, norm_final, *, n_tok, tok_off, rows_per_group):
    d = h_buf.shape[1]
    off = tok_off // COMBINE_TOK
    kern = functools.partial(_combine_kernel, tok_off=tok_off)
    grid_spec = pltpu.PrefetchScalarGridSpec(
        num_scalar_prefetch=1,
        grid=(n_tok // COMBINE_TOK,),
        in_specs=[pl.BlockSpec(memory_space=pl.ANY),
                  pl.BlockSpec((COMBINE_TOK, d), lambda i, p: (off + i, 0)),
                  pl.BlockSpec((COMBINE_TOK, LANES), lambda i, p: (off + i, 0)),
                  _mod_spec(gate_f, COMBINE_TOK, rows_per_group),
                  pl.BlockSpec((1, d), lambda i, p: (0, 0))],
        out_specs=pl.BlockSpec((COMBINE_TOK, d), lambda i, p: (i, 0)),
        scratch_shapes=[pltpu.VMEM((2, COMBINE_TOK * TOP_K, d), F32),
                        pltpu.SemaphoreType.DMA((2,))],
    )
    return pl.pallas_call(
        kern,
        out_shape=jax.ShapeDtypeStruct((n_tok, d), F32),
        grid_spec=grid_spec,
        compiler_params=_cparams(("arbitrary",), 32),
        name="moe_combine",
    )(pos, y_rows, h_buf, gate, gate_f, norm_final.reshape(1, d))


def _routing_tables(top_idx):
    n_tok = top_idx.shape[0]
    n_assign = n_tok * TOP_K
    flat_e = top_idx.reshape(-1)
    onehot = (flat_e[:, None] == jnp.arange(N_EXPERTS, dtype=I32)[None, :]).astype(I32)
    running = jnp.cumsum(onehot, axis=0)
    rank = jnp.take_along_axis(running, flat_e[:, None], axis=1)[:, 0] - 1
    counts = running[-1]
    tiles_e = (counts + MOE_TILE - 1) // MOE_TILE
    tile_start_e = jnp.cumsum(tiles_e) - tiles_e
    pos = tile_start_e[flat_e] * MOE_TILE + rank

    n_tiles_max = -(-n_assign // MOE_TILE) + N_EXPERTS
    n_rows = n_tiles_max * MOE_TILE
    row_tok = jnp.zeros((n_rows,), I32).at[pos].set(jnp.arange(n_assign, dtype=I32) // TOP_K)

    n_items_max = N_EXPERTS + n_tiles_max // MOE_ITEM_TILES
    items_e = (tiles_e + MOE_ITEM_TILES - 1) // MOE_ITEM_TILES
    items_end = jnp.cumsum(items_e)
    it = jnp.arange(n_items_max, dtype=I32)
    e_of = jnp.minimum(jnp.sum((it[:, None] >= items_end[None, :]).astype(I32), axis=1), N_EXPERTS - 1)
    local = it - (items_end[e_of] - items_e[e_of])
    valid = it < items_end[-1]
    tiles_left = tiles_e[e_of] - local * MOE_ITEM_TILES
    item_tiles = jnp.where(valid, jnp.clip(tiles_left, 0, MOE_ITEM_TILES), 0).astype(I32)
    item_start = jnp.where(valid, tile_start_e[e_of] + local * MOE_ITEM_TILES, 0).astype(I32)
    last_e = e_of[jnp.maximum(items_end[-1] - 1, 0)]
    item_e = jnp.where(valid, e_of, last_e).astype(I32)
    return pos.astype(I32), row_tok, item_e, item_start, item_tiles


def _pick_tile(n, pref):
    t = min(n, pref)
    while n % t:
        t //= 2
    return t


def kernel(x_prompt, x_sample, c_prompt, c_sample, cache_k, cache_v, page_table, state_C, state_n, state_m, w_ada, b_ada, norm_mix, norm_ffn, w_in, b_gates, sb_bias, norm_sb_out, norm_ml_out, w_out, w_router, b_router, w_up, b_up, w_down, b_down, norm_final):
    bsz, seq, d = x_prompt.shape
    dbs, dec_seq, _ = x_sample.shape
    depth = w_ada.shape[0]
    assert depth == 1
    n_p, n_s = bsz * seq, dbs * dec_seq
    n_tok = n_p + n_s

    (w_ada, b_ada, norm_mix, norm_ffn, w_in, b_gates, sb_bias, norm_sb_out, norm_ml_out, w_out,
     w_router, b_router, w_up, b_up, w_down, b_down) = [
        a[0] for a in (w_ada, b_ada, norm_mix, norm_ffn, w_in, b_gates, sb_bias, norm_sb_out,
                       norm_ml_out, w_out, w_router, b_router, w_up, b_up, w_down, b_down)]

    n_c = bsz + dbs
    n_c_pad = -(-n_c // SUBLANES) * SUBLANES
    c_all = jnp.concatenate([c_prompt, c_sample, jnp.zeros((n_c_pad - n_c, d), F32)], axis=0)
    mod = _ada(c_all, w_ada, b_ada)
    mods_p = [mod[:bsz, i * d:(i + 1) * d].reshape(bsz, 1, d) for i in range(N_MOD)]
    mods_s = [jnp.repeat(mod[bsz:n_c, i * d:(i + 1) * d], dec_seq, axis=0).reshape(1, n_s, d)
              for i in range(N_MOD)]

    wg_pad = jnp.pad(w_in[:, PROJ_MAIN_W:], ((0, 0), (0, LANES - N_GATES)))
    bg_pad = jnp.pad(b_gates, (0, LANES - N_GATES)).reshape(1, LANES)
    wr_pad = jnp.pad(w_router, ((0, 0), (0, LANES - N_EXPERTS)))
    br_pad = jnp.pad(b_router, (0, LANES - N_EXPERTS)).reshape(1, LANES)

    xp2 = x_prompt.reshape(n_p, d)
    xs2 = x_sample.reshape(n_s, d)
    tm_p = _pick_tile(seq, 1024)

    proj_p, gates_p = _inproj(xp2, mods_p[0], mods_p[1], norm_mix, w_in, wg_pad, bg_pad,
                              tm=tm_p, rows_per_group=seq)
    osb_p = _sb_prompt(proj_p, sb_bias, norm_sb_out, bsz=bsz, seq=seq)
    chunk_p = _pick_tile(seq, ML_CHUNK)
    hml_p, c_p, nn_p, m_p = _mlstm(
        proj_p, gates_p,
        jnp.zeros((bsz, H_ML, DV_ML, DQK_ML), F32), jnp.zeros((bsz, H_ML, DQK_ML), F32),
        jnp.zeros((bsz, H_ML), F32), norm_ml_out,
        bsz=bsz, n_chunks=seq // chunk_p, chunk=chunk_p, valid=chunk_p)

    proj_s, gates_s = _inproj(xs2, mods_s[0], mods_s[1], norm_mix, w_in, wg_pad, bg_pad,
                              tm=n_s, rows_per_group=n_s)
    qkv_s = proj_s[:, :3 * SB_W].reshape(dbs, dec_seq, 3, SB_W)
    osb_s = _sb_paged(qkv_s[:, :, 0], qkv_s[:, :, 1], qkv_s[:, :, 2], cache_k[0], cache_v[0],
                      page_table, sb_bias, norm_sb_out)
    osb_s = (osb_s.reshape(dbs, H_SB, dec_seq, D_SB).swapaxes(1, 2)
             .reshape(n_s, SB_W).astype(BF16))
    chunk_s = -(-dec_seq // BF16_SUBLANES) * BF16_SUBLANES
    pad_rows = lambda a: jnp.pad(a.reshape(dbs, dec_seq, a.shape[-1]),
                                 ((0, 0), (0, chunk_s - dec_seq), (0, 0))).reshape(dbs * chunk_s, a.shape[-1])
    hml_s, c_s, nn_s, m_s = _mlstm(
        pad_rows(proj_s), pad_rows(gates_s), state_C[0], state_n[0], state_m[0], norm_ml_out,
        bsz=dbs, n_chunks=1, chunk=chunk_s, valid=dec_seq)
    hml_s = hml_s.reshape(dbs, chunk_s, ML_V_W)[:, :dec_seq].reshape(n_s, ML_V_W)

    tm_o = _pick_tile(seq, 512)
    h_buf, xf_buf, idx_p, gate_p = _outproj(
        osb_p, hml_p, w_out, xp2, mods_p[2], mods_p[3], mods_p[4], norm_ffn, wr_pad, br_pad,
        None, n_tok=n_tok, tm=tm_o, rows_per_group=seq, row_off=0)
    h_buf, xf_buf, idx_s, gate_s = _outproj(
        osb_s, hml_s, w_out, xs2, mods_s[2], mods_s[3], mods_s[4], norm_ffn, wr_pad, br_pad,
        (h_buf, xf_buf), n_tok=n_tok, tm=n_s, rows_per_group=n_s, row_off=n_p)
    top_idx = jnp.concatenate([idx_p[:, :TOP_K], idx_s[:, :TOP_K]], axis=0)
    gate_all = jnp.concatenate([gate_p, gate_s], axis=0)

    pos, row_tok, item_e, item_start, item_tiles = _routing_tables(top_idx)
    x_sorted = _gather_rows(row_tok, xf_buf)
    y_rows = _moe(item_e, item_start, item_tiles, x_sorted, w_up, b_up, w_down, b_down)
    y_p = _combine(pos, y_rows, h_buf, gate_all, mods_p[5], norm_final,
                   n_tok=n_p, tok_off=0, rows_per_group=seq)
    y_s = _combine(pos, y_rows, h_buf, gate_all, mods_s[5], norm_final,
                   n_tok=n_s, tok_off=n_p, rows_per_group=n_s)

    k_p = proj_p[:, SB_W:2 * SB_W].reshape(1, bsz, seq, H_SB, D_SB)
    v_p = proj_p[:, 2 * SB_W:3 * SB_W].reshape(1, bsz, seq, H_SB, D_SB)
    k_s = qkv_s[:, :, 1].reshape(1, dbs, dec_seq, H_SB, D_SB)
    v_s = qkv_s[:, :, 2].reshape(1, dbs, dec_seq, H_SB, D_SB)
    return (y_p.reshape(bsz, seq, d), y_s.reshape(dbs, dec_seq, d),
            k_p, v_p, k_s, v_s,
            c_p[None], nn_p[None], m_p[None], c_s[None], nn_s[None], m_s[None])
```

```python
import functools

import jax
import jax.numpy as jnp
from jax import lax
from jax.experimental import pallas as pl
from jax.experimental.pallas import tpu as pltpu

F32 = jnp.float32
BF16 = jnp.bfloat16
I32 = jnp.int32

H_SB = 8
D_SB = 128
SB_W = H_SB * D_SB
H_ML = 4
DQK_ML = 128
DV_ML = 256
ML_QK_W = H_ML * DQK_ML
ML_V_W = H_ML * DV_ML
PROJ_MAIN_W = 3 * SB_W + 2 * ML_QK_W + 2 * ML_V_W
N_GATES = 2 * H_ML
N_EXPERTS = 32
TOP_K = 4
N_MOD = 6
SWIGLU_LIMIT = 7.0
SWIGLU_ALPHA = 1.702
EPS = 1e-6

LANES = 128
SUBLANES = 8
BF16_SUBLANES = 16
MIB = 1024 * 1024

MOE_TILE = 128
MOE_ITEM_TILES = 12
MOE_FF_TILE = 256
ML_CHUNK = 128
SB_BLOCK = 512
SB_SUB = 256
PAGES_PER_STEP = 8
COMBINE_TOK = 64

NT_DIMS = (((1,), (1,)), ((), ()))
TN_DIMS = (((0,), (0,)), ((), ()))


def _cparams(sem, vmem_mib):
    return pltpu.CompilerParams(dimension_semantics=sem, vmem_limit_bytes=vmem_mib * MIB)


def _log_sigmoid_pair(z):
    t = jnp.log(1.0 + jnp.exp(-jnp.abs(z)))
    return jnp.minimum(z, 0.0) - t, -jnp.maximum(z, 0.0) - t


def _split_bf16(x):
    hi = x.astype(BF16)
    lo = (x - hi.astype(F32)).astype(BF16)
    return hi, lo


def _dot3(x, w):
    xh, xl = _split_bf16(x)
    wh, wl = _split_bf16(w)
    return (jnp.dot(xh, wh, preferred_element_type=F32)
            + (jnp.dot(xl, wh, preferred_element_type=F32)
               + jnp.dot(xh, wl, preferred_element_type=F32)))


def _mod_spec(arr, tile, rows_per_group, col_tile=None):
    per_group = arr.shape[1] == 1
    width = arr.shape[2] if col_tile is None else col_tile

    def index_map(i, *rest):
        col = 0 if col_tile is None else rest[0]
        if per_group:
            return ((i * tile) // rows_per_group, 0, col)
        return (0, i, col)

    return pl.BlockSpec((None, 1 if per_group else tile, width), index_map)


def _ada_kernel(c_ref, w_ref, b_ref, o_ref):
    c = c_ref[...]
    o_ref[...] = _dot3(c * jax.nn.sigmoid(c), w_ref[...]) + b_ref[...]


def _ada(c_all, w_ada, b_ada):
    n, d = c_all.shape
    w_out = w_ada.shape[1]
    tn = 1024
    return pl.pallas_call(
        _ada_kernel,
        out_shape=jax.ShapeDtypeStruct((n, w_out), F32),
        grid=(w_out // tn,),
        in_specs=[pl.BlockSpec((n, d), lambda j: (0, 0)),
                  pl.BlockSpec((d, tn), lambda j: (0, j)),
                  pl.BlockSpec((1, tn), lambda j: (0, j))],
        out_specs=pl.BlockSpec((n, tn), lambda j: (0, j)),
        compiler_params=_cparams(("arbitrary",), 40),
        name="ada",
    )(c_all, w_ada, b_ada.reshape(1, w_out))


def _inproj_kernel(x_ref, sh_ref, sc_ref, g_ref, w_ref, wg_ref, bg_ref,
                   proj_ref, gates_ref, xm_scr):
    @pl.when(pl.program_id(1) == 0)
    def _():
        x = x_ref[...]
        xn = x * lax.rsqrt(jnp.mean(x * x, axis=-1, keepdims=True) + EPS) * g_ref[...]
        xm = (xn * (1.0 + sc_ref[...]) + sh_ref[...]).astype(BF16)
        xm_scr[...] = xm
        gates_ref[...] = jnp.dot(xm, wg_ref[...].astype(BF16),
                                 preferred_element_type=F32) + bg_ref[...]

    proj_ref[...] = jnp.dot(xm_scr[...], w_ref[...].astype(BF16),
                            preferred_element_type=F32)


def _inproj(x2d, shift, scale, norm_w, w_in, wg_pad, bg_pad, *, tm, rows_per_group):
    t, d = x2d.shape
    tn = 512
    mod_spec = _mod_spec(shift, tm, rows_per_group)
    return pl.pallas_call(
        _inproj_kernel,
        out_shape=(jax.ShapeDtypeStruct((t, PROJ_MAIN_W), F32),
                   jax.ShapeDtypeStruct((t, LANES), F32)),
        grid=(t // tm, PROJ_MAIN_W // tn),
        in_specs=[pl.BlockSpec((tm, d), lambda i, j: (i, 0)),
                  mod_spec, mod_spec,
                  pl.BlockSpec((1, d), lambda i, j: (0, 0)),
                  pl.BlockSpec((d, tn), lambda i, j: (0, j)),
                  pl.BlockSpec((d, LANES), lambda i, j: (0, 0)),
                  pl.BlockSpec((1, LANES), lambda i, j: (0, 0))],
        out_specs=(pl.BlockSpec((tm, tn), lambda i, j: (i, j)),
                   pl.BlockSpec((tm, LANES), lambda i, j: (i, 0))),
        scratch_shapes=[pltpu.VMEM((tm, d), BF16)],
        compiler_params=_cparams(("arbitrary", "arbitrary"), 48),
        name="inproj",
    )(x2d, shift, scale, norm_w.reshape(1, d), w_in, wg_pad, bg_pad)


def _sb_terms(z):
    neg = jnp.minimum(z, 0.0)
    pos = jnp.maximum(z, 0.0)
    t = jnp.log(1.0 + jnp.exp(neg - pos))
    return neg - t, pos + t


def _sb_strip(z, mask, later_uu, sub, v_bf, spent):
    log_beta, cost = _sb_terms(z)
    if mask is not None:
        cost = jnp.where(mask, cost, 0.0)
    hi, lo = _split_bf16(cost)
    n_sub = z.shape[1] // sub
    parts = [None] * n_sub
    for j in reversed(range(n_sub)):
        cols = slice(j * sub, (j + 1) * sub)
        later = jnp.dot(jnp.concatenate([hi[:, cols], lo[:, cols]], axis=1), later_uu,
                        preferred_element_type=F32)
        parts[j] = jnp.exp(log_beta[:, cols] - (later + spent))
        spent = spent + (later[:, 0:1] + cost[:, j * sub:j * sub + 1])
    a = parts[0] if n_sub == 1 else jnp.concatenate(parts, axis=1)
    if mask is not None:
        a = jnp.where(mask, a, 0.0)
    return jnp.dot(a.astype(BF16), v_bf, preferred_element_type=F32), spent


def _later_matrix2(n):
    j = lax.broadcasted_iota(I32, (2 * n, n), 0) % n
    s = lax.broadcasted_iota(I32, (2 * n, n), 1)
    return jnp.where(j > s, 1.0, 0.0).astype(BF16)


def _sbp_kernel(bias_ref, q_ref, k_ref, v_ref, g_ref, o_ref, o_scr, carry_scr, *, blk, sub, scale):
    h = pl.program_id(1)
    qi = pl.program_id(2)
    bias = bias_ref[h]
    q = q_ref[...].astype(BF16)
    later_uu = _later_matrix2(sub)

    def strip(k_start, width, diagonal):
        k_bf = k_ref[pl.ds(k_start, width), :].astype(BF16)
        v_bf = v_ref[pl.ds(k_start, width), :].astype(BF16)
        z = lax.dot_general(q, k_bf, NT_DIMS, preferred_element_type=F32) * scale + bias
        mask = None
        if diagonal:
            t = lax.broadcasted_iota(I32, (blk, width), 0)
            s = lax.broadcasted_iota(I32, (blk, width), 1)
            mask = s < t
        o_blk, spent = _sb_strip(z, mask, later_uu, sub, v_bf, carry_scr[:, 0:1])
        o_scr[...] += o_blk
        carry_scr[...] = jnp.broadcast_to(spent, carry_scr.shape)

    o_scr[...] = jnp.zeros_like(o_scr)
    carry_scr[...] = jnp.zeros_like(carry_scr)
    strip(pl.multiple_of(qi * blk, blk), blk, True)

    @pl.when(qi % 2 == 1)
    def _():
        strip(pl.multiple_of((qi - 1) * blk, blk), blk, False)

    def older(n, carry):
        strip(pl.multiple_of((qi // 2 - 1 - n) * (2 * blk), 2 * blk), 2 * blk, False)
        return carry

    lax.fori_loop(0, qi // 2, older, 0)
    o = o_scr[...]
    o = o * lax.rsqrt(jnp.mean(o * o, axis=-1, keepdims=True) + EPS) * g_ref[...]
    o_ref[...] = o.astype(o_ref.dtype)


def _sb_prompt(proj, sb_bias, norm_sb_out, *, bsz, seq):
    blk = min(SB_BLOCK, seq)
    nq = seq // blk
    kern = functools.partial(_sbp_kernel, blk=blk, sub=min(SB_SUB, blk), scale=D_SB ** -0.5)
    return pl.pallas_call(
        kern,
        out_shape=jax.ShapeDtypeStruct((bsz * seq, SB_W), BF16),
        grid=(bsz, H_SB, nq),
        in_specs=[pl.BlockSpec(memory_space=pltpu.SMEM),
                  pl.BlockSpec((blk, D_SB), lambda b, h, i: (b * nq + i, h)),
                  pl.BlockSpec((seq, D_SB), lambda b, h, i: (b, H_SB + h)),
                  pl.BlockSpec((seq, D_SB), lambda b, h, i: (b, 2 * H_SB + h)),
                  pl.BlockSpec((None, 1, D_SB), lambda b, h, i: (h, 0, 0))],
        out_specs=pl.BlockSpec((blk, D_SB), lambda b, h, i: (b * nq + i, h)),
        scratch_shapes=[pltpu.VMEM((blk, D_SB), F32), pltpu.VMEM((blk, LANES), F32)],
        compiler_params=_cparams(("arbitrary", "arbitrary", "arbitrary"), 48),
        name="sb_prompt",
    )(sb_bias, proj, proj, proj, norm_sb_out.reshape(H_SB, 1, D_SB))


def _head_block_mask(rows, dec_seq):
    r = lax.broadcasted_iota(I32, (rows, SB_W), 0)
    c = lax.broadcasted_iota(I32, (rows, SB_W), 1)
    return (r // dec_seq) == (c // D_SB)


def _page_heads(page_ref):
    n_keys = page_ref.shape[0] // H_SB
    cols = [page_ref[pl.ds(h, n_keys, stride=H_SB), :] for h in range(H_SB)]
    return jnp.concatenate(cols, axis=1).astype(BF16)


def _sbd_kernel(pt_ref, qbd_ref, kn_ref, vn_ref, bias_ref, g_ref, *rest,
                dec_seq, page, n_steps, scale):
    del pt_ref
    pages = rest[:2 * PAGES_PER_STEP]
    o_ref = rest[2 * PAGES_PER_STEP]
    o_scr, carry_scr, new_scr = rest[2 * PAGES_PER_STEP + 1:]
    rows = H_SB * dec_seq
    j = pl.program_id(1)
    later_uu = _later_matrix2(page)

    def sweep(k_bf, v_bf, mask, spent):
        width = k_bf.shape[0]
        z = (lax.dot_general(qbd_ref[...], k_bf, NT_DIMS, preferred_element_type=F32) * scale
             + bias_ref[:, 0:width])
        return _sb_strip(z, mask, later_uu, page, v_bf, spent)

    @pl.when(j == 0)
    def _():
        new_scr[...] = jnp.zeros_like(new_scr)
        new_scr[0, 0:dec_seq, :] = kn_ref[...]
        new_scr[1, 0:dec_seq, :] = vn_ref[...]
        r = lax.broadcasted_iota(I32, (rows, page), 0)
        s = lax.broadcasted_iota(I32, (rows, page), 1)
        o_new, spent = sweep(new_scr[0].astype(BF16), new_scr[1].astype(BF16), s < (r % dec_seq),
                             jnp.zeros((rows, 1), F32))
        o_scr[...] = o_new
        carry_scr[...] = jnp.broadcast_to(spent, carry_scr.shape)

    order = range(PAGES_PER_STEP - 1, -1, -1)
    k_bf = jnp.concatenate([_page_heads(pages[i]) for i in order], axis=0)
    v_bf = jnp.concatenate([_page_heads(pages[PAGES_PER_STEP + i]) for i in order], axis=0)
    o_blk, spent = sweep(k_bf, v_bf, None, carry_scr[:, 0:1])
    o_scr[...] += o_blk
    carry_scr[...] = jnp.broadcast_to(spent, carry_scr.shape)

    @pl.when(j == n_steps - 1)
    def _():
        o_full = jnp.where(_head_block_mask(rows, dec_seq), o_scr[...], 0.0)
        o = o_full[:, 0:D_SB]
        for h in range(1, H_SB):
            o = o + o_full[:, h * D_SB:(h + 1) * D_SB]
        o = o * lax.rsqrt(jnp.mean(o * o, axis=-1, keepdims=True) + EPS) * g_ref[...]
        o_ref[...] = o


def _sb_paged(q_s, k_new, v_new, cache_k, cache_v, page_table, sb_bias, norm_sb_out):
    bsz, dec_seq, _ = q_s.shape
    n_pool, page = cache_k.shape[0], cache_k.shape[1]
    n_pages = page_table.shape[1]
    assert n_pages % PAGES_PER_STEP == 0
    n_steps = n_pages // PAGES_PER_STEP
    rows = H_SB * dec_seq
    ck = cache_k.reshape(n_pool, page * H_SB, D_SB)
    cv = cache_v.reshape(n_pool, page * H_SB, D_SB)
    strip_w = PAGES_PER_STEP * page
    bias_rows = jnp.broadcast_to(jnp.repeat(sb_bias, dec_seq)[:, None], (rows, strip_w)).astype(F32)
    g_rows = jnp.repeat(norm_sb_out.reshape(H_SB, D_SB), dec_seq, axis=0)
    q_heads = q_s.reshape(bsz, dec_seq, H_SB, D_SB).swapaxes(1, 2)
    eye = jnp.eye(H_SB, dtype=F32)
    qbd = (q_heads[:, :, :, None, :] * eye[None, :, None, :, None]).reshape(bsz, rows, SB_W).astype(BF16)

    def page_spec(i):
        return pl.BlockSpec(
            (None, page * H_SB, D_SB),
            lambda b, j, pt: (pt[b, n_pages - 1 - (j * PAGES_PER_STEP + i)], 0, 0))

    tok_spec = pl.BlockSpec((None, dec_seq, SB_W), lambda b, j, pt: (b, 0, 0))
    kern = functools.partial(_sbd_kernel, dec_seq=dec_seq, page=page, n_steps=n_steps,
                             scale=D_SB ** -0.5)
    grid_spec = pltpu.PrefetchScalarGridSpec(
        num_scalar_prefetch=1,
        grid=(bsz, n_steps),
        in_specs=[pl.BlockSpec((None, rows, SB_W), lambda b, j, pt: (b, 0, 0)),
                  tok_spec, tok_spec,
                  pl.BlockSpec((rows, strip_w), lambda b, j, pt: (0, 0)),
                  pl.BlockSpec((rows, D_SB), lambda b, j, pt: (0, 0))]
                 + [page_spec(i) for i in range(PAGES_PER_STEP)] * 2,
        out_specs=pl.BlockSpec((None, rows, D_SB), lambda b, j, pt: (b, 0, 0)),
        scratch_shapes=[pltpu.VMEM((rows, SB_W), F32),
                        pltpu.VMEM((rows, LANES), F32),
                        pltpu.VMEM((2, page, SB_W), F32)],
    )
    return pl.pallas_call(
        kern,
        out_shape=jax.ShapeDtypeStruct((bsz, rows, D_SB), F32),
        grid_spec=grid_spec,
        compiler_params=_cparams(("arbitrary", "arbitrary"), 48),
        name="sb_paged",
    )(page_table, qbd, k_new, v_new, bias_rows, g_rows,
      *([ck] * PAGES_PER_STEP), *([cv] * PAGES_PER_STEP))


def _mlstm_kernel(q_ref, k_ref, v_ref, og_ref, gcol_ref, grow_ref, c0_ref, n0_ref, m0_ref, gn_ref,
                  h_ref, c_out, n_out, m_out, c_scr, n_scr, m_scr, *, chunk, valid, n_chunks):
    ci = pl.program_id(1)

    @pl.when(ci == 0)
    def _():
        c_scr[...] = c0_ref[...]
        n_scr[...] = n0_ref[...]
        m_scr[...] = m0_ref[...]

    t_idx = lax.broadcasted_iota(I32, (chunk, chunk), 0)
    s_idx = lax.broadcasted_iota(I32, (chunk, chunk), 1)
    causal = s_idx <= t_idx
    col_valid = lax.broadcasted_iota(I32, (chunk, 1), 0) < valid
    row_valid = lax.broadcasted_iota(I32, (1, chunk), 1) < valid
    gcol = gcol_ref[...]
    grow = grow_ref[...]
    neg_inf = -jnp.inf

    for h in range(H_ML):
        q = q_ref[:, h * DQK_ML:(h + 1) * DQK_ML]
        ks = k_ref[:, h * DQK_ML:(h + 1) * DQK_ML] * (DQK_ML ** -0.5)
        v = v_ref[:, h * DV_ML:(h + 1) * DV_ML]
        q_bf, ks_bf, v_bf = q.astype(BF16), ks.astype(BF16), v.astype(BF16)

        i_col = jnp.where(col_valid, gcol[:, h:h + 1], neg_inf)
        i_row = jnp.where(row_valid, grow[h:h + 1, :], neg_inf)
        lf_col = jnp.where(col_valid, _log_sigmoid_pair(gcol[:, H_ML + h:H_ML + h + 1])[0], 0.0)
        lf_row = jnp.where(row_valid, _log_sigmoid_pair(grow[H_ML + h:H_ML + h + 1, :])[0], 0.0)
        bcum_col = jnp.sum(jnp.where(causal, lf_row, 0.0), axis=1, keepdims=True)
        bcum_row = jnp.sum(jnp.where(t_idx <= s_idx, lf_col, 0.0), axis=0, keepdims=True)

        log_w = jnp.where(causal, bcum_col - bcum_row + i_row, neg_inf)
        m0 = m_scr[h:h + 1, 0:1]
        m_state = bcum_col + m0
        m_new = jnp.maximum(m_state, jnp.max(log_w, axis=1, keepdims=True))
        w = jnp.exp(log_w - m_new)
        g = jnp.exp(m_state - m_new)

        s_mat = w * lax.dot_general(q_bf, ks_bf, NT_DIMS, preferred_element_type=F32)
        c0 = c_scr[h]
        n0 = n_scr[h:h + 1, :]
        num = (jnp.dot(s_mat.astype(BF16), v_bf, preferred_element_type=F32)
               + g * lax.dot_general(q_bf, c0.astype(BF16), NT_DIMS, preferred_element_type=F32))
        den = (jnp.sum(s_mat, axis=1, keepdims=True)
               + g * jnp.sum(q * n0, axis=1, keepdims=True))
        hh = num / jnp.maximum(jnp.abs(den), jnp.exp(-m_new))

        m_end = m_new[chunk - 1:chunk, :]
        b_last = bcum_col[chunk - 1:chunk, :]
        w_end = jnp.exp(b_last - bcum_col + i_col - m_end)
        g_end = jnp.exp(b_last + m0 - m_end)
        vw_bf = (v * w_end).astype(BF16)
        c_scr[h] = g_end * c0 + lax.dot_general(vw_bf, ks_bf, TN_DIMS, preferred_element_type=F32)
        n_scr[h:h + 1, :] = g_end * n0 + jnp.sum(ks * w_end, axis=0, keepdims=True)
        m_scr[h:h + 1, :] = jnp.broadcast_to(m_end, (1, LANES))

        hn = (hh * lax.rsqrt(jnp.mean(hh * hh, axis=-1, keepdims=True) + EPS)
              * gn_ref[:, h * DV_ML:(h + 1) * DV_ML])
        out = hn * jax.nn.sigmoid(og_ref[:, h * DV_ML:(h + 1) * DV_ML])
        h_ref[:, h * DV_ML:(h + 1) * DV_ML] = out.astype(h_ref.dtype)

    @pl.when(ci == n_chunks - 1)
    def _():
        c_out[...] = c_scr[...]
        n_out[...] = n_scr[...]
        m_out[...] = m_scr[...]


def _mlstm(proj, gates, c0, n0, m0, norm_ml_out, *, bsz, n_chunks, chunk, valid):
    rows = bsz * n_chunks * chunk
    grow = gates[:, :N_GATES].reshape(bsz, n_chunks, chunk, N_GATES).swapaxes(2, 3)
    m0_b = jnp.broadcast_to(m0[:, :, None], (bsz, H_ML, LANES)).astype(F32)
    qk_blk = ML_QK_W // LANES
    row_map = lambda b, c: (b * n_chunks + c)
    kern = functools.partial(_mlstm_kernel, chunk=chunk, valid=valid, n_chunks=n_chunks)
    state_specs = [pl.BlockSpec((None, H_ML, DV_ML, DQK_ML), lambda b, c: (b, 0, 0, 0)),
                   pl.BlockSpec((None, H_ML, DQK_ML), lambda b, c: (b, 0, 0)),
                   pl.BlockSpec((None, H_ML, LANES), lambda b, c: (b, 0, 0))]
    q_col = (3 * SB_W) // ML_QK_W
    v_col = (3 * SB_W + 2 * ML_QK_W) // ML_V_W
    h, c1, n1, m1 = pl.pallas_call(
        kern,
        out_shape=(jax.ShapeDtypeStruct((rows, ML_V_W), BF16),
                   jax.ShapeDtypeStruct((bsz, H_ML, DV_ML, DQK_ML), F32),
                   jax.ShapeDtypeStruct((bsz, H_ML, DQK_ML), F32),
                   jax.ShapeDtypeStruct((bsz, H_ML, LANES), F32)),
        grid=(bsz, n_chunks),
        in_specs=[pl.BlockSpec((chunk, ML_QK_W), lambda b, c: (row_map(b, c), q_col)),
                  pl.BlockSpec((chunk, ML_QK_W), lambda b, c: (row_map(b, c), q_col + 1)),
                  pl.BlockSpec((chunk, ML_V_W), lambda b, c: (row_map(b, c), v_col)),
                  pl.BlockSpec((chunk, ML_V_W), lambda b, c: (row_map(b, c), v_col + 1)),
                  pl.BlockSpec((chunk, LANES), lambda b, c: (row_map(b, c), 0)),
                  pl.BlockSpec((None, None, N_GATES, chunk), lambda b, c: (b, c, 0, 0))]
                 + state_specs
                 + [pl.BlockSpec((1, ML_V_W), lambda b, c: (0, 0))],
        out_specs=(pl.BlockSpec((chunk, ML_V_W), lambda b, c: (row_map(b, c), 0)),) + tuple(state_specs),
        scratch_shapes=[pltpu.VMEM((H_ML, DV_ML, DQK_ML), F32),
                        pltpu.VMEM((H_ML, DQK_ML), F32),
                        pltpu.VMEM((H_ML, LANES), F32)],
        compiler_params=_cparams(("arbitrary", "arbitrary"), 40),
        name="mlstm",
    )(proj, proj, proj, proj, gates, grow, c0, n0, m0_b, norm_ml_out.reshape(1, ML_V_W))
    del qk_blk
    return h, c1, n1, m1[:, :, 0]


def _outproj_kernel(osb_ref, hml_ref, w_ref, x_ref, gt_ref, sh_ref, sc_ref, g_ref, wr_ref, br_ref,
                    *rest, tn, n_col):
    h_ref, xf_ref, idx_ref, gate_ref, cat_scr, h_scr = rest[-6:]
    j = pl.program_id(1)

    @pl.when(j == 0)
    def _():
        cat_scr[:, 0:SB_W] = osb_ref[...]
        cat_scr[:, SB_W:SB_W + ML_V_W] = hml_ref[...]

    mix = jnp.dot(cat_scr[...], w_ref[...].astype(BF16), preferred_element_type=F32)
    h_blk = x_ref[...] + gt_ref[...] * mix
    h_ref[...] = h_blk
    for jj in range(n_col):
        @pl.when(j == jj)
        def _(jj=jj):
            h_scr[:, jj * tn:(jj + 1) * tn] = h_blk

    @pl.when(j == n_col - 1)
    def _():
        hf = h_scr[...]
        xn = hf * lax.rsqrt(jnp.mean(hf * hf, axis=-1, keepdims=True) + EPS) * g_ref[...]
        xf = xn * (1.0 + sc_ref[...]) + sh_ref[...]
        xf_ref[...] = xf
        logits = _dot3(xf, wr_ref[...]) + br_ref[...]
        lane = lax.broadcasted_iota(I32, logits.shape, 1)
        lane_f = lane.astype(F32)
        neg_inf = -jnp.inf
        lg = jnp.where(lane < N_EXPERTS, logits, neg_inf)
        vals, idxs = [], []
        for _k in range(TOP_K):
            mx = jnp.max(lg, axis=1, keepdims=True)
            ix = jnp.min(jnp.where(lg == mx, lane_f, float(LANES)), axis=1, keepdims=True)
            vals.append(mx)
            idxs.append(ix)
            lg = jnp.where(lane_f == ix, neg_inf, lg)
        es = [jnp.exp(vk - vals[0]) for vk in vals]
        tot = es[0] + es[1] + es[2] + es[3]
        idx_out = jnp.zeros(logits.shape, F32)
        gate_out = jnp.zeros(logits.shape, F32)
        for k in range(TOP_K):
            idx_out = jnp.where(lane == k, idxs[k], idx_out)
            gate_out = jnp.where(lane == k, es[k] / tot, gate_out)
        idx_ref[...] = idx_out.astype(I32)
        gate_ref[...] = gate_out


def _outproj(o_sb, h_ml, w_out, x2d, gate_a, shift_f, scale_f, norm_ffn, wr_pad, br_pad,
             bufs, *, n_tok, tm, rows_per_group, row_off):
    t, d = x2d.shape
    tn = 512
    n_col = d // tn
    off = row_off // tm
    kern = functools.partial(_outproj_kernel, tn=tn, n_col=n_col)
    n_fixed = 10
    alias_specs, alias_args, aliases = [], (), {}
    if bufs is not None:
        alias_specs = [pl.BlockSpec(memory_space=pl.ANY)] * 2
        alias_args = tuple(bufs)
        aliases = {n_fixed: 0, n_fixed + 1: 1}
    return pl.pallas_call(
        kern,
        out_shape=(jax.ShapeDtypeStruct((n_tok, d), F32),
                   jax.ShapeDtypeStruct((n_tok, d), F32),
                   jax.ShapeDtypeStruct((t, LANES), I32),
                   jax.ShapeDtypeStruct((t, LANES), F32)),
        grid=(t // tm, n_col),
        in_specs=[pl.BlockSpec((tm, SB_W), lambda i, j: (i, 0)),
                  pl.BlockSpec((tm, ML_V_W), lambda i, j: (i, 0)),
                  pl.BlockSpec((SB_W + ML_V_W, tn), lambda i, j: (0, j)),
                  pl.BlockSpec((tm, tn), lambda i, j: (i, j)),
                  _mod_spec(gate_a, tm, rows_per_group, col_tile=tn),
                  _mod_spec(shift_f, tm, rows_per_group),
                  _mod_spec(scale_f, tm, rows_per_group),
                  pl.BlockSpec((1, d), lambda i, j: (0, 0)),
                  pl.BlockSpec((d, LANES), lambda i, j: (0, 0)),
                  pl.BlockSpec((1, LANES), lambda i, j: (0, 0))] + alias_specs,
        out_specs=(pl.BlockSpec((tm, tn), lambda i, j: (off + i, j)),
                   pl.BlockSpec((tm, d), lambda i, j: (off + i, 0)),
                   pl.BlockSpec((tm, LANES), lambda i, j: (i, 0)),
                   pl.BlockSpec((tm, LANES), lambda i, j: (i, 0))),
        scratch_shapes=[pltpu.VMEM((tm, SB_W + ML_V_W), BF16),
                        pltpu.VMEM((tm, d), F32)],
        input_output_aliases=aliases,
        compiler_params=_cparams(("arbitrary", "arbitrary"), 48),
        name="outproj",
    )(o_sb, h_ml, w_out, x2d, gate_a, shift_f, scale_f, norm_ffn.reshape(1, d), wr_pad, br_pad,
      *alias_args)


def _scatter_kernel(pos_ref, x_ref, init_hbm, o_hbm, sem, *, tok_tile):
    del init_hbm
    base = pl.program_id(0) * tok_tile * TOP_K

    def token(tok, carry):
        for k in range(TOP_K):
            p = pos_ref[base + tok * TOP_K + k]
            pltpu.make_async_copy(x_ref.at[pl.ds(tok, 1), :], o_hbm.at[pl.ds(p, 1), :], sem).start()
        return carry

    lax.fori_loop(0, tok_tile, token, 0, unroll=2)
    done = o_hbm.at[pl.ds(0, tok_tile * TOP_K), :]
    pltpu.make_async_copy(done, done, sem).wait()


def _scatter_rows(pos, xf, n_rows):
    n_tok, w = xf.shape
    tok_tile = next(t for t in range(512, 0, -SUBLANES) if n_tok % t == 0)
    kern = functools.partial(_scatter_kernel, tok_tile=tok_tile)
    grid_spec = pltpu.PrefetchScalarGridSpec(
        num_scalar_prefetch=1,
        grid=(n_tok // tok_tile,),
        in_specs=[pl.BlockSpec((tok_tile, w), lambda i, p: (i, 0)),
                  pl.BlockSpec(memory_space=pl.ANY)],
        out_specs=pl.BlockSpec(memory_space=pl.ANY),
        scratch_shapes=[pltpu.SemaphoreType.DMA],
    )
    return pl.pallas_call(
        kern,
        out_shape=jax.ShapeDtypeStruct((n_rows, w), xf.dtype),
        grid_spec=grid_spec,
        input_output_aliases={2: 0},
        compiler_params=_cparams(("arbitrary",), 32),
        name="moe_scatter",
    )(pos, xf, jnp.zeros((n_rows, w), xf.dtype))


def _moe_kernel(e_ref, start_ref, tiles_ref, x_hbm, wu_ref, bu_ref, wd_ref, bd_ref, y_hbm,
                x_stage, x_bf, acc, wu_bf, wd_perm, wd_bf, sem_in, sem_out, *, n_ff, d_model, n_items):
    del e_ref
    it = pl.program_id(0)
    f = pl.program_id(1)
    n_tiles = tiles_ref[it]
    row0 = pl.multiple_of(start_ref[it] * MOE_TILE, MOE_TILE)
    half = LANES // 2

    def tile_rows(t):
        return pl.ds(pl.multiple_of(t * MOE_TILE, MOE_TILE), MOE_TILE)

    def x_copy(item, t):
        src0 = pl.multiple_of(start_ref[item] * MOE_TILE, MOE_TILE)
        return pltpu.make_async_copy(x_hbm.at[pl.ds(src0 + t * MOE_TILE, MOE_TILE), :],
                                     x_stage.at[tile_rows(t), :], sem_in)

    def y_copy(t):
        return pltpu.make_async_copy(acc.at[tile_rows(t), :],
                                     y_hbm.at[pl.ds(row0 + t * MOE_TILE, MOE_TILE), :], sem_out)

    def for_tiles(count, fn):
        def body(t, carry):
            fn(t)
            return carry
        lax.fori_loop(0, count, body, 0)

    @pl.when(n_tiles > 0)
    def _():
        @pl.when(f == 0)
        def _():
            @pl.when(it == 0)
            def _():
                for_tiles(n_tiles, lambda t: x_copy(it, t).start())

            for_tiles(n_tiles, lambda t: x_copy(it, t).wait())
            b_dn = jnp.broadcast_to(bd_ref[...], (MOE_TILE, d_model))

            def init(t):
                x_bf[tile_rows(t), :] = x_stage[tile_rows(t), :].astype(BF16)
                acc[tile_rows(t), :] = b_dn

            for_tiles(n_tiles, init)
            nxt = jnp.minimum(it + 1, n_items - 1)
            n_next = jnp.where(it + 1 < n_items, tiles_ref[nxt], 0)
            for_tiles(n_next, lambda t: x_copy(nxt, t).start())

        wu_bf[...] = wu_ref[...].astype(BF16)
        for c in range(MOE_FF_TILE // LANES):
            for s in range(d_model // LANES):
                for par in range(2):
                    src = wd_ref[c * LANES + par * half:c * LANES + (par + 1) * half,
                                 s * LANES:(s + 1) * LANES]
                    wd_perm[s, pl.ds(c * LANES + par, half, stride=2), :] = src
        for s in range(d_model // LANES):
            wd_bf[:, s * LANES:(s + 1) * LANES] = wd_perm[s].astype(BF16)

        b_up = bu_ref[...]

        def span_rows(t, n):
            return pl.ds(pl.multiple_of(t * MOE_TILE, MOE_TILE), n * MOE_TILE)

        def up_proj(t, n):
            return jnp.dot(x_bf[span_rows(t, n), :], wu_bf[...],
                           preferred_element_type=F32) + b_up

        def activation(hu):
            even = (lax.broadcasted_iota(I32, (hu.shape[0], LANES), 1) % 2) == 0
            acts = []
            for c in range(MOE_FF_TILE // LANES):
                a_blk = hu[:, (2 * c) * LANES:(2 * c + 1) * LANES]
                b_blk = hu[:, (2 * c + 1) * LANES:(2 * c + 2) * LANES]
                gate = jnp.where(even, a_blk, pltpu.roll(b_blk, 1, 1))
                up = jnp.where(even, pltpu.roll(a_blk, LANES - 1, 1), b_blk)
                gate = jnp.minimum(gate, SWIGLU_LIMIT)
                up = jnp.clip(up, -SWIGLU_LIMIT, SWIGLU_LIMIT)
                acts.append((up + 1.0) * gate * jax.nn.sigmoid(SWIGLU_ALPHA * gate))
            return jnp.concatenate(acts, axis=1).astype(BF16)

        def down_proj(t, n, act):
            acc[span_rows(t, n), :] += jnp.dot(act, wd_bf[...], preferred_element_type=F32)

        def chains(spans):
            hus = [up_proj(t, n) for t, n in spans]
            acts = [activation(hu) for hu in hus]
            for (t, n), act in zip(spans, acts):
                down_proj(t, n, act)

        def run_tiles(after_tile):
            def quad(p, carry):
                chains([(4 * p, 2), (4 * p + 2, 2)])
                for u in range(4):
                    after_tile(4 * p + u)
                return carry

            n_quads = n_tiles // 4
            lax.fori_loop(0, n_quads, quad, 0)
            rest = n_tiles - 4 * n_quads

            @pl.when(rest >= 2)
            def _():
                chains([(4 * n_quads, 1), (4 * n_quads + 1, 1)])
                after_tile(4 * n_quads)
                after_tile(4 * n_quads + 1)

            @pl.when(rest % 2 == 1)
            def _():
                chains([(n_tiles - 1, 1)])
                after_tile(n_tiles - 1)

        @pl.when(f < n_ff - 1)
        def _():
            run_tiles(lambda t: None)

        @pl.when(f == n_ff - 1)
        def _():
            run_tiles(lambda t: y_copy(t).start())
            for_tiles(n_tiles, lambda t: y_copy(t).wait())


def _moe(item_e, item_start, item_tiles, x_sorted, w_up, b_up, w_down, b_down):
    n_rows = x_sorted.shape[0]
    n_exp, d, ff2_total = w_up.shape
    d_ff = ff2_total // 2
    n_ff = d_ff // MOE_FF_TILE
    n_items = item_e.shape[0]
    item_rows = MOE_ITEM_TILES * MOE_TILE
    kern = functools.partial(_moe_kernel, n_ff=n_ff, d_model=d, n_items=n_items)

    def ff_tile(i, f, n):
        return jnp.where(n[i] > 0, f, n_ff - 1)

    grid_spec = pltpu.PrefetchScalarGridSpec(
        num_scalar_prefetch=3,
        grid=(n_items, n_ff),
        in_specs=[pl.BlockSpec(memory_space=pl.ANY),
                  pl.BlockSpec((None, d, 2 * MOE_FF_TILE),
                               lambda i, f, e, s, n: (e[i], 0, ff_tile(i, f, n))),
                  pl.BlockSpec((None, 1, 2 * MOE_FF_TILE),
                               lambda i, f, e, s, n: (e[i], 0, ff_tile(i, f, n))),
                  pl.BlockSpec((None, MOE_FF_TILE, d),
                               lambda i, f, e, s, n: (e[i], ff_tile(i, f, n), 0)),
                  pl.BlockSpec((None, 1, d), lambda i, f, e, s, n: (e[i], 0, 0))],
        out_specs=pl.BlockSpec(memory_space=pl.ANY),
        scratch_shapes=[pltpu.VMEM((item_rows, d), F32),
                        pltpu.VMEM((item_rows, d), BF16),
                        pltpu.VMEM((item_rows, d), F32),
                        pltpu.VMEM((d, 2 * MOE_FF_TILE), BF16),
                        pltpu.VMEM((d // LANES, MOE_FF_TILE, LANES), F32),
                        pltpu.VMEM((MOE_FF_TILE, d), BF16),
                        pltpu.SemaphoreType.DMA,
                        pltpu.SemaphoreType.DMA],
    )
    return pl.pallas_call(
        kern,
        out_shape=jax.ShapeDtypeStruct((n_rows, d), F32),
        grid_spec=grid_spec,
        compiler_params=_cparams(("arbitrary", "arbitrary"), 56),
        name="moe_experts",
    )(item_e, item_start, item_tiles, x_sorted, w_up,
      b_up.reshape(n_exp, 1, ff2_total), w_down, b_down.reshape(n_exp, 1, d))


def _combine_kernel(pos_ref, y_hbm, h_ref, gate_ref, gt_ref, g_ref, o_ref, buf, sem, *, tok_off):
    i = pl.program_id(0)

    def issue(step, slot):
        base = (step * COMBINE_TOK + tok_off) * TOP_K

        def token(tok, carry):
            for k in range(TOP_K):
                p = pos_ref[base + tok * TOP_K + k]
                pltpu.make_async_copy(y_hbm.at[pl.ds(p, 1), :],
                                      buf.at[slot, pl.ds(k * COMBINE_TOK + tok, 1), :],
                                      sem.at[slot]).start()
            return carry

        lax.fori_loop(0, COMBINE_TOK, token, 0, unroll=2)

    @pl.when(i == 0)
    def _():
        issue(0, 0)

    @pl.when(i + 1 < pl.num_programs(0))
    def _():
        issue(i + 1, (i + 1) % 2)

    slot = i % 2
    pltpu.make_async_copy(y_hbm.at[pl.ds(0, COMBINE_TOK * TOP_K), :], buf.at[slot],
                          sem.at[slot]).wait()
    gates = gate_ref[...]
    ffn = jnp.zeros(h_ref.shape, F32)
    for k in range(TOP_K):
        ffn = ffn + gates[:, k:k + 1] * buf[slot, k * COMBINE_TOK:(k + 1) * COMBINE_TOK, :]
    y = h_ref[...] + gt_ref[...] * ffn
    o_ref[...] = y * lax.rsqrt(jnp.mean(y * y, axis=-1, keepdims=True) + EPS) * g_ref[...]


def _combine(pos, y_rows, h_buf, gate, gate_f, norm_final, *, n_tok, tok_off, rows_per_group):
    d = h_buf.shape[1]
    off = tok_off // COMBINE_TOK
    kern = functools.partial(_combine_kernel, tok_off=tok_off)
    grid_spec = pltpu.PrefetchScalarGridSpec(
        num_scalar_prefetch=1,
        grid=(n_tok // COMBINE_TOK,),
        in_specs=[pl.BlockSpec(memory_space=pl.ANY),
                  pl.BlockSpec((COMBINE_TOK, d), lambda i, p: (off + i, 0)),
                  pl.BlockSpec((COMBINE_TOK, LANES), lambda i, p: (off + i, 0)),
                  _mod_spec(gate_f, COMBINE_TOK, rows_per_group),
                  pl.BlockSpec((1, d), lambda i, p: (0, 0))],
        out_specs=pl.BlockSpec((COMBINE_TOK, d), lambda i, p: (i, 0)),
        scratch_shapes=[pltpu.VMEM((2, COMBINE_TOK * TOP_K, d), F32),
                        pltpu.SemaphoreType.DMA((2,))],
    )
    return pl.pallas_call(
        kern,
        out_shape=jax.ShapeDtypeStruct((n_tok, d), F32),
        grid_spec=grid_spec,
        compiler_params=_cparams(("arbitrary",), 32),
        name="moe_combine",
    )(pos, y_rows, h_buf, gate, gate_f, norm_final.reshape(1, d))


def _routing_tables(top_idx):
    n_tok = top_idx.shape[0]
    n_assign = n_tok * TOP_K
    flat_e = top_idx.reshape(-1)
    onehot = (flat_e[:, None] == jnp.arange(N_EXPERTS, dtype=I32)[None, :]).astype(I32)
    running = jnp.cumsum(onehot, axis=0)
    rank = jnp.take_along_axis(running, flat_e[:, None], axis=1)[:, 0] - 1
    counts = running[-1]
    tiles_e = (counts + MOE_TILE - 1) // MOE_TILE
    tile_start_e = jnp.cumsum(tiles_e) - tiles_e
    pos = tile_start_e[flat_e] * MOE_TILE + rank

    n_tiles_max = -(-n_assign // MOE_TILE) + N_EXPERTS
    n_rows = n_tiles_max * MOE_TILE

    n_items_max = N_EXPERTS + n_tiles_max // MOE_ITEM_TILES
    items_e = (tiles_e + MOE_ITEM_TILES - 1) // MOE_ITEM_TILES
    items_end = jnp.cumsum(items_e)
    it = jnp.arange(n_items_max, dtype=I32)
    e_of = jnp.minimum(jnp.sum((it[:, None] >= items_end[None, :]).astype(I32), axis=1), N_EXPERTS - 1)
    local = it - (items_end[e_of] - items_e[e_of])
    valid = it < items_end[-1]
    tiles_left = tiles_e[e_of] - local * MOE_ITEM_TILES
    item_tiles = jnp.where(valid, jnp.clip(tiles_left, 0, MOE_ITEM_TILES), 0).astype(I32)
    item_start = jnp.where(valid, tile_start_e[e_of] + local * MOE_ITEM_TILES, 0).astype(I32)
    last_e = e_of[jnp.maximum(items_end[-1] - 1, 0)]
    item_e = jnp.where(valid, e_of, last_e).astype(I32)
    return pos.astype(I32), n_rows, item_e, item_start, item_tiles


def _pick_tile(n, pref):
    t = min(n, pref)
    while n % t:
        t //= 2
    return t


def kernel(x_prompt, x_sample, c_prompt, c_sample, cache_k, cache_v, page_table, state_C, state_n, state_m, w_ada, b_ada, norm_mix, norm_ffn, w_in, b_gates, sb_bias, norm_sb_out, norm_ml_out, w_out, w_router, b_router, w_up, b_up, w_down, b_down, norm_final):
    bsz, seq, d = x_prompt.shape
    dbs, dec_seq, _ = x_sample.shape
    depth = w_ada.shape[0]
    assert depth == 1
    n_p, n_s = bsz * seq, dbs * dec_seq
    n_tok = n_p + n_s

    (w_ada, b_ada, norm_mix, norm_ffn, w_in, b_gates, sb_bias, norm_sb_out, norm_ml_out, w_out,
     w_router, b_router, w_up, b_up, w_down, b_down) = [
        a[0] for a in (w_ada, b_ada, norm_mix, norm_ffn, w_in, b_gates, sb_bias, norm_sb_out,
                       norm_ml_out, w_out, w_router, b_router, w_up, b_up, w_down, b_down)]

    n_c = bsz + dbs
    n_c_pad = -(-n_c // SUBLANES) * SUBLANES
    c_all = jnp.concatenate([c_prompt, c_sample, jnp.zeros((n_c_pad - n_c, d), F32)], axis=0)
    mod = _ada(c_all, w_ada, b_ada)
    mods_p = [mod[:bsz, i * d:(i + 1) * d].reshape(bsz, 1, d) for i in range(N_MOD)]
    mods_s = [jnp.repeat(mod[bsz:n_c, i * d:(i + 1) * d], dec_seq, axis=0).reshape(1, n_s, d)
              for i in range(N_MOD)]

    wg_pad = jnp.pad(w_in[:, PROJ_MAIN_W:], ((0, 0), (0, LANES - N_GATES)))
    bg_pad = jnp.pad(b_gates, (0, LANES - N_GATES)).reshape(1, LANES)
    wr_pad = jnp.pad(w_router, ((0, 0), (0, LANES - N_EXPERTS)))
    br_pad = jnp.pad(b_router, (0, LANES - N_EXPERTS)).reshape(1, LANES)

    xp2 = x_prompt.reshape(n_p, d)
    xs2 = x_sample.reshape(n_s, d)
    tm_p = _pick_tile(seq, 1024)

    proj_p, gates_p = _inproj(xp2, mods_p[0], mods_p[1], norm_mix, w_in, wg_pad, bg_pad,
                              tm=tm_p, rows_per_group=seq)
    osb_p = _sb_prompt(proj_p, sb_bias, norm_sb_out, bsz=bsz, seq=seq)
    chunk_p = _pick_tile(seq, ML_CHUNK)
    hml_p, c_p, nn_p, m_p = _mlstm(
        proj_p, gates_p,
        jnp.zeros((bsz, H_ML, DV_ML, DQK_ML), F32), jnp.zeros((bsz, H_ML, DQK_ML), F32),
        jnp.zeros((bsz, H_ML), F32), norm_ml_out,
        bsz=bsz, n_chunks=seq // chunk_p, chunk=chunk_p, valid=chunk_p)

    proj_s, gates_s = _inproj(xs2, mods_s[0], mods_s[1], norm_mix, w_in, wg_pad, bg_pad,
                              tm=n_s, rows_per_group=n_s)
    qkv_s = proj_s[:, :3 * SB_W].reshape(dbs, dec_seq, 3, SB_W)
    osb_s = _sb_paged(qkv_s[:, :, 0], qkv_s[:, :, 1], qkv_s[:, :, 2], cache_k[0], cache_v[0],
                      page_table, sb_bias, norm_sb_out)
    osb_s = (osb_s.reshape(dbs, H_SB, dec_seq, D_SB).swapaxes(1, 2)
             .reshape(n_s, SB_W).astype(BF16))
    chunk_s = -(-dec_seq // BF16_SUBLANES) * BF16_SUBLANES
    pad_rows = lambda a: jnp.pad(a.reshape(dbs, dec_seq, a.shape[-1]),
                                 ((0, 0), (0, chunk_s - dec_seq), (0, 0))).reshape(dbs * chunk_s, a.shape[-1])
    hml_s, c_s, nn_s, m_s = _mlstm(
        pad_rows(proj_s), pad_rows(gates_s), state_C[0], state_n[0], state_m[0], norm_ml_out,
        bsz=dbs, n_chunks=1, chunk=chunk_s, valid=dec_seq)
    hml_s = hml_s.reshape(dbs, chunk_s, ML_V_W)[:, :dec_seq].reshape(n_s, ML_V_W)

    tm_o = _pick_tile(seq, 512)
    h_buf, xf_buf, idx_p, gate_p = _outproj(
        osb_p, hml_p, w_out, xp2, mods_p[2], mods_p[3], mods_p[4], norm_ffn, wr_pad, br_pad,
        None, n_tok=n_tok, tm=tm_o, rows_per_group=seq, row_off=0)
    h_buf, xf_buf, idx_s, gate_s = _outproj(
        osb_s, hml_s, w_out, xs2, mods_s[2], mods_s[3], mods_s[4], norm_ffn, wr_pad, br_pad,
        (h_buf, xf_buf), n_tok=n_tok, tm=n_s, rows_per_group=n_s, row_off=n_p)
    top_idx = jnp.concatenate([idx_p[:, :TOP_K], idx_s[:, :TOP_K]], axis=0)
    gate_all = jnp.concatenate([gate_p, gate_s], axis=0)

    pos, n_rows, item_e, item_start, item_tiles = _routing_tables(top_idx)
    x_sorted = _scatter_rows(pos, xf_buf, n_rows)
    y_rows = _moe(item_e, item_start, item_tiles, x_sorted, w_up, b_up, w_down, b_down)
    y_p = _combine(pos, y_rows, h_buf, gate_all, mods_p[5], norm_final,
                   n_tok=n_p, tok_off=0, rows_per_group=seq)
    y_s = _combine(pos, y_rows, h_buf, gate_all, mods_s[5], norm_final,
                   n_tok=n_s, tok_off=n_p, rows_per_group=n_s)

    k_p = proj_p[:, SB_W:2 * SB_W].reshape(1, bsz, seq, H_SB, D_SB)
    v_p = proj_p[:, 2 * SB_W:3 * SB_W].reshape(1, bsz, seq, H_SB, D_SB)
    k_s = qkv_s[:, :, 1].reshape(1, dbs, dec_seq, H_SB, D_SB)
    v_s = qkv_s[:, :, 2].reshape(1, dbs, dec_seq, H_SB, D_SB)
    return (y_p.reshape(bsz, seq, d), y_s.reshape(dbs, dec_seq, d),
            k_p, v_p, k_s, v_s,
            c_p[None], nn_p[None], m_p[None], c_s[None], nn_s[None], m_s[None])
```

```python
import functools

import jax
import jax.numpy as jnp
from jax import lax
from jax.experimental import pallas as pl
from jax.experimental.pallas import tpu as pltpu

F32 = jnp.float32
BF16 = jnp.bfloat16
I32 = jnp.int32

H_SB = 8
D_SB = 128
SB_W = H_SB * D_SB
H_ML = 4
DQK_ML = 128
DV_ML = 256
ML_QK_W = H_ML * DQK_ML
ML_V_W = H_ML * DV_ML
PROJ_MAIN_W = 3 * SB_W + 2 * ML_QK_W + 2 * ML_V_W
N_GATES = 2 * H_ML
N_EXPERTS = 32
TOP_K = 4
N_MOD = 6
SWIGLU_LIMIT = 7.0
SWIGLU_ALPHA = 1.702
EPS = 1e-6

LANES = 128
SUBLANES = 8
BF16_SUBLANES = 16
MIB = 1024 * 1024

MOE_TILE = 128
MOE_ITEM_TILES = 12
MOE_FF_TILE = 256
ML_CHUNK = 128
SB_BLOCK = 512
SB_SUB = 256
PAGES_PER_STEP = 8
COMBINE_TOK = 64

NT_DIMS = (((1,), (1,)), ((), ()))
TN_DIMS = (((0,), (0,)), ((), ()))


def _cparams(sem, vmem_mib):
    return pltpu.CompilerParams(dimension_semantics=sem, vmem_limit_bytes=vmem_mib * MIB)


def _log_sigmoid_pair(z):
    t = jnp.log(1.0 + jnp.exp(-jnp.abs(z)))
    return jnp.minimum(z, 0.0) - t, -jnp.maximum(z, 0.0) - t


def _split_bf16(x):
    hi = x.astype(BF16)
    lo = (x - hi.astype(F32)).astype(BF16)
    return hi, lo


def _dot3(x, w):
    xh, xl = _split_bf16(x)
    wh, wl = _split_bf16(w)
    return (jnp.dot(xh, wh, preferred_element_type=F32)
            + (jnp.dot(xl, wh, preferred_element_type=F32)
               + jnp.dot(xh, wl, preferred_element_type=F32)))


def _mod_spec(arr, tile, rows_per_group, col_tile=None):
    per_group = arr.shape[1] == 1
    width = arr.shape[2] if col_tile is None else col_tile

    def index_map(i, *rest):
        col = 0 if col_tile is None else rest[0]
        if per_group:
            return ((i * tile) // rows_per_group, 0, col)
        return (0, i, col)

    return pl.BlockSpec((None, 1 if per_group else tile, width), index_map)


def _ada_kernel(c_ref, w_ref, b_ref, o_ref):
    c = c_ref[...]
    o_ref[...] = _dot3(c * jax.nn.sigmoid(c), w_ref[...]) + b_ref[...]


def _ada(c_all, w_ada, b_ada):
    n, d = c_all.shape
    w_out = w_ada.shape[1]
    tn = 1024
    return pl.pallas_call(
        _ada_kernel,
        out_shape=jax.ShapeDtypeStruct((n, w_out), F32),
        grid=(w_out // tn,),
        in_specs=[pl.BlockSpec((n, d), lambda j: (0, 0)),
                  pl.BlockSpec((d, tn), lambda j: (0, j)),
                  pl.BlockSpec((1, tn), lambda j: (0, j))],
        out_specs=pl.BlockSpec((n, tn), lambda j: (0, j)),
        compiler_params=_cparams(("arbitrary",), 40),
        name="ada",
    )(c_all, w_ada, b_ada.reshape(1, w_out))


def _inproj_kernel(x_ref, sh_ref, sc_ref, g_ref, w_ref, wg_ref, bg_ref,
                   proj_ref, gates_ref, k_ref, v_ref, xm_scr, *, tn):
    j = pl.program_id(1)

    @pl.when(j == 0)
    def _():
        x = x_ref[...]
        xn = x * lax.rsqrt(jnp.mean(x * x, axis=-1, keepdims=True) + EPS) * g_ref[...]
        xm = (xn * (1.0 + sc_ref[...]) + sh_ref[...]).astype(BF16)
        xm_scr[...] = xm
        gates_ref[...] = jnp.dot(xm, wg_ref[...].astype(BF16),
                                 preferred_element_type=F32) + bg_ref[...]

    res = jnp.dot(xm_scr[...], w_ref[...].astype(BF16), preferred_element_type=F32)
    proj_ref[...] = res
    k_tiles = SB_W // tn

    @pl.when((j >= k_tiles) & (j < 2 * k_tiles))
    def _():
        k_ref[...] = res

    @pl.when((j >= 2 * k_tiles) & (j < 3 * k_tiles))
    def _():
        v_ref[...] = res


def _inproj(x2d, shift, scale, norm_w, w_in, wg_pad, bg_pad, *, tm, rows_per_group):
    t, d = x2d.shape
    tn = 512
    k_tiles = SB_W // tn
    mod_spec = _mod_spec(shift, tm, rows_per_group)

    def kv_spec(first):
        return pl.BlockSpec((tm, tn), lambda i, j: (i, jnp.clip(j - first, 0, k_tiles - 1)))

    return pl.pallas_call(
        functools.partial(_inproj_kernel, tn=tn),
        out_shape=(jax.ShapeDtypeStruct((t, PROJ_MAIN_W), F32),
                   jax.ShapeDtypeStruct((t, LANES), F32),
                   jax.ShapeDtypeStruct((t, SB_W), F32),
                   jax.ShapeDtypeStruct((t, SB_W), F32)),
        grid=(t // tm, PROJ_MAIN_W // tn),
        in_specs=[pl.BlockSpec((tm, d), lambda i, j: (i, 0)),
                  mod_spec, mod_spec,
                  pl.BlockSpec((1, d), lambda i, j: (0, 0)),
                  pl.BlockSpec((d, tn), lambda i, j: (0, j)),
                  pl.BlockSpec((d, LANES), lambda i, j: (0, 0)),
                  pl.BlockSpec((1, LANES), lambda i, j: (0, 0))],
        out_specs=(pl.BlockSpec((tm, tn), lambda i, j: (i, j)),
                   pl.BlockSpec((tm, LANES), lambda i, j: (i, 0)),
                   kv_spec(k_tiles), kv_spec(2 * k_tiles)),
        scratch_shapes=[pltpu.VMEM((tm, d), BF16)],
        compiler_params=_cparams(("arbitrary", "arbitrary"), 48),
        name="inproj",
    )(x2d, shift, scale, norm_w.reshape(1, d), w_in, wg_pad, bg_pad)


def _sb_strip(z, mask, from_uu, sub, v_bf, spent):
    pos = jnp.maximum(z, 0.0)
    cost = pos + jnp.log(1.0 + jnp.exp(jnp.minimum(z, 0.0) - pos))
    if mask is not None:
        cost = jnp.where(mask, cost, 0.0)
    hi, lo = _split_bf16(cost)
    n_sub = z.shape[1] // sub
    parts = [None] * n_sub
    for j in reversed(range(n_sub)):
        cols = slice(j * sub, (j + 1) * sub)
        from_s = jnp.dot(jnp.concatenate([hi[:, cols], lo[:, cols]], axis=1), from_uu,
                         preferred_element_type=F32)
        parts[j] = jnp.exp(z[:, cols] - (from_s + spent))
        spent = spent + from_s[:, 0:1]
    a = parts[0] if n_sub == 1 else jnp.concatenate(parts, axis=1)
    if mask is not None:
        a = jnp.where(mask, a, 0.0)
    return jnp.dot(a.astype(BF16), v_bf, preferred_element_type=F32), spent


def _from_matrix2(n):
    j = lax.broadcasted_iota(I32, (2 * n, n), 0) % n
    s = lax.broadcasted_iota(I32, (2 * n, n), 1)
    return jnp.where(j >= s, 1.0, 0.0).astype(BF16)


def _sbp_kernel(bias_ref, q_ref, k_ref, v_ref, g_ref, o_ref, o_scr, carry_scr, *, blk, sub, scale):
    h = pl.program_id(1)
    qi = pl.program_id(2)
    bias = bias_ref[h]
    q = q_ref[...].astype(BF16)
    from_uu = _from_matrix2(sub)

    def strip(k_start, width, diagonal):
        k_bf = k_ref[pl.ds(k_start, width), :].astype(BF16)
        v_bf = v_ref[pl.ds(k_start, width), :].astype(BF16)
        z = lax.dot_general(q, k_bf, NT_DIMS, preferred_element_type=F32) * scale + bias
        mask = None
        if diagonal:
            t = lax.broadcasted_iota(I32, (blk, width), 0)
            s = lax.broadcasted_iota(I32, (blk, width), 1)
            mask = s < t
        o_blk, spent = _sb_strip(z, mask, from_uu, sub, v_bf, carry_scr[:, 0:1])
        o_scr[...] += o_blk
        carry_scr[...] = jnp.broadcast_to(spent, carry_scr.shape)

    o_scr[...] = jnp.zeros_like(o_scr)
    carry_scr[...] = jnp.zeros_like(carry_scr)
    strip(pl.multiple_of(qi * blk, blk), blk, True)

    @pl.when(qi % 2 == 1)
    def _():
        strip(pl.multiple_of((qi - 1) * blk, blk), blk, False)

    def older(n, carry):
        strip(pl.multiple_of((qi // 2 - 1 - n) * (2 * blk), 2 * blk), 2 * blk, False)
        return carry

    lax.fori_loop(0, qi // 2, older, 0)
    o = o_scr[...]
    o = o * lax.rsqrt(jnp.mean(o * o, axis=-1, keepdims=True) + EPS) * g_ref[...]
    o_ref[...] = o.astype(o_ref.dtype)


def _sb_prompt(proj, sb_bias, norm_sb_out, *, bsz, seq):
    blk = min(SB_BLOCK, seq)
    nq = seq // blk
    kern = functools.partial(_sbp_kernel, blk=blk, sub=min(SB_SUB, blk), scale=D_SB ** -0.5)
    return pl.pallas_call(
        kern,
        out_shape=jax.ShapeDtypeStruct((bsz * seq, SB_W), BF16),
        grid=(bsz, H_SB, nq),
        in_specs=[pl.BlockSpec(memory_space=pltpu.SMEM),
                  pl.BlockSpec((blk, D_SB), lambda b, h, i: (b * nq + i, h)),
                  pl.BlockSpec((seq, D_SB), lambda b, h, i: (b, H_SB + h)),
                  pl.BlockSpec((seq, D_SB), lambda b, h, i: (b, 2 * H_SB + h)),
                  pl.BlockSpec((None, 1, D_SB), lambda b, h, i: (h, 0, 0))],
        out_specs=pl.BlockSpec((blk, D_SB), lambda b, h, i: (b * nq + i, h)),
        scratch_shapes=[pltpu.VMEM((blk, D_SB), F32), pltpu.VMEM((blk, LANES), F32)],
        compiler_params=_cparams(("arbitrary", "arbitrary", "arbitrary"), 48),
        name="sb_prompt",
    )(sb_bias, proj, proj, proj, norm_sb_out.reshape(H_SB, 1, D_SB))


def _head_block_mask(rows, dec_seq):
    r = lax.broadcasted_iota(I32, (rows, SB_W), 0)
    c = lax.broadcasted_iota(I32, (rows, SB_W), 1)
    return (r // dec_seq) == (c // D_SB)


def _page_heads(page_ref):
    n_keys = page_ref.shape[0] // H_SB
    cols = [page_ref[pl.ds(h, n_keys, stride=H_SB), :] for h in range(H_SB)]
    return jnp.concatenate(cols, axis=1).astype(BF16)


def _sbd_kernel(pt_ref, qbd_ref, kn_ref, vn_ref, bias_ref, g_ref, *rest,
                dec_seq, page, n_steps, scale):
    del pt_ref
    pages = rest[:2 * PAGES_PER_STEP]
    o_ref = rest[2 * PAGES_PER_STEP]
    o_scr, carry_scr, new_scr = rest[2 * PAGES_PER_STEP + 1:]
    rows = H_SB * dec_seq
    j = pl.program_id(1)
    from_uu = _from_matrix2(page)

    def sweep(k_bf, v_bf, mask, spent):
        width = k_bf.shape[0]
        z = (lax.dot_general(qbd_ref[...], k_bf, NT_DIMS, preferred_element_type=F32) * scale
             + bias_ref[:, 0:width])
        return _sb_strip(z, mask, from_uu, page, v_bf, spent)

    @pl.when(j == 0)
    def _():
        new_scr[...] = jnp.zeros_like(new_scr)
        new_scr[0, 0:dec_seq, :] = kn_ref[...]
        new_scr[1, 0:dec_seq, :] = vn_ref[...]
        r = lax.broadcasted_iota(I32, (rows, page), 0)
        s = lax.broadcasted_iota(I32, (rows, page), 1)
        o_new, spent = sweep(new_scr[0].astype(BF16), new_scr[1].astype(BF16), s < (r % dec_seq),
                             jnp.zeros((rows, 1), F32))
        o_scr[...] = o_new
        carry_scr[...] = jnp.broadcast_to(spent, carry_scr.shape)

    order = range(PAGES_PER_STEP - 1, -1, -1)
    k_bf = jnp.concatenate([_page_heads(pages[i]) for i in order], axis=0)
    v_bf = jnp.concatenate([_page_heads(pages[PAGES_PER_STEP + i]) for i in order], axis=0)
    o_blk, spent = sweep(k_bf, v_bf, None, carry_scr[:, 0:1])
    o_scr[...] += o_blk
    carry_scr[...] = jnp.broadcast_to(spent, carry_scr.shape)

    @pl.when(j == n_steps - 1)
    def _():
        o_full = jnp.where(_head_block_mask(rows, dec_seq), o_scr[...], 0.0)
        o = o_full[:, 0:D_SB]
        for h in range(1, H_SB):
            o = o + o_full[:, h * D_SB:(h + 1) * D_SB]
        o = o * lax.rsqrt(jnp.mean(o * o, axis=-1, keepdims=True) + EPS) * g_ref[...]
        o_ref[...] = o


def _sb_paged(q_s, k_new, v_new, cache_k, cache_v, page_table, sb_bias, norm_sb_out):
    bsz, dec_seq, _ = q_s.shape
    n_pool, page = cache_k.shape[0], cache_k.shape[1]
    n_pages = page_table.shape[1]
    assert n_pages % PAGES_PER_STEP == 0
    n_steps = n_pages // PAGES_PER_STEP
    rows = H_SB * dec_seq
    ck = cache_k.reshape(n_pool, page * H_SB, D_SB)
    cv = cache_v.reshape(n_pool, page * H_SB, D_SB)
    strip_w = PAGES_PER_STEP * page
    bias_rows = jnp.broadcast_to(jnp.repeat(sb_bias, dec_seq)[:, None], (rows, strip_w)).astype(F32)
    g_rows = jnp.repeat(norm_sb_out.reshape(H_SB, D_SB), dec_seq, axis=0)
    q_heads = q_s.reshape(bsz, dec_seq, H_SB, D_SB).swapaxes(1, 2)
    eye = jnp.eye(H_SB, dtype=F32)
    qbd = (q_heads[:, :, :, None, :] * eye[None, :, None, :, None]).reshape(bsz, rows, SB_W).astype(BF16)

    def page_spec(i):
        return pl.BlockSpec(
            (None, page * H_SB, D_SB),
            lambda b, j, pt: (pt[b, n_pages - 1 - (j * PAGES_PER_STEP + i)], 0, 0))

    tok_spec = pl.BlockSpec((None, dec_seq, SB_W), lambda b, j, pt: (b, 0, 0))
    kern = functools.partial(_sbd_kernel, dec_seq=dec_seq, page=page, n_steps=n_steps,
                             scale=D_SB ** -0.5)
    grid_spec = pltpu.PrefetchScalarGridSpec(
        num_scalar_prefetch=1,
        grid=(bsz, n_steps),
        in_specs=[pl.BlockSpec((None, rows, SB_W), lambda b, j, pt: (b, 0, 0)),
                  tok_spec, tok_spec,
                  pl.BlockSpec((rows, strip_w), lambda b, j, pt: (0, 0)),
                  pl.BlockSpec((rows, D_SB), lambda b, j, pt: (0, 0))]
                 + [page_spec(i) for i in range(PAGES_PER_STEP)] * 2,
        out_specs=pl.BlockSpec((None, rows, D_SB), lambda b, j, pt: (b, 0, 0)),
        scratch_shapes=[pltpu.VMEM((rows, SB_W), F32),
                        pltpu.VMEM((rows, LANES), F32),
                        pltpu.VMEM((2, page, SB_W), F32)],
    )
    return pl.pallas_call(
        kern,
        out_shape=jax.ShapeDtypeStruct((bsz, rows, D_SB), F32),
        grid_spec=grid_spec,
        compiler_params=_cparams(("arbitrary", "arbitrary"), 48),
        name="sb_paged",
    )(page_table, qbd, k_new, v_new, bias_rows, g_rows,
      *([ck] * PAGES_PER_STEP), *([cv] * PAGES_PER_STEP))


def _mlstm_kernel(q_ref, k_ref, v_ref, og_ref, gcol_ref, grow_ref, c0_ref, n0_ref, m0_ref, gn_ref,
                  h_ref, c_out, n_out, m_out, c_scr, n_scr, m_scr, *, chunk, valid, n_chunks):
    ci = pl.program_id(1)

    @pl.when(ci == 0)
    def _():
        c_scr[...] = c0_ref[...]
        n_scr[...] = n0_ref[...]
        m_scr[...] = m0_ref[...]

    t_idx = lax.broadcasted_iota(I32, (chunk, chunk), 0)
    s_idx = lax.broadcasted_iota(I32, (chunk, chunk), 1)
    causal = s_idx <= t_idx
    col_valid = lax.broadcasted_iota(I32, (chunk, 1), 0) < valid
    row_valid = lax.broadcasted_iota(I32, (1, chunk), 1) < valid
    gcol = gcol_ref[...]
    grow = grow_ref[...]
    neg_inf = -jnp.inf

    for h in range(H_ML):
        q = q_ref[:, h * DQK_ML:(h + 1) * DQK_ML]
        ks = k_ref[:, h * DQK_ML:(h + 1) * DQK_ML] * (DQK_ML ** -0.5)
        v = v_ref[:, h * DV_ML:(h + 1) * DV_ML]
        q_bf, ks_bf, v_bf = q.astype(BF16), ks.astype(BF16), v.astype(BF16)

        i_col = jnp.where(col_valid, gcol[:, h:h + 1], neg_inf)
        i_row = jnp.where(row_valid, grow[h:h + 1, :], neg_inf)
        lf_col = jnp.where(col_valid, _log_sigmoid_pair(gcol[:, H_ML + h:H_ML + h + 1])[0], 0.0)
        lf_row = jnp.where(row_valid, _log_sigmoid_pair(grow[H_ML + h:H_ML + h + 1, :])[0], 0.0)
        bcum_col = jnp.sum(jnp.where(causal, lf_row, 0.0), axis=1, keepdims=True)
        bcum_row = jnp.sum(jnp.where(t_idx <= s_idx, lf_col, 0.0), axis=0, keepdims=True)

        log_w = jnp.where(causal, bcum_col - bcum_row + i_row, neg_inf)
        m0 = m_scr[h:h + 1, 0:1]
        m_state = bcum_col + m0
        m_new = jnp.maximum(m_state, jnp.max(log_w, axis=1, keepdims=True))
        w = jnp.exp(log_w - m_new)
        g = jnp.exp(m_state - m_new)

        s_mat = w * lax.dot_general(q_bf, ks_bf, NT_DIMS, preferred_element_type=F32)
        c0 = c_scr[h]
        n0 = n_scr[h:h + 1, :]
        num = (jnp.dot(s_mat.astype(BF16), v_bf, preferred_element_type=F32)
               + g * lax.dot_general(q_bf, c0.astype(BF16), NT_DIMS, preferred_element_type=F32))
        den = (jnp.sum(s_mat, axis=1, keepdims=True)
               + g * jnp.sum(q * n0, axis=1, keepdims=True))
        hh = num / jnp.maximum(jnp.abs(den), jnp.exp(-m_new))

        m_end = m_new[chunk - 1:chunk, :]
        b_last = bcum_col[chunk - 1:chunk, :]
        w_end = jnp.exp(b_last - bcum_col + i_col - m_end)
        g_end = jnp.exp(b_last + m0 - m_end)
        vw_bf = (v * w_end).astype(BF16)
        c_scr[h] = g_end * c0 + lax.dot_general(vw_bf, ks_bf, TN_DIMS, preferred_element_type=F32)
        n_scr[h:h + 1, :] = g_end * n0 + jnp.sum(ks * w_end, axis=0, keepdims=True)
        m_scr[h:h + 1, :] = jnp.broadcast_to(m_end, (1, LANES))

        hn = (hh * lax.rsqrt(jnp.mean(hh * hh, axis=-1, keepdims=True) + EPS)
              * gn_ref[:, h * DV_ML:(h + 1) * DV_ML])
        out = hn * jax.nn.sigmoid(og_ref[:, h * DV_ML:(h + 1) * DV_ML])
        h_ref[:, h * DV_ML:(h + 1) * DV_ML] = out.astype(h_ref.dtype)

    @pl.when(ci == n_chunks - 1)
    def _():
        c_out[...] = c_scr[...]
        n_out[...] = n_scr[...]
        m_out[...] = m_scr[...]


def _mlstm(proj, gates, c0, n0, m0, norm_ml_out, *, bsz, n_chunks, chunk, valid):
    rows = bsz * n_chunks * chunk
    grow = gates[:, :N_GATES].reshape(bsz, n_chunks, chunk, N_GATES).swapaxes(2, 3)
    m0_b = jnp.broadcast_to(m0[:, :, None], (bsz, H_ML, LANES)).astype(F32)
    qk_blk = ML_QK_W // LANES
    row_map = lambda b, c: (b * n_chunks + c)
    kern = functools.partial(_mlstm_kernel, chunk=chunk, valid=valid, n_chunks=n_chunks)
    state_specs = [pl.BlockSpec((None, H_ML, DV_ML, DQK_ML), lambda b, c: (b, 0, 0, 0)),
                   pl.BlockSpec((None, H_ML, DQK_ML), lambda b, c: (b, 0, 0)),
                   pl.BlockSpec((None, H_ML, LANES), lambda b, c: (b, 0, 0))]
    q_col = (3 * SB_W) // ML_QK_W
    v_col = (3 * SB_W + 2 * ML_QK_W) // ML_V_W
    h, c1, n1, m1 = pl.pallas_call(
        kern,
        out_shape=(jax.ShapeDtypeStruct((rows, ML_V_W), BF16),
                   jax.ShapeDtypeStruct((bsz, H_ML, DV_ML, DQK_ML), F32),
                   jax.ShapeDtypeStruct((bsz, H_ML, DQK_ML), F32),
                   jax.ShapeDtypeStruct((bsz, H_ML, LANES), F32)),
        grid=(bsz, n_chunks),
        in_specs=[pl.BlockSpec((chunk, ML_QK_W), lambda b, c: (row_map(b, c), q_col)),
                  pl.BlockSpec((chunk, ML_QK_W), lambda b, c: (row_map(b, c), q_col + 1)),
                  pl.BlockSpec((chunk, ML_V_W), lambda b, c: (row_map(b, c), v_col)),
                  pl.BlockSpec((chunk, ML_V_W), lambda b, c: (row_map(b, c), v_col + 1)),
                  pl.BlockSpec((chunk, LANES), lambda b, c: (row_map(b, c), 0)),
                  pl.BlockSpec((None, None, N_GATES, chunk), lambda b, c: (b, c, 0, 0))]
                 + state_specs
                 + [pl.BlockSpec((1, ML_V_W), lambda b, c: (0, 0))],
        out_specs=(pl.BlockSpec((chunk, ML_V_W), lambda b, c: (row_map(b, c), 0)),) + tuple(state_specs),
        scratch_shapes=[pltpu.VMEM((H_ML, DV_ML, DQK_ML), F32),
                        pltpu.VMEM((H_ML, DQK_ML), F32),
                        pltpu.VMEM((H_ML, LANES), F32)],
        compiler_params=_cparams(("arbitrary", "arbitrary"), 40),
        name="mlstm",
    )(proj, proj, proj, proj, gates, grow, c0, n0, m0_b, norm_ml_out.reshape(1, ML_V_W))
    del qk_blk
    return h, c1, n1, m1[:, :, 0]


def _outproj_kernel(osb_ref, hml_ref, w_ref, x_ref, gt_ref, sh_ref, sc_ref, g_ref, wr_ref, br_ref,
                    *rest, tn, n_col):
    h_ref, xf_ref, idx_ref, gate_ref, cat_scr, h_scr = rest[-6:]
    j = pl.program_id(1)

    @pl.when(j == 0)
    def _():
        cat_scr[:, 0:SB_W] = osb_ref[...]
        cat_scr[:, SB_W:SB_W + ML_V_W] = hml_ref[...]

    mix = jnp.dot(cat_scr[...], w_ref[...].astype(BF16), preferred_element_type=F32)
    h_blk = x_ref[...] + gt_ref[...] * mix
    h_ref[...] = h_blk
    for jj in range(n_col):
        @pl.when(j == jj)
        def _(jj=jj):
            h_scr[:, jj * tn:(jj + 1) * tn] = h_blk

    @pl.when(j == n_col - 1)
    def _():
        hf = h_scr[...]
        xn = hf * lax.rsqrt(jnp.mean(hf * hf, axis=-1, keepdims=True) + EPS) * g_ref[...]
        xf = xn * (1.0 + sc_ref[...]) + sh_ref[...]
        xf_ref[...] = xf
        logits = _dot3(xf, wr_ref[...]) + br_ref[...]
        lane = lax.broadcasted_iota(I32, logits.shape, 1)
        lane_f = lane.astype(F32)
        neg_inf = -jnp.inf
        lg = jnp.where(lane < N_EXPERTS, logits, neg_inf)
        vals, idxs = [], []
        for _k in range(TOP_K):
            mx = jnp.max(lg, axis=1, keepdims=True)
            ix = jnp.min(jnp.where(lg == mx, lane_f, float(LANES)), axis=1, keepdims=True)
            vals.append(mx)
            idxs.append(ix)
            lg = jnp.where(lane_f == ix, neg_inf, lg)
        es = [jnp.exp(vk - vals[0]) for vk in vals]
        tot = es[0] + es[1] + es[2] + es[3]
        idx_out = jnp.zeros(logits.shape, F32)
        gate_out = jnp.zeros(logits.shape, F32)
        for k in range(TOP_K):
            idx_out = jnp.where(lane == k, idxs[k], idx_out)
            gate_out = jnp.where(lane == k, es[k] / tot, gate_out)
        idx_ref[...] = idx_out.astype(I32)
        gate_ref[...] = gate_out


def _outproj(o_sb, h_ml, w_out, x2d, gate_a, shift_f, scale_f, norm_ffn, wr_pad, br_pad,
             bufs, *, n_tok, tm, rows_per_group, row_off):
    t, d = x2d.shape
    tn = 512
    n_col = d // tn
    off = row_off // tm
    kern = functools.partial(_outproj_kernel, tn=tn, n_col=n_col)
    n_fixed = 10
    alias_specs, alias_args, aliases = [], (), {}
    if bufs is not None:
        alias_specs = [pl.BlockSpec(memory_space=pl.ANY)] * 2
        alias_args = tuple(bufs)
        aliases = {n_fixed: 0, n_fixed + 1: 1}
    return pl.pallas_call(
        kern,
        out_shape=(jax.ShapeDtypeStruct((n_tok, d), F32),
                   jax.ShapeDtypeStruct((n_tok, d), F32),
                   jax.ShapeDtypeStruct((t, LANES), I32),
                   jax.ShapeDtypeStruct((t, LANES), F32)),
        grid=(t // tm, n_col),
        in_specs=[pl.BlockSpec((tm, SB_W), lambda i, j: (i, 0)),
                  pl.BlockSpec((tm, ML_V_W), lambda i, j: (i, 0)),
                  pl.BlockSpec((SB_W + ML_V_W, tn), lambda i, j: (0, j)),
                  pl.BlockSpec((tm, tn), lambda i, j: (i, j)),
                  _mod_spec(gate_a, tm, rows_per_group, col_tile=tn),
                  _mod_spec(shift_f, tm, rows_per_group),
                  _mod_spec(scale_f, tm, rows_per_group),
                  pl.BlockSpec((1, d), lambda i, j: (0, 0)),
                  pl.BlockSpec((d, LANES), lambda i, j: (0, 0)),
                  pl.BlockSpec((1, LANES), lambda i, j: (0, 0))] + alias_specs,
        out_specs=(pl.BlockSpec((tm, tn), lambda i, j: (off + i, j)),
                   pl.BlockSpec((tm, d), lambda i, j: (off + i, 0)),
                   pl.BlockSpec((tm, LANES), lambda i, j: (i, 0)),
                   pl.BlockSpec((tm, LANES), lambda i, j: (i, 0))),
        scratch_shapes=[pltpu.VMEM((tm, SB_W + ML_V_W), BF16),
                        pltpu.VMEM((tm, d), F32)],
        input_output_aliases=aliases,
        compiler_params=_cparams(("arbitrary", "arbitrary"), 48),
        name="outproj",
    )(o_sb, h_ml, w_out, x2d, gate_a, shift_f, scale_f, norm_ffn.reshape(1, d), wr_pad, br_pad,
      *alias_args)


def _scatter_kernel(pos_ref, pad_ref, x_ref, o_hbm, sem, *, tok_tile):
    base = pl.program_id(0) * tok_tile * TOP_K

    @pl.when(pl.program_id(0) == 0)
    def _():
        def pad_copy(n):
            return pltpu.make_async_copy(x_ref.at[pl.ds(0, 1), :],
                                         o_hbm.at[pl.ds(pad_ref[n], 1), :], sem)

        def fill(n, carry):
            @pl.when(pad_ref[n] >= 0)
            def _():
                pad_copy(n).start()
            return carry

        def drain(n, carry):
            @pl.when(pad_ref[n] >= 0)
            def _():
                pad_copy(n).wait()
            return carry

        lax.fori_loop(0, pad_ref.shape[0], fill, 0)
        lax.fori_loop(0, pad_ref.shape[0], drain, 0)

    def token(tok, carry):
        for k in range(TOP_K):
            p = pos_ref[base + tok * TOP_K + k]
            pltpu.make_async_copy(x_ref.at[pl.ds(tok, 1), :], o_hbm.at[pl.ds(p, 1), :], sem).start()
        return carry

    lax.fori_loop(0, tok_tile, token, 0, unroll=2)
    done = o_hbm.at[pl.ds(0, tok_tile * TOP_K), :]
    pltpu.make_async_copy(done, done, sem).wait()


def _scatter_rows(pos, pad_rows, xf, n_rows):
    n_tok, w = xf.shape
    tok_tile = next(t for t in range(512, 0, -SUBLANES) if n_tok % t == 0)
    kern = functools.partial(_scatter_kernel, tok_tile=tok_tile)
    grid_spec = pltpu.PrefetchScalarGridSpec(
        num_scalar_prefetch=2,
        grid=(n_tok // tok_tile,),
        in_specs=[pl.BlockSpec((tok_tile, w), lambda i, p, q: (i, 0))],
        out_specs=pl.BlockSpec(memory_space=pl.ANY),
        scratch_shapes=[pltpu.SemaphoreType.DMA],
    )
    return pl.pallas_call(
        kern,
        out_shape=jax.ShapeDtypeStruct((n_rows, w), xf.dtype),
        grid_spec=grid_spec,
        compiler_params=_cparams(("arbitrary",), 32),
        name="moe_scatter",
    )(pos, pad_rows, xf)


def _moe_kernel(e_ref, start_ref, tiles_ref, x_hbm, wu_ref, bu_ref, wd_ref, bd_ref, y_hbm,
                x_stage, x_bf, acc, wu_bf, wd_perm, wd_bf, sem_in, sem_out, *, n_ff, d_model, n_items):
    del e_ref
    it = pl.program_id(0)
    f = pl.program_id(1)
    n_tiles = tiles_ref[it]
    row0 = pl.multiple_of(start_ref[it] * MOE_TILE, MOE_TILE)
    half = LANES // 2

    def tile_rows(t):
        return pl.ds(pl.multiple_of(t * MOE_TILE, MOE_TILE), MOE_TILE)

    def x_copy(item, t):
        src0 = pl.multiple_of(start_ref[item] * MOE_TILE, MOE_TILE)
        return pltpu.make_async_copy(x_hbm.at[pl.ds(src0 + t * MOE_TILE, MOE_TILE), :],
                                     x_stage.at[tile_rows(t), :], sem_in)

    def y_copy(t):
        return pltpu.make_async_copy(acc.at[tile_rows(t), :],
                                     y_hbm.at[pl.ds(row0 + t * MOE_TILE, MOE_TILE), :], sem_out)

    def for_tiles(count, fn):
        def body(t, carry):
            fn(t)
            return carry
        lax.fori_loop(0, count, body, 0)

    @pl.when(n_tiles > 0)
    def _():
        @pl.when(f == 0)
        def _():
            @pl.when(it == 0)
            def _():
                for_tiles(n_tiles, lambda t: x_copy(it, t).start())

            for_tiles(n_tiles, lambda t: x_copy(it, t).wait())
            b_dn = jnp.broadcast_to(bd_ref[...], (MOE_TILE, d_model))

            def init(t):
                x_bf[tile_rows(t), :] = x_stage[tile_rows(t), :].astype(BF16)
                acc[tile_rows(t), :] = b_dn

            for_tiles(n_tiles, init)
            nxt = jnp.minimum(it + 1, n_items - 1)
            n_next = jnp.where(it + 1 < n_items, tiles_ref[nxt], 0)
            for_tiles(n_next, lambda t: x_copy(nxt, t).start())

        wu_bf[...] = wu_ref[...].astype(BF16)
        for c in range(MOE_FF_TILE // LANES):
            for s in range(d_model // LANES):
                for par in range(2):
                    src = wd_ref[c * LANES + par * half:c * LANES + (par + 1) * half,
                                 s * LANES:(s + 1) * LANES]
                    wd_perm[s, pl.ds(c * LANES + par, half, stride=2), :] = src
        for s in range(d_model // LANES):
            wd_bf[:, s * LANES:(s + 1) * LANES] = wd_perm[s].astype(BF16)

        b_up = bu_ref[...]

        def span_rows(t, n):
            return pl.ds(pl.multiple_of(t * MOE_TILE, MOE_TILE), n * MOE_TILE)

        def up_proj(t, n):
            return jnp.dot(x_bf[span_rows(t, n), :], wu_bf[...],
                           preferred_element_type=F32) + b_up

        def activation(hu):
            even = (lax.broadcasted_iota(I32, (hu.shape[0], LANES), 1) % 2) == 0
            acts = []
            for c in range(MOE_FF_TILE // LANES):
                a_blk = hu[:, (2 * c) * LANES:(2 * c + 1) * LANES]
                b_blk = hu[:, (2 * c + 1) * LANES:(2 * c + 2) * LANES]
                gate = jnp.where(even, a_blk, pltpu.roll(b_blk, 1, 1))
                up = jnp.where(even, pltpu.roll(a_blk, LANES - 1, 1), b_blk)
                gate = jnp.minimum(gate, SWIGLU_LIMIT)
                up = jnp.clip(up, -SWIGLU_LIMIT, SWIGLU_LIMIT)
                acts.append((up + 1.0) * gate * jax.nn.sigmoid(SWIGLU_ALPHA * gate))
            return jnp.concatenate(acts, axis=1).astype(BF16)

        def down_proj(t, n, act):
            acc[span_rows(t, n), :] += jnp.dot(act, wd_bf[...], preferred_element_type=F32)

        def chains(spans):
            hus = [up_proj(t, n) for t, n in spans]
            acts = [activation(hu) for hu in hus]
            for (t, n), act in zip(spans, acts):
                down_proj(t, n, act)

        def run_tiles(after_tile):
            def quad(p, carry):
                chains([(4 * p, 2), (4 * p + 2, 2)])
                for u in range(4):
                    after_tile(4 * p + u)
                return carry

            n_quads = n_tiles // 4
            lax.fori_loop(0, n_quads, quad, 0)
            rest = n_tiles - 4 * n_quads

            @pl.when(rest >= 2)
            def _():
                chains([(4 * n_quads, 1), (4 * n_quads + 1, 1)])
                after_tile(4 * n_quads)
                after_tile(4 * n_quads + 1)

            @pl.when(rest % 2 == 1)
            def _():
                chains([(n_tiles - 1, 1)])
                after_tile(n_tiles - 1)

        @pl.when(f < n_ff - 1)
        def _():
            run_tiles(lambda t: None)

        @pl.when(f == n_ff - 1)
        def _():
            run_tiles(lambda t: y_copy(t).start())
            for_tiles(n_tiles, lambda t: y_copy(t).wait())


def _moe(item_e, item_start, item_tiles, x_sorted, w_up, b_up, w_down, b_down):
    n_rows = x_sorted.shape[0]
    n_exp, d, ff2_total = w_up.shape
    d_ff = ff2_total // 2
    n_ff = d_ff // MOE_FF_TILE
    n_items = item_e.shape[0]
    item_rows = MOE_ITEM_TILES * MOE_TILE
    kern = functools.partial(_moe_kernel, n_ff=n_ff, d_model=d, n_items=n_items)

    def ff_tile(i, f, n):
        return jnp.where(n[i] > 0, f, n_ff - 1)

    grid_spec = pltpu.PrefetchScalarGridSpec(
        num_scalar_prefetch=3,
        grid=(n_items, n_ff),
        in_specs=[pl.BlockSpec(memory_space=pl.ANY),
                  pl.BlockSpec((None, d, 2 * MOE_FF_TILE),
                               lambda i, f, e, s, n: (e[i], 0, ff_tile(i, f, n))),
                  pl.BlockSpec((None, 1, 2 * MOE_FF_TILE),
                               lambda i, f, e, s, n: (e[i], 0, ff_tile(i, f, n))),
                  pl.BlockSpec((None, MOE_FF_TILE, d),
                               lambda i, f, e, s, n: (e[i], ff_tile(i, f, n), 0)),
                  pl.BlockSpec((None, 1, d), lambda i, f, e, s, n: (e[i], 0, 0))],
        out_specs=pl.BlockSpec(memory_space=pl.ANY),
        scratch_shapes=[pltpu.VMEM((item_rows, d), F32),
                        pltpu.VMEM((item_rows, d), BF16),
                        pltpu.VMEM((item_rows, d), F32),
                        pltpu.VMEM((d, 2 * MOE_FF_TILE), BF16),
                        pltpu.VMEM((d // LANES, MOE_FF_TILE, LANES), F32),
                        pltpu.VMEM((MOE_FF_TILE, d), BF16),
                        pltpu.SemaphoreType.DMA,
                        pltpu.SemaphoreType.DMA],
    )
    return pl.pallas_call(
        kern,
        out_shape=jax.ShapeDtypeStruct((n_rows, d), F32),
        grid_spec=grid_spec,
        compiler_params=_cparams(("arbitrary", "arbitrary"), 56),
        name="moe_experts",
    )(item_e, item_start, item_tiles, x_sorted, w_up,
      b_up.reshape(n_exp, 1, ff2_total), w_down, b_down.reshape(n_exp, 1, d))


def _combine_kernel(pos_ref, y_hbm, h_ref, gate_ref, gt_ref, g_ref, o_ref, buf, sem, *, tok_off):
    i = pl.program_id(0)

    def issue(step, slot):
        base = (step * COMBINE_TOK + tok_off) * TOP_K

        def token(tok, carry):
            for k in range(TOP_K):
                p = pos_ref[base + tok * TOP_K + k]
                pltpu.make_async_copy(y_hbm.at[pl.ds(p, 1), :],
                                      buf.at[slot, pl.ds(k * COMBINE_TOK + tok, 1), :],
                                      sem.at[slot]).start()
            return carry

        lax.fori_loop(0, COMBINE_TOK, token, 0, unroll=2)

    @pl.when(i == 0)
    def _():
        issue(0, 0)

    @pl.when(i + 1 < pl.num_programs(0))
    def _():
        issue(i + 1, (i + 1) % 2)

    slot = i % 2
    pltpu.make_async_copy(y_hbm.at[pl.ds(0, COMBINE_TOK * TOP_K), :], buf.at[slot],
                          sem.at[slot]).wait()
    gates = gate_ref[...]
    ffn = jnp.zeros(h_ref.shape, F32)
    for k in range(TOP_K):
        ffn = ffn + gates[:, k:k + 1] * buf[slot, k * COMBINE_TOK:(k + 1) * COMBINE_TOK, :]
    y = h_ref[...] + gt_ref[...] * ffn
    o_ref[...] = y * lax.rsqrt(jnp.mean(y * y, axis=-1, keepdims=True) + EPS) * g_ref[...]


def _combine(pos, y_rows, h_buf, gate, gate_f, norm_final, *, n_tok, tok_off, rows_per_group):
    d = h_buf.shape[1]
    off = tok_off // COMBINE_TOK
    kern = functools.partial(_combine_kernel, tok_off=tok_off)
    grid_spec = pltpu.PrefetchScalarGridSpec(
        num_scalar_prefetch=1,
        grid=(n_tok // COMBINE_TOK,),
        in_specs=[pl.BlockSpec(memory_space=pl.ANY),
                  pl.BlockSpec((COMBINE_TOK, d), lambda i, p: (off + i, 0)),
                  pl.BlockSpec((COMBINE_TOK, LANES), lambda i, p: (off + i, 0)),
                  _mod_spec(gate_f, COMBINE_TOK, rows_per_group),
                  pl.BlockSpec((1, d), lambda i, p: (0, 0))],
        out_specs=pl.BlockSpec((COMBINE_TOK, d), lambda i, p: (i, 0)),
        scratch_shapes=[pltpu.VMEM((2, COMBINE_TOK * TOP_K, d), F32),
                        pltpu.SemaphoreType.DMA((2,))],
    )
    return pl.pallas_call(
        kern,
        out_shape=jax.ShapeDtypeStruct((n_tok, d), F32),
        grid_spec=grid_spec,
        compiler_params=_cparams(("arbitrary",), 32),
        name="moe_combine",
    )(pos, y_rows, h_buf, gate, gate_f, norm_final.reshape(1, d))


def _routing_tables(top_idx):
    n_tok = top_idx.shape[0]
    n_assign = n_tok * TOP_K
    flat_e = top_idx.reshape(-1)
    onehot = (flat_e[:, None] == jnp.arange(N_EXPERTS, dtype=I32)[None, :]).astype(I32)
    running = jnp.cumsum(onehot, axis=0)
    rank = jnp.take_along_axis(running, flat_e[:, None], axis=1)[:, 0] - 1
    counts = running[-1]
    tiles_e = (counts + MOE_TILE - 1) // MOE_TILE
    tile_start_e = jnp.cumsum(tiles_e) - tiles_e
    pos = tile_start_e[flat_e] * MOE_TILE + rank

    n_tiles_max = -(-n_assign // MOE_TILE) + N_EXPERTS
    n_rows = n_tiles_max * MOE_TILE
    within = jnp.arange(MOE_TILE, dtype=I32)[None, :]
    n_pad_e = tiles_e * MOE_TILE - counts
    pad_rows = jnp.where(within < n_pad_e[:, None],
                         (tile_start_e * MOE_TILE + counts)[:, None] + within, -1).reshape(-1)

    n_items_max = N_EXPERTS + n_tiles_max // MOE_ITEM_TILES
    items_e = (tiles_e + MOE_ITEM_TILES - 1) // MOE_ITEM_TILES
    items_end = jnp.cumsum(items_e)
    it = jnp.arange(n_items_max, dtype=I32)
    e_of = jnp.minimum(jnp.sum((it[:, None] >= items_end[None, :]).astype(I32), axis=1), N_EXPERTS - 1)
    local = it - (items_end[e_of] - items_e[e_of])
    valid = it < items_end[-1]
    tiles_left = tiles_e[e_of] - local * MOE_ITEM_TILES
    item_tiles = jnp.where(valid, jnp.clip(tiles_left, 0, MOE_ITEM_TILES), 0).astype(I32)
    item_start = jnp.where(valid, tile_start_e[e_of] + local * MOE_ITEM_TILES, 0).astype(I32)
    last_e = e_of[jnp.maximum(items_end[-1] - 1, 0)]
    item_e = jnp.where(valid, e_of, last_e).astype(I32)
    return pos.astype(I32), pad_rows.astype(I32), n_rows, item_e, item_start, item_tiles


def _pick_tile(n, pref):
    t = min(n, pref)
    while n % t:
        t //= 2
    return t


def kernel(x_prompt, x_sample, c_prompt, c_sample, cache_k, cache_v, page_table, state_C, state_n, state_m, w_ada, b_ada, norm_mix, norm_ffn, w_in, b_gates, sb_bias, norm_sb_out, norm_ml_out, w_out, w_router, b_router, w_up, b_up, w_down, b_down, norm_final):
    bsz, seq, d = x_prompt.shape
    dbs, dec_seq, _ = x_sample.shape
    depth = w_ada.shape[0]
    assert depth == 1
    n_p, n_s = bsz * seq, dbs * dec_seq
    n_tok = n_p + n_s

    (w_ada, b_ada, norm_mix, norm_ffn, w_in, b_gates, sb_bias, norm_sb_out, norm_ml_out, w_out,
     w_router, b_router, w_up, b_up, w_down, b_down) = [
        a[0] for a in (w_ada, b_ada, norm_mix, norm_ffn, w_in, b_gates, sb_bias, norm_sb_out,
                       norm_ml_out, w_out, w_router, b_router, w_up, b_up, w_down, b_down)]

    n_c = bsz + dbs
    n_c_pad = -(-n_c // SUBLANES) * SUBLANES
    c_all = jnp.concatenate([c_prompt, c_sample, jnp.zeros((n_c_pad - n_c, d), F32)], axis=0)
    mod = _ada(c_all, w_ada, b_ada)
    mods_p = [mod[:bsz, i * d:(i + 1) * d].reshape(bsz, 1, d) for i in range(N_MOD)]
    mods_s = [jnp.repeat(mod[bsz:n_c, i * d:(i + 1) * d], dec_seq, axis=0).reshape(1, n_s, d)
              for i in range(N_MOD)]

    wg_pad = jnp.pad(w_in[:, PROJ_MAIN_W:], ((0, 0), (0, LANES - N_GATES)))
    bg_pad = jnp.pad(b_gates, (0, LANES - N_GATES)).reshape(1, LANES)
    wr_pad = jnp.pad(w_router, ((0, 0), (0, LANES - N_EXPERTS)))
    br_pad = jnp.pad(b_router, (0, LANES - N_EXPERTS)).reshape(1, LANES)

    xp2 = x_prompt.reshape(n_p, d)
    xs2 = x_sample.reshape(n_s, d)
    tm_p = _pick_tile(seq, 1024)

    proj_p, gates_p, k_p, v_p = _inproj(xp2, mods_p[0], mods_p[1], norm_mix, w_in, wg_pad, bg_pad,
                                        tm=tm_p, rows_per_group=seq)
    osb_p = _sb_prompt(proj_p, sb_bias, norm_sb_out, bsz=bsz, seq=seq)
    chunk_p = _pick_tile(seq, ML_CHUNK)
    hml_p, c_p, nn_p, m_p = _mlstm(
        proj_p, gates_p,
        jnp.zeros((bsz, H_ML, DV_ML, DQK_ML), F32), jnp.zeros((bsz, H_ML, DQK_ML), F32),
        jnp.zeros((bsz, H_ML), F32), norm_ml_out,
        bsz=bsz, n_chunks=seq // chunk_p, chunk=chunk_p, valid=chunk_p)

    proj_s, gates_s, k_s, v_s = _inproj(xs2, mods_s[0], mods_s[1], norm_mix, w_in, wg_pad, bg_pad,
                                        tm=n_s, rows_per_group=n_s)
    osb_s = _sb_paged(proj_s[:, :SB_W].reshape(dbs, dec_seq, SB_W),
                      k_s.reshape(dbs, dec_seq, SB_W), v_s.reshape(dbs, dec_seq, SB_W),
                      cache_k[0], cache_v[0], page_table, sb_bias, norm_sb_out)
    osb_s = (osb_s.reshape(dbs, H_SB, dec_seq, D_SB).swapaxes(1, 2)
             .reshape(n_s, SB_W).astype(BF16))
    chunk_s = -(-dec_seq // BF16_SUBLANES) * BF16_SUBLANES
    pad_rows = lambda a: jnp.pad(a.reshape(dbs, dec_seq, a.shape[-1]),
                                 ((0, 0), (0, chunk_s - dec_seq), (0, 0))).reshape(dbs * chunk_s, a.shape[-1])
    hml_s, c_s, nn_s, m_s = _mlstm(
        pad_rows(proj_s), pad_rows(gates_s), state_C[0], state_n[0], state_m[0], norm_ml_out,
        bsz=dbs, n_chunks=1, chunk=chunk_s, valid=dec_seq)
    hml_s = hml_s.reshape(dbs, chunk_s, ML_V_W)[:, :dec_seq].reshape(n_s, ML_V_W)

    tm_o = _pick_tile(seq, 512)
    h_buf, xf_buf, idx_p, gate_p = _outproj(
        osb_p, hml_p, w_out, xp2, mods_p[2], mods_p[3], mods_p[4], norm_ffn, wr_pad, br_pad,
        None, n_tok=n_tok, tm=tm_o, rows_per_group=seq, row_off=0)
    h_buf, xf_buf, idx_s, gate_s = _outproj(
        osb_s, hml_s, w_out, xs2, mods_s[2], mods_s[3], mods_s[4], norm_ffn, wr_pad, br_pad,
        (h_buf, xf_buf), n_tok=n_tok, tm=n_s, rows_per_group=n_s, row_off=n_p)
    top_idx = jnp.concatenate([idx_p[:, :TOP_K], idx_s[:, :TOP_K]], axis=0)
    gate_all = jnp.concatenate([gate_p, gate_s], axis=0)

    pos, pad_rows, n_rows, item_e, item_start, item_tiles = _routing_tables(top_idx)
    x_sorted = _scatter_rows(pos, pad_rows, xf_buf, n_rows)
    y_rows = _moe(item_e, item_start, item_tiles, x_sorted, w_up, b_up, w_down, b_down)
    y_p = _combine(pos, y_rows, h_buf, gate_all, mods_p[5], norm_final,
                   n_tok=n_p, tok_off=0, rows_per_group=seq)
    y_s = _combine(pos, y_rows, h_buf, gate_all, mods_s[5], norm_final,
                   n_tok=n_s, tok_off=n_p, rows_per_group=n_s)

    return (y_p.reshape(bsz, seq, d), y_s.reshape(dbs, dec_seq, d),
            k_p.reshape(1, bsz, seq, H_SB, D_SB), v_p.reshape(1, bsz, seq, H_SB, D_SB),
            k_s.reshape(1, dbs, dec_seq, H_SB, D_SB), v_s.reshape(1, dbs, dec_seq, H_SB, D_SB),
            c_p[None], nn_p[None], m_p[None], c_s[None], nn_s[None], m_s[None])
```

```python
import functools

import jax
import jax.numpy as jnp
from jax import lax
from jax.experimental import pallas as pl
from jax.experimental.pallas import tpu as pltpu

F32 = jnp.float32
BF16 = jnp.bfloat16
I32 = jnp.int32

H_SB = 8
D_SB = 128
SB_W = H_SB * D_SB
H_ML = 4
DQK_ML = 128
DV_ML = 256
ML_QK_W = H_ML * DQK_ML
ML_V_W = H_ML * DV_ML
PROJ_MAIN_W = 3 * SB_W + 2 * ML_QK_W + 2 * ML_V_W
N_GATES = 2 * H_ML
N_EXPERTS = 32
TOP_K = 4
N_MOD = 6
SWIGLU_LIMIT = 7.0
SWIGLU_ALPHA = 1.702
EPS = 1e-6

LANES = 128
SUBLANES = 8
BF16_SUBLANES = 16
MIB = 1024 * 1024

MOE_TILE = 128
MOE_ITEM_TILES = 12
MOE_FF_TILE = 256
ML_CHUNK = 128
SB_BLOCK = 512
SB_SUB = 256
PAGES_PER_STEP = 8
COMBINE_TOK = 64

NT_DIMS = (((1,), (1,)), ((), ()))
TN_DIMS = (((0,), (0,)), ((), ()))


def _cparams(sem, vmem_mib):
    return pltpu.CompilerParams(dimension_semantics=sem, vmem_limit_bytes=vmem_mib * MIB)


def _log_sigmoid_pair(z):
    t = jnp.log(1.0 + jnp.exp(-jnp.abs(z)))
    return jnp.minimum(z, 0.0) - t, -jnp.maximum(z, 0.0) - t


def _split_bf16(x):
    hi = x.astype(BF16)
    lo = (x - hi.astype(F32)).astype(BF16)
    return hi, lo


def _dot3(x, w):
    xh, xl = _split_bf16(x)
    wh, wl = _split_bf16(w)
    return (jnp.dot(xh, wh, preferred_element_type=F32)
            + (jnp.dot(xl, wh, preferred_element_type=F32)
               + jnp.dot(xh, wl, preferred_element_type=F32)))


def _mod_spec(arr, tile, rows_per_group, col_tile=None):
    per_group = arr.shape[1] == 1
    width = arr.shape[2] if col_tile is None else col_tile

    def index_map(i, *rest):
        col = 0 if col_tile is None else rest[0]
        if per_group:
            return ((i * tile) // rows_per_group, 0, col)
        return (0, i, col)

    return pl.BlockSpec((None, 1 if per_group else tile, width), index_map)


def _ada_kernel(c_ref, w_ref, b_ref, o_ref):
    c = c_ref[...]
    o_ref[...] = _dot3(c * jax.nn.sigmoid(c), w_ref[...]) + b_ref[...]


def _ada(c_all, w_ada, b_ada):
    n, d = c_all.shape
    w_out = w_ada.shape[1]
    tn = 1024
    return pl.pallas_call(
        _ada_kernel,
        out_shape=jax.ShapeDtypeStruct((n, w_out), F32),
        grid=(w_out // tn,),
        in_specs=[pl.BlockSpec((n, d), lambda j: (0, 0)),
                  pl.BlockSpec((d, tn), lambda j: (0, j)),
                  pl.BlockSpec((1, tn), lambda j: (0, j))],
        out_specs=pl.BlockSpec((n, tn), lambda j: (0, j)),
        compiler_params=_cparams(("arbitrary",), 40),
        name="ada",
    )(c_all, w_ada, b_ada.reshape(1, w_out))


def _inproj_kernel(x_ref, sh_ref, sc_ref, g_ref, w_ref, wg_ref, bg_ref,
                   proj_ref, gates_ref, k_ref, v_ref, xm_scr, *, tn):
    j = pl.program_id(1)

    @pl.when(j == 0)
    def _():
        x = x_ref[...]
        xn = x * lax.rsqrt(jnp.mean(x * x, axis=-1, keepdims=True) + EPS) * g_ref[...]
        xm = (xn * (1.0 + sc_ref[...]) + sh_ref[...]).astype(BF16)
        xm_scr[...] = xm
        gates_ref[...] = jnp.dot(xm, wg_ref[...].astype(BF16),
                                 preferred_element_type=F32) + bg_ref[...]

    res = jnp.dot(xm_scr[...], w_ref[...].astype(BF16), preferred_element_type=F32)
    proj_ref[...] = res
    k_tiles = SB_W // tn

    @pl.when((j >= k_tiles) & (j < 2 * k_tiles))
    def _():
        k_ref[...] = res

    @pl.when((j >= 2 * k_tiles) & (j < 3 * k_tiles))
    def _():
        v_ref[...] = res


def _inproj(x2d, shift, scale, norm_w, w_in, wg_pad, bg_pad, *, tm, rows_per_group):
    t, d = x2d.shape
    tn = 512
    k_tiles = SB_W // tn
    mod_spec = _mod_spec(shift, tm, rows_per_group)

    def kv_spec(first):
        return pl.BlockSpec((tm, tn), lambda i, j: (i, jnp.clip(j - first, 0, k_tiles - 1)))

    return pl.pallas_call(
        functools.partial(_inproj_kernel, tn=tn),
        out_shape=(jax.ShapeDtypeStruct((t, PROJ_MAIN_W), F32),
                   jax.ShapeDtypeStruct((t, LANES), F32),
                   jax.ShapeDtypeStruct((t, SB_W), F32),
                   jax.ShapeDtypeStruct((t, SB_W), F32)),
        grid=(t // tm, PROJ_MAIN_W // tn),
        in_specs=[pl.BlockSpec((tm, d), lambda i, j: (i, 0)),
                  mod_spec, mod_spec,
                  pl.BlockSpec((1, d), lambda i, j: (0, 0)),
                  pl.BlockSpec((d, tn), lambda i, j: (0, j)),
                  pl.BlockSpec((d, LANES), lambda i, j: (0, 0)),
                  pl.BlockSpec((1, LANES), lambda i, j: (0, 0))],
        out_specs=(pl.BlockSpec((tm, tn), lambda i, j: (i, j)),
                   pl.BlockSpec((tm, LANES), lambda i, j: (i, 0)),
                   kv_spec(k_tiles), kv_spec(2 * k_tiles)),
        scratch_shapes=[pltpu.VMEM((tm, d), BF16)],
        compiler_params=_cparams(("arbitrary", "arbitrary"), 48),
        name="inproj",
    )(x2d, shift, scale, norm_w.reshape(1, d), w_in, wg_pad, bg_pad)


def _sb_strip(z, mask, from_uu, sub, v_bf, spent):
    pos = jnp.maximum(z, 0.0)
    cost = pos + jnp.log(1.0 + jnp.exp(jnp.minimum(z, 0.0) - pos))
    if mask is not None:
        cost = jnp.where(mask, cost, 0.0)
    hi, lo = _split_bf16(cost)
    n_sub = z.shape[1] // sub
    parts = [None] * n_sub
    for j in reversed(range(n_sub)):
        cols = slice(j * sub, (j + 1) * sub)
        from_s = jnp.dot(jnp.concatenate([hi[:, cols], lo[:, cols]], axis=1), from_uu,
                         preferred_element_type=F32)
        parts[j] = jnp.exp(z[:, cols] - (from_s + spent))
        spent = spent + from_s[:, 0:1]
    a = parts[0] if n_sub == 1 else jnp.concatenate(parts, axis=1)
    if mask is not None:
        a = jnp.where(mask, a, 0.0)
    return jnp.dot(a.astype(BF16), v_bf, preferred_element_type=F32), spent


def _from_matrix2(n):
    j = lax.broadcasted_iota(I32, (2 * n, n), 0) % n
    s = lax.broadcasted_iota(I32, (2 * n, n), 1)
    return jnp.where(j >= s, 1.0, 0.0).astype(BF16)


def _sbp_kernel(bias_ref, q_ref, k_ref, v_ref, g_ref, o_ref, o_scr, carry_scr, *, blk, sub, scale):
    h = pl.program_id(1)
    qi = pl.program_id(2)
    bias = bias_ref[h]
    q = q_ref[...].astype(BF16)
    from_uu = _from_matrix2(sub)

    def strip(k_start, width, diagonal):
        k_bf = k_ref[pl.ds(k_start, width), :].astype(BF16)
        v_bf = v_ref[pl.ds(k_start, width), :].astype(BF16)
        z = lax.dot_general(q, k_bf, NT_DIMS, preferred_element_type=F32) * scale + bias
        mask = None
        if diagonal:
            t = lax.broadcasted_iota(I32, (blk, width), 0)
            s = lax.broadcasted_iota(I32, (blk, width), 1)
            mask = s < t
        o_blk, spent = _sb_strip(z, mask, from_uu, sub, v_bf, carry_scr[:, 0:1])
        o_scr[...] += o_blk
        carry_scr[...] = jnp.broadcast_to(spent, carry_scr.shape)

    o_scr[...] = jnp.zeros_like(o_scr)
    carry_scr[...] = jnp.zeros_like(carry_scr)
    strip(pl.multiple_of(qi * blk, blk), blk, True)

    @pl.when(qi % 2 == 1)
    def _():
        strip(pl.multiple_of((qi - 1) * blk, blk), blk, False)

    def older(n, carry):
        strip(pl.multiple_of((qi // 2 - 1 - n) * (2 * blk), 2 * blk), 2 * blk, False)
        return carry

    lax.fori_loop(0, qi // 2, older, 0)
    o = o_scr[...]
    o = o * lax.rsqrt(jnp.mean(o * o, axis=-1, keepdims=True) + EPS) * g_ref[...]
    o_ref[...] = o.astype(o_ref.dtype)


def _sb_prompt(proj, sb_bias, norm_sb_out, *, bsz, seq):
    blk = min(SB_BLOCK, seq)
    nq = seq // blk
    kern = functools.partial(_sbp_kernel, blk=blk, sub=min(SB_SUB, blk), scale=D_SB ** -0.5)
    return pl.pallas_call(
        kern,
        out_shape=jax.ShapeDtypeStruct((bsz * seq, SB_W), BF16),
        grid=(bsz, H_SB, nq),
        in_specs=[pl.BlockSpec(memory_space=pltpu.SMEM),
                  pl.BlockSpec((blk, D_SB), lambda b, h, i: (b * nq + i, h)),
                  pl.BlockSpec((seq, D_SB), lambda b, h, i: (b, H_SB + h)),
                  pl.BlockSpec((seq, D_SB), lambda b, h, i: (b, 2 * H_SB + h)),
                  pl.BlockSpec((None, 1, D_SB), lambda b, h, i: (h, 0, 0))],
        out_specs=pl.BlockSpec((blk, D_SB), lambda b, h, i: (b * nq + i, h)),
        scratch_shapes=[pltpu.VMEM((blk, D_SB), F32), pltpu.VMEM((blk, LANES), F32)],
        compiler_params=_cparams(("arbitrary", "arbitrary", "arbitrary"), 48),
        name="sb_prompt",
    )(sb_bias, proj, proj, proj, norm_sb_out.reshape(H_SB, 1, D_SB))


def _head_block_mask(rows, dec_seq):
    r = lax.broadcasted_iota(I32, (rows, SB_W), 0)
    c = lax.broadcasted_iota(I32, (rows, SB_W), 1)
    return (r // dec_seq) == (c // D_SB)


def _page_heads(page_ref):
    n_keys = page_ref.shape[0] // H_SB
    cols = [page_ref[pl.ds(h, n_keys, stride=H_SB), :] for h in range(H_SB)]
    return jnp.concatenate(cols, axis=1).astype(BF16)


def _sbd_kernel(pt_ref, qbd_ref, kn_ref, vn_ref, bias_ref, g_ref, *rest,
                dec_seq, page, n_steps, scale):
    del pt_ref
    pages = rest[:2 * PAGES_PER_STEP]
    o_ref = rest[2 * PAGES_PER_STEP]
    o_scr, carry_scr, new_scr = rest[2 * PAGES_PER_STEP + 1:]
    rows = H_SB * dec_seq
    j = pl.program_id(1)
    from_uu = _from_matrix2(page)

    def sweep(k_bf, v_bf, mask, spent):
        width = k_bf.shape[0]
        z = (lax.dot_general(qbd_ref[...], k_bf, NT_DIMS, preferred_element_type=F32) * scale
             + bias_ref[:, 0:width])
        return _sb_strip(z, mask, from_uu, page, v_bf, spent)

    @pl.when(j == 0)
    def _():
        new_scr[...] = jnp.zeros_like(new_scr)
        new_scr[0, 0:dec_seq, :] = kn_ref[...]
        new_scr[1, 0:dec_seq, :] = vn_ref[...]
        r = lax.broadcasted_iota(I32, (rows, page), 0)
        s = lax.broadcasted_iota(I32, (rows, page), 1)
        o_new, spent = sweep(new_scr[0].astype(BF16), new_scr[1].astype(BF16), s < (r % dec_seq),
                             jnp.zeros((rows, 1), F32))
        o_scr[...] = o_new
        carry_scr[...] = jnp.broadcast_to(spent, carry_scr.shape)

    order = range(PAGES_PER_STEP - 1, -1, -1)
    k_bf = jnp.concatenate([_page_heads(pages[i]) for i in order], axis=0)
    v_bf = jnp.concatenate([_page_heads(pages[PAGES_PER_STEP + i]) for i in order], axis=0)
    o_blk, spent = sweep(k_bf, v_bf, None, carry_scr[:, 0:1])
    o_scr[...] += o_blk
    carry_scr[...] = jnp.broadcast_to(spent, carry_scr.shape)

    @pl.when(j == n_steps - 1)
    def _():
        o_full = jnp.where(_head_block_mask(rows, dec_seq), o_scr[...], 0.0)
        o = o_full[:, 0:D_SB]
        for h in range(1, H_SB):
            o = o + o_full[:, h * D_SB:(h + 1) * D_SB]
        o = o * lax.rsqrt(jnp.mean(o * o, axis=-1, keepdims=True) + EPS) * g_ref[...]
        o_ref[...] = o


def _sb_paged(q_s, k_new, v_new, cache_k, cache_v, page_table, sb_bias, norm_sb_out):
    bsz, dec_seq, _ = q_s.shape
    n_pool, page = cache_k.shape[0], cache_k.shape[1]
    n_pages = page_table.shape[1]
    assert n_pages % PAGES_PER_STEP == 0
    n_steps = n_pages // PAGES_PER_STEP
    rows = H_SB * dec_seq
    ck = cache_k.reshape(n_pool, page * H_SB, D_SB)
    cv = cache_v.reshape(n_pool, page * H_SB, D_SB)
    strip_w = PAGES_PER_STEP * page
    bias_rows = jnp.broadcast_to(jnp.repeat(sb_bias, dec_seq)[:, None], (rows, strip_w)).astype(F32)
    g_rows = jnp.repeat(norm_sb_out.reshape(H_SB, D_SB), dec_seq, axis=0)
    q_heads = q_s.reshape(bsz, dec_seq, H_SB, D_SB).swapaxes(1, 2)
    eye = jnp.eye(H_SB, dtype=F32)
    qbd = (q_heads[:, :, :, None, :] * eye[None, :, None, :, None]).reshape(bsz, rows, SB_W).astype(BF16)

    def page_spec(i):
        return pl.BlockSpec(
            (None, page * H_SB, D_SB),
            lambda b, j, pt: (pt[b, n_pages - 1 - (j * PAGES_PER_STEP + i)], 0, 0))

    tok_spec = pl.BlockSpec((None, dec_seq, SB_W), lambda b, j, pt: (b, 0, 0))
    kern = functools.partial(_sbd_kernel, dec_seq=dec_seq, page=page, n_steps=n_steps,
                             scale=D_SB ** -0.5)
    grid_spec = pltpu.PrefetchScalarGridSpec(
        num_scalar_prefetch=1,
        grid=(bsz, n_steps),
        in_specs=[pl.BlockSpec((None, rows, SB_W), lambda b, j, pt: (b, 0, 0)),
                  tok_spec, tok_spec,
                  pl.BlockSpec((rows, strip_w), lambda b, j, pt: (0, 0)),
                  pl.BlockSpec((rows, D_SB), lambda b, j, pt: (0, 0))]
                 + [page_spec(i) for i in range(PAGES_PER_STEP)] * 2,
        out_specs=pl.BlockSpec((None, rows, D_SB), lambda b, j, pt: (b, 0, 0)),
        scratch_shapes=[pltpu.VMEM((rows, SB_W), F32),
                        pltpu.VMEM((rows, LANES), F32),
                        pltpu.VMEM((2, page, SB_W), F32)],
    )
    return pl.pallas_call(
        kern,
        out_shape=jax.ShapeDtypeStruct((bsz, rows, D_SB), F32),
        grid_spec=grid_spec,
        compiler_params=_cparams(("arbitrary", "arbitrary"), 48),
        name="sb_paged",
    )(page_table, qbd, k_new, v_new, bias_rows, g_rows,
      *([ck] * PAGES_PER_STEP), *([cv] * PAGES_PER_STEP))


def _mlstm_kernel(q_ref, k_ref, v_ref, og_ref, gcol_ref, grow_ref, c0_ref, n0_ref, m0_ref, gn_ref,
                  h_ref, c_out, n_out, m_out, c_scr, n_scr, m_scr, *, chunk, valid, n_chunks):
    ci = pl.program_id(1)

    @pl.when(ci == 0)
    def _():
        c_scr[...] = c0_ref[...]
        n_scr[...] = n0_ref[...]
        m_scr[...] = m0_ref[...]

    t_idx = lax.broadcasted_iota(I32, (chunk, chunk), 0)
    s_idx = lax.broadcasted_iota(I32, (chunk, chunk), 1)
    causal = s_idx <= t_idx
    col_valid = lax.broadcasted_iota(I32, (chunk, 1), 0) < valid
    row_valid = lax.broadcasted_iota(I32, (1, chunk), 1) < valid
    gcol = gcol_ref[...]
    grow = grow_ref[...]
    neg_inf = -jnp.inf

    for h in range(H_ML):
        q = q_ref[:, h * DQK_ML:(h + 1) * DQK_ML]
        ks = k_ref[:, h * DQK_ML:(h + 1) * DQK_ML] * (DQK_ML ** -0.5)
        v = v_ref[:, h * DV_ML:(h + 1) * DV_ML]
        q_bf, ks_bf, v_bf = q.astype(BF16), ks.astype(BF16), v.astype(BF16)

        i_col = jnp.where(col_valid, gcol[:, h:h + 1], neg_inf)
        i_row = jnp.where(row_valid, grow[h:h + 1, :], neg_inf)
        lf_col = jnp.where(col_valid, _log_sigmoid_pair(gcol[:, H_ML + h:H_ML + h + 1])[0], 0.0)
        lf_row = jnp.where(row_valid, _log_sigmoid_pair(grow[H_ML + h:H_ML + h + 1, :])[0], 0.0)
        bcum_col = jnp.sum(jnp.where(causal, lf_row, 0.0), axis=1, keepdims=True)
        bcum_row = jnp.sum(jnp.where(t_idx <= s_idx, lf_col, 0.0), axis=0, keepdims=True)

        log_w = jnp.where(causal, bcum_col - bcum_row + i_row, neg_inf)
        m0 = m_scr[h:h + 1, 0:1]
        m_state = bcum_col + m0
        m_new = jnp.maximum(m_state, jnp.max(log_w, axis=1, keepdims=True))
        w = jnp.exp(log_w - m_new)
        g = jnp.exp(m_state - m_new)

        s_mat = w * lax.dot_general(q_bf, ks_bf, NT_DIMS, preferred_element_type=F32)
        c0 = c_scr[h]
        n0 = n_scr[h:h + 1, :]
        num = (jnp.dot(s_mat.astype(BF16), v_bf, preferred_element_type=F32)
               + g * lax.dot_general(q_bf, c0.astype(BF16), NT_DIMS, preferred_element_type=F32))
        den = (jnp.sum(s_mat, axis=1, keepdims=True)
               + g * jnp.sum(q * n0, axis=1, keepdims=True))
        hh = num / jnp.maximum(jnp.abs(den), jnp.exp(-m_new))

        m_end = m_new[chunk - 1:chunk, :]
        b_last = bcum_col[chunk - 1:chunk, :]
        w_end = jnp.exp(b_last - bcum_col + i_col - m_end)
        g_end = jnp.exp(b_last + m0 - m_end)
        vw_bf = (v * w_end).astype(BF16)
        c_scr[h] = g_end * c0 + lax.dot_general(vw_bf, ks_bf, TN_DIMS, preferred_element_type=F32)
        n_scr[h:h + 1, :] = g_end * n0 + jnp.sum(ks * w_end, axis=0, keepdims=True)
        m_scr[h:h + 1, :] = jnp.broadcast_to(m_end, (1, LANES))

        hn = (hh * lax.rsqrt(jnp.mean(hh * hh, axis=-1, keepdims=True) + EPS)
              * gn_ref[:, h * DV_ML:(h + 1) * DV_ML])
        out = hn * jax.nn.sigmoid(og_ref[:, h * DV_ML:(h + 1) * DV_ML])
        h_ref[:, h * DV_ML:(h + 1) * DV_ML] = out.astype(h_ref.dtype)

    @pl.when(ci == n_chunks - 1)
    def _():
        c_out[...] = c_scr[...]
        n_out[...] = n_scr[...]
        m_out[...] = m_scr[...]


def _mlstm(proj, gates, c0, n0, m0, norm_ml_out, *, bsz, n_chunks, chunk, valid):
    rows = bsz * n_chunks * chunk
    grow = gates[:, :N_GATES].reshape(bsz, n_chunks, chunk, N_GATES).swapaxes(2, 3)
    m0_b = jnp.broadcast_to(m0[:, :, None], (bsz, H_ML, LANES)).astype(F32)
    qk_blk = ML_QK_W // LANES
    row_map = lambda b, c: (b * n_chunks + c)
    kern = functools.partial(_mlstm_kernel, chunk=chunk, valid=valid, n_chunks=n_chunks)
    state_specs = [pl.BlockSpec((None, H_ML, DV_ML, DQK_ML), lambda b, c: (b, 0, 0, 0)),
                   pl.BlockSpec((None, H_ML, DQK_ML), lambda b, c: (b, 0, 0)),
                   pl.BlockSpec((None, H_ML, LANES), lambda b, c: (b, 0, 0))]
    q_col = (3 * SB_W) // ML_QK_W
    v_col = (3 * SB_W + 2 * ML_QK_W) // ML_V_W
    h, c1, n1, m1 = pl.pallas_call(
        kern,
        out_shape=(jax.ShapeDtypeStruct((rows, ML_V_W), BF16),
                   jax.ShapeDtypeStruct((bsz, H_ML, DV_ML, DQK_ML), F32),
                   jax.ShapeDtypeStruct((bsz, H_ML, DQK_ML), F32),
                   jax.ShapeDtypeStruct((bsz, H_ML, LANES), F32)),
        grid=(bsz, n_chunks),
        in_specs=[pl.BlockSpec((chunk, ML_QK_W), lambda b, c: (row_map(b, c), q_col)),
                  pl.BlockSpec((chunk, ML_QK_W), lambda b, c: (row_map(b, c), q_col + 1)),
                  pl.BlockSpec((chunk, ML_V_W), lambda b, c: (row_map(b, c), v_col)),
                  pl.BlockSpec((chunk, ML_V_W), lambda b, c: (row_map(b, c), v_col + 1)),
                  pl.BlockSpec((chunk, LANES), lambda b, c: (row_map(b, c), 0)),
                  pl.BlockSpec((None, None, N_GATES, chunk), lambda b, c: (b, c, 0, 0))]
                 + state_specs
                 + [pl.BlockSpec((1, ML_V_W), lambda b, c: (0, 0))],
        out_specs=(pl.BlockSpec((chunk, ML_V_W), lambda b, c: (row_map(b, c), 0)),) + tuple(state_specs),
        scratch_shapes=[pltpu.VMEM((H_ML, DV_ML, DQK_ML), F32),
                        pltpu.VMEM((H_ML, DQK_ML), F32),
                        pltpu.VMEM((H_ML, LANES), F32)],
        compiler_params=_cparams(("arbitrary", "arbitrary"), 40),
        name="mlstm",
    )(proj, proj, proj, proj, gates, grow, c0, n0, m0_b, norm_ml_out.reshape(1, ML_V_W))
    del qk_blk
    return h, c1, n1, m1[:, :, 0]


def _outproj_kernel(osb_ref, hml_ref, w_ref, x_ref, gt_ref, sh_ref, sc_ref, g_ref, wr_ref, br_ref,
                    h_ref, xf_ref, idx_ref, gate_ref, cat_scr, h_scr, *, tn, n_col):
    j = pl.program_id(1)

    @pl.when(j == 0)
    def _():
        cat_scr[:, 0:SB_W] = osb_ref[...]
        cat_scr[:, SB_W:SB_W + ML_V_W] = hml_ref[...]

    mix = jnp.dot(cat_scr[...], w_ref[...].astype(BF16), preferred_element_type=F32)
    h_blk = x_ref[...] + gt_ref[...] * mix
    h_ref[...] = h_blk
    for jj in range(n_col):
        @pl.when(j == jj)
        def _(jj=jj):
            h_scr[:, jj * tn:(jj + 1) * tn] = h_blk

    @pl.when(j == n_col - 1)
    def _():
        hf = h_scr[...]
        xn = hf * lax.rsqrt(jnp.mean(hf * hf, axis=-1, keepdims=True) + EPS) * g_ref[...]
        xf = xn * (1.0 + sc_ref[...]) + sh_ref[...]
        xf_ref[...] = xf
        logits = _dot3(xf, wr_ref[...]) + br_ref[...]
        lane = lax.broadcasted_iota(I32, logits.shape, 1)
        lane_f = lane.astype(F32)
        neg_inf = -jnp.inf
        lg = jnp.where(lane < N_EXPERTS, logits, neg_inf)
        vals, idxs = [], []
        for _k in range(TOP_K):
            mx = jnp.max(lg, axis=1, keepdims=True)
            ix = jnp.min(jnp.where(lg == mx, lane_f, float(LANES)), axis=1, keepdims=True)
            vals.append(mx)
            idxs.append(ix)
            lg = jnp.where(lane_f == ix, neg_inf, lg)
        es = [jnp.exp(vk - vals[0]) for vk in vals]
        tot = es[0] + es[1] + es[2] + es[3]
        idx_out = jnp.zeros(logits.shape, F32)
        gate_out = jnp.zeros(logits.shape, F32)
        for k in range(TOP_K):
            idx_out = jnp.where(lane == k, idxs[k], idx_out)
            gate_out = jnp.where(lane == k, es[k] / tot, gate_out)
        idx_ref[...] = idx_out.astype(I32)
        gate_ref[...] = gate_out


def _outproj(o_sb, h_ml, w_out, x2d, gate_a, shift_f, scale_f, norm_ffn, wr_pad, br_pad,
             *, tm, rows_per_group):
    t, d = x2d.shape
    tn = 512
    n_col = d // tn
    kern = functools.partial(_outproj_kernel, tn=tn, n_col=n_col)
    return pl.pallas_call(
        kern,
        out_shape=(jax.ShapeDtypeStruct((t, d), F32),
                   jax.ShapeDtypeStruct((t, d), F32),
                   jax.ShapeDtypeStruct((t, LANES), I32),
                   jax.ShapeDtypeStruct((t, LANES), F32)),
        grid=(t // tm, n_col),
        in_specs=[pl.BlockSpec((tm, SB_W), lambda i, j: (i, 0)),
                  pl.BlockSpec((tm, ML_V_W), lambda i, j: (i, 0)),
                  pl.BlockSpec((SB_W + ML_V_W, tn), lambda i, j: (0, j)),
                  pl.BlockSpec((tm, tn), lambda i, j: (i, j)),
                  _mod_spec(gate_a, tm, rows_per_group, col_tile=tn),
                  _mod_spec(shift_f, tm, rows_per_group),
                  _mod_spec(scale_f, tm, rows_per_group),
                  pl.BlockSpec((1, d), lambda i, j: (0, 0)),
                  pl.BlockSpec((d, LANES), lambda i, j: (0, 0)),
                  pl.BlockSpec((1, LANES), lambda i, j: (0, 0))],
        out_specs=(pl.BlockSpec((tm, tn), lambda i, j: (i, j)),
                   pl.BlockSpec((tm, d), lambda i, j: (i, 0)),
                   pl.BlockSpec((tm, LANES), lambda i, j: (i, 0)),
                   pl.BlockSpec((tm, LANES), lambda i, j: (i, 0))),
        scratch_shapes=[pltpu.VMEM((tm, SB_W + ML_V_W), BF16),
                        pltpu.VMEM((tm, d), F32)],
        compiler_params=_cparams(("arbitrary", "arbitrary"), 48),
        name="outproj",
    )(o_sb, h_ml, w_out, x2d, gate_a, shift_f, scale_f, norm_ffn.reshape(1, d), wr_pad, br_pad)


def _scatter_kernel(pos_ref, last_ref, xp_ref, xs_ref, o_hbm, sem, *, tok_tile, n_p, n_s):
    i = pl.program_id(0)
    fill_rows = min(MOE_TILE, tok_tile)

    @pl.when(i == 0)
    def _():
        def fill_copy(e, part):
            row = last_ref[e] * MOE_TILE + part * fill_rows
            return pltpu.make_async_copy(xp_ref.at[pl.ds(0, fill_rows), :],
                                         o_hbm.at[pl.ds(row, fill_rows), :], sem)

        def fill(e, carry):
            @pl.when(last_ref[e] >= 0)
            def _():
                for part in range(MOE_TILE // fill_rows):
                    fill_copy(e, part).start()
            return carry

        def drain(e, carry):
            @pl.when(last_ref[e] >= 0)
            def _():
                for part in range(MOE_TILE // fill_rows):
                    fill_copy(e, part).wait()
            return carry

        lax.fori_loop(0, N_EXPERTS, fill, 0)
        lax.fori_loop(0, N_EXPERTS, drain, 0)

    def scatter_block(x_ref, n_rows, base):
        def token(tok, carry):
            for k in range(TOP_K):
                p = pos_ref[base + tok * TOP_K + k]
                pltpu.make_async_copy(x_ref.at[pl.ds(tok, 1), :], o_hbm.at[pl.ds(p, 1), :],
                                      sem).start()
            return carry

        lax.fori_loop(0, n_rows, token, 0, unroll=2)
        done = o_hbm.at[pl.ds(0, n_rows * TOP_K), :]
        pltpu.make_async_copy(done, done, sem).wait()

    @pl.when(i < n_p // tok_tile)
    def _():
        scatter_block(xp_ref, tok_tile, i * tok_tile * TOP_K)

    @pl.when(i == n_p // tok_tile)
    def _():
        scatter_block(xs_ref, n_s, n_p * TOP_K)


def _scatter_rows(pos, last_tile, xf_p, xf_s, n_rows):
    n_p, w = xf_p.shape
    n_s = xf_s.shape[0]
    tok_tile = next(t for t in range(512, 0, -SUBLANES) if n_p % t == 0)
    n_p_steps = n_p // tok_tile
    kern = functools.partial(_scatter_kernel, tok_tile=tok_tile, n_p=n_p, n_s=n_s)
    grid_spec = pltpu.PrefetchScalarGridSpec(
        num_scalar_prefetch=2,
        grid=(n_p_steps + 1,),
        in_specs=[pl.BlockSpec((tok_tile, w), lambda i, p, q: (jnp.minimum(i, n_p_steps - 1), 0)),
                  pl.BlockSpec((n_s, w), lambda i, p, q: (0, 0))],
        out_specs=pl.BlockSpec(memory_space=pl.ANY),
        scratch_shapes=[pltpu.SemaphoreType.DMA],
    )
    return pl.pallas_call(
        kern,
        out_shape=jax.ShapeDtypeStruct((n_rows, w), xf_p.dtype),
        grid_spec=grid_spec,
        compiler_params=_cparams(("arbitrary",), 32),
        name="moe_scatter",
    )(pos, last_tile, xf_p, xf_s)


def _moe_kernel(e_ref, start_ref, tiles_ref, x_hbm, wu_ref, bu_ref, wd_ref, bd_ref, y_hbm,
                x_stage, x_bf, acc, wu_bf, wd_perm, wd_bf, sem_in, sem_out, *, n_ff, d_model, n_items):
    del e_ref
    it = pl.program_id(0)
    f = pl.program_id(1)
    n_tiles = tiles_ref[it]
    row0 = pl.multiple_of(start_ref[it] * MOE_TILE, MOE_TILE)
    half = LANES // 2

    def tile_rows(t):
        return pl.ds(pl.multiple_of(t * MOE_TILE, MOE_TILE), MOE_TILE)

    def x_copy(item, t):
        src0 = pl.multiple_of(start_ref[item] * MOE_TILE, MOE_TILE)
        return pltpu.make_async_copy(x_hbm.at[pl.ds(src0 + t * MOE_TILE, MOE_TILE), :],
                                     x_stage.at[tile_rows(t), :], sem_in)

    def y_copy(t):
        return pltpu.make_async_copy(acc.at[tile_rows(t), :],
                                     y_hbm.at[pl.ds(row0 + t * MOE_TILE, MOE_TILE), :], sem_out)

    def for_tiles(count, fn):
        def body(t, carry):
            fn(t)
            return carry
        lax.fori_loop(0, count, body, 0)

    @pl.when(n_tiles > 0)
    def _():
        @pl.when(f == 0)
        def _():
            @pl.when(it == 0)
            def _():
                for_tiles(n_tiles, lambda t: x_copy(it, t).start())

            for_tiles(n_tiles, lambda t: x_copy(it, t).wait())
            b_dn = jnp.broadcast_to(bd_ref[...], (MOE_TILE, d_model))

            def init(t):
                x_bf[tile_rows(t), :] = x_stage[tile_rows(t), :].astype(BF16)
                acc[tile_rows(t), :] = b_dn

            for_tiles(n_tiles, init)
            nxt = jnp.minimum(it + 1, n_items - 1)
            n_next = jnp.where(it + 1 < n_items, tiles_ref[nxt], 0)
            for_tiles(n_next, lambda t: x_copy(nxt, t).start())

        for c in range(MOE_FF_TILE // LANES):
            for s in range(d_model // LANES):
                for par in range(2):
                    src = wd_ref[c * LANES + par * half:c * LANES + (par + 1) * half,
                                 s * LANES:(s + 1) * LANES]
                    wd_perm[s, pl.ds(c * LANES + par, half, stride=2), :] = src
        for s in range(d_model // LANES):
            wd_bf[:, s * LANES:(s + 1) * LANES] = wd_perm[s].astype(BF16)

        b_up = bu_ref[...]

        def span_rows(t, n):
            return pl.ds(pl.multiple_of(t * MOE_TILE, MOE_TILE), n * MOE_TILE)

        def up_proj(t, n, wu):
            return jnp.dot(x_bf[span_rows(t, n), :], wu,
                           preferred_element_type=F32) + b_up

        def activation(hu):
            even = (lax.broadcasted_iota(I32, (hu.shape[0], LANES), 1) % 2) == 0
            acts = []
            for c in range(MOE_FF_TILE // LANES):
                a_blk = hu[:, (2 * c) * LANES:(2 * c + 1) * LANES]
                b_blk = hu[:, (2 * c + 1) * LANES:(2 * c + 2) * LANES]
                gate = jnp.where(even, a_blk, pltpu.roll(b_blk, 1, 1))
                up = jnp.where(even, pltpu.roll(a_blk, LANES - 1, 1), b_blk)
                gate = jnp.minimum(gate, SWIGLU_LIMIT)
                up = jnp.clip(up, -SWIGLU_LIMIT, SWIGLU_LIMIT)
                acts.append((up + 1.0) * gate * jax.nn.sigmoid(SWIGLU_ALPHA * gate))
            return jnp.concatenate(acts, axis=1).astype(BF16)

        def down_proj(t, n, act):
            acc[span_rows(t, n), :] += jnp.dot(act, wd_bf[...], preferred_element_type=F32)

        def chains(spans):
            wu = wu_ref[...].astype(BF16)
            hus = [up_proj(t, n, wu) for t, n in spans]
            acts = [activation(hu) for hu in hus]
            for (t, n), act in zip(spans, acts):
                down_proj(t, n, act)

        def run_tiles(after_tile):
            def quad(p, carry):
                chains([(4 * p, 2), (4 * p + 2, 2)])
                for u in range(4):
                    after_tile(4 * p + u)
                return carry

            n_quads = n_tiles // 4
            lax.fori_loop(0, n_quads, quad, 0)
            rest = n_tiles - 4 * n_quads

            @pl.when(rest >= 2)
            def _():
                chains([(4 * n_quads, 1), (4 * n_quads + 1, 1)])
                after_tile(4 * n_quads)
                after_tile(4 * n_quads + 1)

            @pl.when(rest % 2 == 1)
            def _():
                chains([(n_tiles - 1, 1)])
                after_tile(n_tiles - 1)

        @pl.when(f < n_ff - 1)
        def _():
            run_tiles(lambda t: None)

        @pl.when(f == n_ff - 1)
        def _():
            run_tiles(lambda t: y_copy(t).start())
            for_tiles(n_tiles, lambda t: y_copy(t).wait())


def _moe(item_e, item_start, item_tiles, x_sorted, w_up, b_up, w_down, b_down):
    n_rows = x_sorted.shape[0]
    n_exp, d, ff2_total = w_up.shape
    d_ff = ff2_total // 2
    n_ff = d_ff // MOE_FF_TILE
    n_items = item_e.shape[0]
    item_rows = MOE_ITEM_TILES * MOE_TILE
    kern = functools.partial(_moe_kernel, n_ff=n_ff, d_model=d, n_items=n_items)

    def ff_tile(i, f, n):
        return jnp.where(n[i] > 0, f, n_ff - 1)

    grid_spec = pltpu.PrefetchScalarGridSpec(
        num_scalar_prefetch=3,
        grid=(n_items, n_ff),
        in_specs=[pl.BlockSpec(memory_space=pl.ANY),
                  pl.BlockSpec((None, d, 2 * MOE_FF_TILE),
                               lambda i, f, e, s, n: (e[i], 0, ff_tile(i, f, n))),
                  pl.BlockSpec((None, 1, 2 * MOE_FF_TILE),
                               lambda i, f, e, s, n: (e[i], 0, ff_tile(i, f, n))),
                  pl.BlockSpec((None, MOE_FF_TILE, d),
                               lambda i, f, e, s, n: (e[i], ff_tile(i, f, n), 0)),
                  pl.BlockSpec((None, 1, d), lambda i, f, e, s, n: (e[i], 0, 0))],
        out_specs=pl.BlockSpec(memory_space=pl.ANY),
        scratch_shapes=[pltpu.VMEM((item_rows, d), F32),
                        pltpu.VMEM((item_rows, d), BF16),
                        pltpu.VMEM((item_rows, d), F32),
                        pltpu.VMEM((d, 2 * MOE_FF_TILE), BF16),
                        pltpu.VMEM((d // LANES, MOE_FF_TILE, LANES), F32),
                        pltpu.VMEM((MOE_FF_TILE, d), BF16),
                        pltpu.SemaphoreType.DMA,
                        pltpu.SemaphoreType.DMA],
    )
    return pl.pallas_call(
        kern,
        out_shape=jax.ShapeDtypeStruct((n_rows, d), F32),
        grid_spec=grid_spec,
        compiler_params=_cparams(("arbitrary", "arbitrary"), 56),
        name="moe_experts",
    )(item_e, item_start, item_tiles, x_sorted, w_up,
      b_up.reshape(n_exp, 1, ff2_total), w_down, b_down.reshape(n_exp, 1, d))


def _combine_kernel(pos_ref, y_hbm, h_ref, gate_ref, gt_ref, g_ref, o_ref, buf, sem, *, tok_off):
    i = pl.program_id(0)

    def issue(step, slot):
        base = (step * COMBINE_TOK + tok_off) * TOP_K

        def token(tok, carry):
            for k in range(TOP_K):
                p = pos_ref[base + tok * TOP_K + k]
                pltpu.make_async_copy(y_hbm.at[pl.ds(p, 1), :],
                                      buf.at[slot, pl.ds(k * COMBINE_TOK + tok, 1), :],
                                      sem.at[slot]).start()
            return carry

        lax.fori_loop(0, COMBINE_TOK, token, 0, unroll=2)

    @pl.when(i == 0)
    def _():
        issue(0, 0)

    @pl.when(i + 1 < pl.num_programs(0))
    def _():
        issue(i + 1, (i + 1) % 2)

    slot = i % 2
    pltpu.make_async_copy(y_hbm.at[pl.ds(0, COMBINE_TOK * TOP_K), :], buf.at[slot],
                          sem.at[slot]).wait()
    gates = gate_ref[...]
    ffn = jnp.zeros(h_ref.shape, F32)
    for k in range(TOP_K):
        ffn = ffn + gates[:, k:k + 1] * buf[slot, k * COMBINE_TOK:(k + 1) * COMBINE_TOK, :]
    y = h_ref[...] + gt_ref[...] * ffn
    o_ref[...] = y * lax.rsqrt(jnp.mean(y * y, axis=-1, keepdims=True) + EPS) * g_ref[...]


def _combine(pos, y_rows, h, gate, gate_f, norm_final, *, tok_off, rows_per_group):
    n_tok, d = h.shape
    kern = functools.partial(_combine_kernel, tok_off=tok_off)
    grid_spec = pltpu.PrefetchScalarGridSpec(
        num_scalar_prefetch=1,
        grid=(n_tok // COMBINE_TOK,),
        in_specs=[pl.BlockSpec(memory_space=pl.ANY),
                  pl.BlockSpec((COMBINE_TOK, d), lambda i, p: (i, 0)),
                  pl.BlockSpec((COMBINE_TOK, LANES), lambda i, p: (i, 0)),
                  _mod_spec(gate_f, COMBINE_TOK, rows_per_group),
                  pl.BlockSpec((1, d), lambda i, p: (0, 0))],
        out_specs=pl.BlockSpec((COMBINE_TOK, d), lambda i, p: (i, 0)),
        scratch_shapes=[pltpu.VMEM((2, COMBINE_TOK * TOP_K, d), F32),
                        pltpu.SemaphoreType.DMA((2,))],
    )
    return pl.pallas_call(
        kern,
        out_shape=jax.ShapeDtypeStruct((n_tok, d), F32),
        grid_spec=grid_spec,
        compiler_params=_cparams(("arbitrary",), 32),
        name="moe_combine",
    )(pos, y_rows, h, gate, gate_f, norm_final.reshape(1, d))


def _routing_tables(top_idx):
    n_tok = top_idx.shape[0]
    n_assign = n_tok * TOP_K
    flat_e = top_idx.reshape(-1)
    onehot = (flat_e[:, None] == jnp.arange(N_EXPERTS, dtype=I32)[None, :]).astype(I32)
    running = jnp.cumsum(onehot, axis=0)
    rank = jnp.take_along_axis(running, flat_e[:, None], axis=1)[:, 0] - 1
    counts = running[-1]
    tiles_e = (counts + MOE_TILE - 1) // MOE_TILE
    tile_start_e = jnp.cumsum(tiles_e) - tiles_e
    pos = tile_start_e[flat_e] * MOE_TILE + rank

    n_tiles_max = -(-n_assign // MOE_TILE) + N_EXPERTS
    n_rows = n_tiles_max * MOE_TILE
    last_tile = jnp.where(tiles_e > 0, tile_start_e + tiles_e - 1, -1)

    n_items_max = N_EXPERTS + n_tiles_max // MOE_ITEM_TILES
    items_e = (tiles_e + MOE_ITEM_TILES - 1) // MOE_ITEM_TILES
    items_end = jnp.cumsum(items_e)
    it = jnp.arange(n_items_max, dtype=I32)
    e_of = jnp.minimum(jnp.sum((it[:, None] >= items_end[None, :]).astype(I32), axis=1), N_EXPERTS - 1)
    local = it - (items_end[e_of] - items_e[e_of])
    valid = it < items_end[-1]
    tiles_left = tiles_e[e_of] - local * MOE_ITEM_TILES
    item_tiles = jnp.where(valid, jnp.clip(tiles_left, 0, MOE_ITEM_TILES), 0).astype(I32)
    item_start = jnp.where(valid, tile_start_e[e_of] + local * MOE_ITEM_TILES, 0).astype(I32)
    last_e = e_of[jnp.maximum(items_end[-1] - 1, 0)]
    item_e = jnp.where(valid, e_of, last_e).astype(I32)
    return pos.astype(I32), last_tile.astype(I32), n_rows, item_e, item_start, item_tiles


def _pick_tile(n, pref):
    t = min(n, pref)
    while n % t:
        t //= 2
    return t


def kernel(x_prompt, x_sample, c_prompt, c_sample, cache_k, cache_v, page_table, state_C, state_n, state_m, w_ada, b_ada, norm_mix, norm_ffn, w_in, b_gates, sb_bias, norm_sb_out, norm_ml_out, w_out, w_router, b_router, w_up, b_up, w_down, b_down, norm_final):
    bsz, seq, d = x_prompt.shape
    dbs, dec_seq, _ = x_sample.shape
    depth = w_ada.shape[0]
    assert depth == 1
    n_p, n_s = bsz * seq, dbs * dec_seq
    n_tok = n_p + n_s

    (w_ada, b_ada, norm_mix, norm_ffn, w_in, b_gates, sb_bias, norm_sb_out, norm_ml_out, w_out,
     w_router, b_router, w_up, b_up, w_down, b_down) = [
        a[0] for a in (w_ada, b_ada, norm_mix, norm_ffn, w_in, b_gates, sb_bias, norm_sb_out,
                       norm_ml_out, w_out, w_router, b_router, w_up, b_up, w_down, b_down)]

    n_c = bsz + dbs
    n_c_pad = -(-n_c // SUBLANES) * SUBLANES
    c_all = jnp.concatenate([c_prompt, c_sample, jnp.zeros((n_c_pad - n_c, d), F32)], axis=0)
    mod = _ada(c_all, w_ada, b_ada)
    mods_p = [mod[:bsz, i * d:(i + 1) * d].reshape(bsz, 1, d) for i in range(N_MOD)]
    mods_s = [jnp.repeat(mod[bsz:n_c, i * d:(i + 1) * d], dec_seq, axis=0).reshape(1, n_s, d)
              for i in range(N_MOD)]

    wg_pad = jnp.pad(w_in[:, PROJ_MAIN_W:], ((0, 0), (0, LANES - N_GATES)))
    bg_pad = jnp.pad(b_gates, (0, LANES - N_GATES)).reshape(1, LANES)
    wr_pad = jnp.pad(w_router, ((0, 0), (0, LANES - N_EXPERTS)))
    br_pad = jnp.pad(b_router, (0, LANES - N_EXPERTS)).reshape(1, LANES)

    xp2 = x_prompt.reshape(n_p, d)
    xs2 = x_sample.reshape(n_s, d)
    tm_p = _pick_tile(seq, 1024)

    proj_p, gates_p, k_p, v_p = _inproj(xp2, mods_p[0], mods_p[1], norm_mix, w_in, wg_pad, bg_pad,
                                        tm=tm_p, rows_per_group=seq)
    osb_p = _sb_prompt(proj_p, sb_bias, norm_sb_out, bsz=bsz, seq=seq)
    chunk_p = _pick_tile(seq, ML_CHUNK)
    hml_p, c_p, nn_p, m_p = _mlstm(
        proj_p, gates_p,
        jnp.zeros((bsz, H_ML, DV_ML, DQK_ML), F32), jnp.zeros((bsz, H_ML, DQK_ML), F32),
        jnp.zeros((bsz, H_ML), F32), norm_ml_out,
        bsz=bsz, n_chunks=seq // chunk_p, chunk=chunk_p, valid=chunk_p)

    proj_s, gates_s, k_s, v_s = _inproj(xs2, mods_s[0], mods_s[1], norm_mix, w_in, wg_pad, bg_pad,
                                        tm=n_s, rows_per_group=n_s)
    osb_s = _sb_paged(proj_s[:, :SB_W].reshape(dbs, dec_seq, SB_W),
                      k_s.reshape(dbs, dec_seq, SB_W), v_s.reshape(dbs, dec_seq, SB_W),
                      cache_k[0], cache_v[0], page_table, sb_bias, norm_sb_out)
    osb_s = (osb_s.reshape(dbs, H_SB, dec_seq, D_SB).swapaxes(1, 2)
             .reshape(n_s, SB_W).astype(BF16))
    chunk_s = -(-dec_seq // BF16_SUBLANES) * BF16_SUBLANES
    pad_rows = lambda a: jnp.pad(a.reshape(dbs, dec_seq, a.shape[-1]),
                                 ((0, 0), (0, chunk_s - dec_seq), (0, 0))).reshape(dbs * chunk_s, a.shape[-1])
    hml_s, c_s, nn_s, m_s = _mlstm(
        pad_rows(proj_s), pad_rows(gates_s), state_C[0], state_n[0], state_m[0], norm_ml_out,
        bsz=dbs, n_chunks=1, chunk=chunk_s, valid=dec_seq)
    hml_s = hml_s.reshape(dbs, chunk_s, ML_V_W)[:, :dec_seq].reshape(n_s, ML_V_W)

    tm_o = _pick_tile(seq, 512)
    h_p, xf_p, idx_p, gate_p = _outproj(
        osb_p, hml_p, w_out, xp2, mods_p[2], mods_p[3], mods_p[4], norm_ffn, wr_pad, br_pad,
        tm=tm_o, rows_per_group=seq)
    h_s, xf_s, idx_s, gate_s = _outproj(
        osb_s, hml_s, w_out, xs2, mods_s[2], mods_s[3], mods_s[4], norm_ffn, wr_pad, br_pad,
        tm=n_s, rows_per_group=n_s)
    top_idx = jnp.concatenate([idx_p[:, :TOP_K], idx_s[:, :TOP_K]], axis=0)

    pos, last_tile, n_rows, item_e, item_start, item_tiles = _routing_tables(top_idx)
    x_sorted = _scatter_rows(pos, last_tile, xf_p, xf_s, n_rows)
    y_rows = _moe(item_e, item_start, item_tiles, x_sorted, w_up, b_up, w_down, b_down)
    y_p = _combine(pos, y_rows, h_p, gate_p, mods_p[5], norm_final, tok_off=0, rows_per_group=seq)
    y_s = _combine(pos, y_rows, h_s, gate_s, mods_s[5], norm_final, tok_off=n_p, rows_per_group=n_s)

    return (y_p.reshape(bsz, seq, d), y_s.reshape(dbs, dec_seq, d),
            k_p.reshape(1, bsz, seq, H_SB, D_SB), v_p.reshape(1, bsz, seq, H_SB, D_SB),
            k_s.reshape(1, dbs, dec_seq, H_SB, D_SB), v_s.reshape(1, dbs, dec_seq, H_SB, D_SB),
            c_p[None], nn_p[None], m_p[None], c_s[None], nn_s[None], m_s[None])
```

```python
import functools

import jax
import jax.numpy as jnp
from jax import lax
from jax.experimental import pallas as pl
from jax.experimental.pallas import tpu as pltpu

F32 = jnp.float32
BF16 = jnp.bfloat16
I32 = jnp.int32

H_SB = 8
D_SB = 128
SB_W = H_SB * D_SB
H_ML = 4
DQK_ML = 128
DV_ML = 256
ML_QK_W = H_ML * DQK_ML
ML_V_W = H_ML * DV_ML
PROJ_MAIN_W = 3 * SB_W + 2 * ML_QK_W + 2 * ML_V_W
N_GATES = 2 * H_ML
N_EXPERTS = 32
TOP_K = 4
N_MOD = 6
SWIGLU_LIMIT = 7.0
SWIGLU_ALPHA = 1.702
EPS = 1e-6

LANES = 128
SUBLANES = 8
BF16_SUBLANES = 16
MIB = 1024 * 1024
GATE_LANE0 = LANES - N_GATES

MOE_TILE = 128
MOE_ITEM_TILES = 12
MOE_FF_TILE = 256
ML_CHUNK = 128
SB_BLOCK = 512
SB_SUB = 256
PAGES_PER_STEP = 16
COMBINE_TOK = 64

NT_DIMS = (((1,), (1,)), ((), ()))
TN_DIMS = (((0,), (0,)), ((), ()))


def _cparams(sem, vmem_mib):
    return pltpu.CompilerParams(dimension_semantics=sem, vmem_limit_bytes=vmem_mib * MIB)


def _log_sigmoid_pair(z):
    t = jnp.log(1.0 + jnp.exp(-jnp.abs(z)))
    return jnp.minimum(z, 0.0) - t, -jnp.maximum(z, 0.0) - t


def _split_bf16(x):
    hi = x.astype(BF16)
    lo = (x - hi.astype(F32)).astype(BF16)
    return hi, lo


def _dot3(x, w):
    xh, xl = _split_bf16(x)
    wh, wl = _split_bf16(w)
    return (jnp.dot(xh, wh, preferred_element_type=F32)
            + (jnp.dot(xl, wh, preferred_element_type=F32)
               + jnp.dot(xh, wl, preferred_element_type=F32)))


def _mod_spec(arr, tile, rows_per_group, col_tile=None):
    per_group = arr.shape[1] == 1
    width = arr.shape[2] if col_tile is None else col_tile

    def index_map(i, *rest):
        col = 0 if col_tile is None else rest[0]
        if per_group:
            return ((i * tile) // rows_per_group, 0, col)
        return (0, i, col)

    return pl.BlockSpec((None, 1 if per_group else tile, width), index_map)


def _ada_kernel(c_ref, w_ref, b_ref, o_ref):
    c = c_ref[...]
    o_ref[...] = _dot3(c * jax.nn.sigmoid(c), w_ref[...]) + b_ref[...]


def _ada(c_all, w_ada, b_ada):
    n, d = c_all.shape
    w_out = w_ada.shape[1]
    tn = 1024
    return pl.pallas_call(
        _ada_kernel,
        out_shape=jax.ShapeDtypeStruct((n, w_out), F32),
        grid=(w_out // tn,),
        in_specs=[pl.BlockSpec((n, d), lambda j: (0, 0)),
                  pl.BlockSpec((d, tn), lambda j: (0, j)),
                  pl.BlockSpec((1, tn), lambda j: (0, j))],
        out_specs=pl.BlockSpec((n, tn), lambda j: (0, j)),
        compiler_params=_cparams(("arbitrary",), 40),
        name="ada",
    )(c_all, w_ada, b_ada.reshape(1, w_out))


def _inproj_kernel(x_ref, sh_ref, sc_ref, g_ref, w_ref, wg_ref, bg_ref,
                   proj_ref, gates_ref, k_ref, v_ref, xm_scr, *, tn):
    j = pl.program_id(1)

    @pl.when(j == 0)
    def _():
        x = x_ref[...]
        xn = x * lax.rsqrt(jnp.mean(x * x, axis=-1, keepdims=True) + EPS) * g_ref[...]
        xm = (xn * (1.0 + sc_ref[...]) + sh_ref[...]).astype(BF16)
        xm_scr[...] = xm
        gates_ref[...] = jnp.dot(xm, wg_ref[...].astype(BF16),
                                 preferred_element_type=F32) + bg_ref[...]

    res = jnp.dot(xm_scr[...], w_ref[...].astype(BF16), preferred_element_type=F32)
    proj_ref[...] = res
    k_tiles = SB_W // tn

    @pl.when((j >= k_tiles) & (j < 2 * k_tiles))
    def _():
        k_ref[...] = res

    @pl.when((j >= 2 * k_tiles) & (j < 3 * k_tiles))
    def _():
        v_ref[...] = res


def _inproj(x2d, shift, scale, norm_w, w_in, wg_pad, bg_pad, *, tm, rows_per_group):
    t, d = x2d.shape
    tn = 512
    k_tiles = SB_W // tn
    mod_spec = _mod_spec(shift, tm, rows_per_group)

    def kv_spec(first):
        return pl.BlockSpec((tm, tn), lambda i, j: (i, jnp.clip(j - first, 0, k_tiles - 1)))

    return pl.pallas_call(
        functools.partial(_inproj_kernel, tn=tn),
        out_shape=(jax.ShapeDtypeStruct((t, PROJ_MAIN_W), F32),
                   jax.ShapeDtypeStruct((t, LANES), F32),
                   jax.ShapeDtypeStruct((t, SB_W), F32),
                   jax.ShapeDtypeStruct((t, SB_W), F32)),
        grid=(t // tm, PROJ_MAIN_W // tn),
        in_specs=[pl.BlockSpec((tm, d), lambda i, j: (i, 0)),
                  mod_spec, mod_spec,
                  pl.BlockSpec((1, d), lambda i, j: (0, 0)),
                  pl.BlockSpec((d, tn), lambda i, j: (0, j)),
                  pl.BlockSpec((d, LANES), lambda i, j: (0, 0)),
                  pl.BlockSpec((1, LANES), lambda i, j: (0, 0))],
        out_specs=(pl.BlockSpec((tm, tn), lambda i, j: (i, j)),
                   pl.BlockSpec((tm, LANES), lambda i, j: (i, 0)),
                   kv_spec(k_tiles), kv_spec(2 * k_tiles)),
        scratch_shapes=[pltpu.VMEM((tm, d), BF16)],
        compiler_params=_cparams(("arbitrary", "arbitrary"), 48),
        name="inproj",
    )(x2d, shift, scale, norm_w.reshape(1, d), w_in, wg_pad, bg_pad)


def _sb_strip(z, mask, from_uu, sub, v_bf, spent):
    pos = jnp.maximum(z, 0.0)
    cost = pos + jnp.log(1.0 + jnp.exp(jnp.minimum(z, 0.0) - pos))
    if mask is not None:
        cost = jnp.where(mask, cost, 0.0)
    hi, lo = _split_bf16(cost)
    n_sub = z.shape[1] // sub
    parts = [None] * n_sub
    for j in reversed(range(n_sub)):
        cols = slice(j * sub, (j + 1) * sub)
        from_s = jnp.dot(jnp.concatenate([hi[:, cols], lo[:, cols]], axis=1), from_uu,
                         preferred_element_type=F32)
        parts[j] = jnp.exp(z[:, cols] - (from_s + spent))
        spent = spent + from_s[:, 0:1]
    a = parts[0] if n_sub == 1 else jnp.concatenate(parts, axis=1)
    if mask is not None:
        a = jnp.where(mask, a, 0.0)
    return jnp.dot(a.astype(BF16), v_bf, preferred_element_type=F32), spent


def _from_matrix2(n):
    j = lax.broadcasted_iota(I32, (2 * n, n), 0) % n
    s = lax.broadcasted_iota(I32, (2 * n, n), 1)
    return jnp.where(j >= s, 1.0, 0.0).astype(BF16)


def _sbp_kernel(bias_ref, q_ref, k_ref, v_ref, g_ref, o_ref, o_scr, carry_scr, *, blk, sub, scale):
    h = pl.program_id(1)
    qi = pl.program_id(2)
    bias = bias_ref[h]
    q = q_ref[...].astype(BF16)
    from_uu = _from_matrix2(sub)

    def strip(k_start, width, diagonal):
        k_bf = k_ref[pl.ds(k_start, width), :].astype(BF16)
        v_bf = v_ref[pl.ds(k_start, width), :].astype(BF16)
        z = lax.dot_general(q, k_bf, NT_DIMS, preferred_element_type=F32) * scale + bias
        mask = None
        if diagonal:
            t = lax.broadcasted_iota(I32, (blk, width), 0)
            s = lax.broadcasted_iota(I32, (blk, width), 1)
            mask = s < t
        o_blk, spent = _sb_strip(z, mask, from_uu, sub, v_bf, carry_scr[:, 0:1])
        o_scr[...] += o_blk
        carry_scr[...] = jnp.broadcast_to(spent, carry_scr.shape)

    o_scr[...] = jnp.zeros_like(o_scr)
    carry_scr[...] = jnp.zeros_like(carry_scr)
    strip(pl.multiple_of(qi * blk, blk), blk, True)

    @pl.when(qi % 2 == 1)
    def _():
        strip(pl.multiple_of((qi - 1) * blk, blk), blk, False)

    def older(n, carry):
        strip(pl.multiple_of((qi // 2 - 1 - n) * (2 * blk), 2 * blk), 2 * blk, False)
        return carry

    lax.fori_loop(0, qi // 2, older, 0)
    o = o_scr[...]
    o = o * lax.rsqrt(jnp.mean(o * o, axis=-1, keepdims=True) + EPS) * g_ref[...]
    o_ref[...] = o.astype(o_ref.dtype)


def _sb_prompt(proj, sb_bias, norm_sb_out, *, bsz, seq):
    blk = min(SB_BLOCK, seq)
    nq = seq // blk
    kern = functools.partial(_sbp_kernel, blk=blk, sub=min(SB_SUB, blk), scale=D_SB ** -0.5)
    return pl.pallas_call(
        kern,
        out_shape=jax.ShapeDtypeStruct((bsz * seq, SB_W), BF16),
        grid=(bsz, H_SB, nq),
        in_specs=[pl.BlockSpec(memory_space=pltpu.SMEM),
                  pl.BlockSpec((blk, D_SB), lambda b, h, i: (b * nq + i, h)),
                  pl.BlockSpec((seq, D_SB), lambda b, h, i: (b, H_SB + h)),
                  pl.BlockSpec((seq, D_SB), lambda b, h, i: (b, 2 * H_SB + h)),
                  pl.BlockSpec((None, 1, D_SB), lambda b, h, i: (h, 0, 0))],
        out_specs=pl.BlockSpec((blk, D_SB), lambda b, h, i: (b * nq + i, h)),
        scratch_shapes=[pltpu.VMEM((blk, D_SB), F32), pltpu.VMEM((blk, LANES), F32)],
        compiler_params=_cparams(("arbitrary", "arbitrary", "arbitrary"), 48),
        name="sb_prompt",
    )(sb_bias, proj, proj, proj, norm_sb_out.reshape(H_SB, 1, D_SB))


def _head_block_mask(rows, dec_seq):
    r = lax.broadcasted_iota(I32, (rows, SB_W), 0)
    c = lax.broadcasted_iota(I32, (rows, SB_W), 1)
    return (r // dec_seq) == (c // D_SB)


def _page_heads(page_ref):
    n_keys = page_ref.shape[0] // H_SB
    cols = [page_ref[pl.ds(h, n_keys, stride=H_SB), :] for h in range(H_SB)]
    return jnp.concatenate(cols, axis=1).astype(BF16)


def _sbd_kernel(pt_ref, qbd_ref, kn_ref, vn_ref, bias_ref, g_ref, *rest,
                dec_seq, page, n_steps, scale):
    del pt_ref
    pages = rest[:2 * PAGES_PER_STEP]
    o_ref = rest[2 * PAGES_PER_STEP]
    o_scr, carry_scr, new_scr = rest[2 * PAGES_PER_STEP + 1:]
    rows = H_SB * dec_seq
    j = pl.program_id(1)
    from_uu = _from_matrix2(page)

    def sweep(k_bf, v_bf, mask, spent):
        width = k_bf.shape[0]
        z = (lax.dot_general(qbd_ref[...], k_bf, NT_DIMS, preferred_element_type=F32) * scale
             + bias_ref[:, 0:width])
        return _sb_strip(z, mask, from_uu, page, v_bf, spent)

    @pl.when(j == 0)
    def _():
        new_scr[...] = jnp.zeros_like(new_scr)
        new_scr[0, 0:dec_seq, :] = kn_ref[...]
        new_scr[1, 0:dec_seq, :] = vn_ref[...]
        r = lax.broadcasted_iota(I32, (rows, page), 0)
        s = lax.broadcasted_iota(I32, (rows, page), 1)
        o_new, spent = sweep(new_scr[0].astype(BF16), new_scr[1].astype(BF16), s < (r % dec_seq),
                             jnp.zeros((rows, 1), F32))
        o_scr[...] = o_new
        carry_scr[...] = jnp.broadcast_to(spent, carry_scr.shape)

    order = range(PAGES_PER_STEP - 1, -1, -1)
    k_bf = jnp.concatenate([_page_heads(pages[i]) for i in order], axis=0)
    v_bf = jnp.concatenate([_page_heads(pages[PAGES_PER_STEP + i]) for i in order], axis=0)
    o_blk, spent = sweep(k_bf, v_bf, None, carry_scr[:, 0:1])
    o_scr[...] += o_blk
    carry_scr[...] = jnp.broadcast_to(spent, carry_scr.shape)

    @pl.when(j == n_steps - 1)
    def _():
        o_full = jnp.where(_head_block_mask(rows, dec_seq), o_scr[...], 0.0)
        o = o_full[:, 0:D_SB]
        for h in range(1, H_SB):
            o = o + o_full[:, h * D_SB:(h + 1) * D_SB]
        o = o * lax.rsqrt(jnp.mean(o * o, axis=-1, keepdims=True) + EPS) * g_ref[...]
        o_ref[...] = o


def _sb_paged(q_s, k_new, v_new, cache_k, cache_v, page_table, sb_bias, norm_sb_out):
    bsz, dec_seq, _ = q_s.shape
    n_pool, page = cache_k.shape[0], cache_k.shape[1]
    n_pages = page_table.shape[1]
    assert n_pages % PAGES_PER_STEP == 0
    n_steps = n_pages // PAGES_PER_STEP
    rows = H_SB * dec_seq
    ck = cache_k.reshape(n_pool, page * H_SB, D_SB)
    cv = cache_v.reshape(n_pool, page * H_SB, D_SB)
    strip_w = PAGES_PER_STEP * page
    bias_rows = jnp.broadcast_to(jnp.repeat(sb_bias, dec_seq)[:, None], (rows, strip_w)).astype(F32)
    g_rows = jnp.repeat(norm_sb_out.reshape(H_SB, D_SB), dec_seq, axis=0)
    q_heads = q_s.reshape(bsz, dec_seq, H_SB, D_SB).swapaxes(1, 2)
    eye = jnp.eye(H_SB, dtype=F32)
    qbd = (q_heads[:, :, :, None, :] * eye[None, :, None, :, None]).reshape(bsz, rows, SB_W).astype(BF16)

    def page_spec(i):
        return pl.BlockSpec(
            (None, page * H_SB, D_SB),
            lambda b, j, pt: (pt[b, n_pages - 1 - (j * PAGES_PER_STEP + i)], 0, 0))

    tok_spec = pl.BlockSpec((None, dec_seq, SB_W), lambda b, j, pt: (b, 0, 0))
    kern = functools.partial(_sbd_kernel, dec_seq=dec_seq, page=page, n_steps=n_steps,
                             scale=D_SB ** -0.5)
    grid_spec = pltpu.PrefetchScalarGridSpec(
        num_scalar_prefetch=1,
        grid=(bsz, n_steps),
        in_specs=[pl.BlockSpec((None, rows, SB_W), lambda b, j, pt: (b, 0, 0)),
                  tok_spec, tok_spec,
                  pl.BlockSpec((rows, strip_w), lambda b, j, pt: (0, 0)),
                  pl.BlockSpec((rows, D_SB), lambda b, j, pt: (0, 0))]
                 + [page_spec(i) for i in range(PAGES_PER_STEP)] * 2,
        out_specs=pl.BlockSpec((None, rows, D_SB), lambda b, j, pt: (b, 0, 0)),
        scratch_shapes=[pltpu.VMEM((rows, SB_W), F32),
                        pltpu.VMEM((rows, LANES), F32),
                        pltpu.VMEM((2, page, SB_W), F32)],
    )
    return pl.pallas_call(
        kern,
        out_shape=jax.ShapeDtypeStruct((bsz, rows, D_SB), F32),
        grid_spec=grid_spec,
        compiler_params=_cparams(("arbitrary", "arbitrary"), 56),
        name="sb_paged",
    )(page_table, qbd, k_new, v_new, bias_rows, g_rows,
      *([ck] * PAGES_PER_STEP), *([cv] * PAGES_PER_STEP))


def _mlstm_kernel(q_ref, k_ref, v_ref, og_ref, gcol_ref, grow_ref, c0_ref, n0_ref, m0_ref, gn_ref,
                  h_ref, c_out, n_out, m_out, c_scr, n_scr, m_scr, *, chunk, valid, n_chunks):
    ci = pl.program_id(1)

    @pl.when(ci == 0)
    def _():
        c_scr[...] = c0_ref[...]
        n_scr[...] = n0_ref[...]
        m_scr[...] = m0_ref[...]

    t_idx = lax.broadcasted_iota(I32, (chunk, chunk), 0)
    s_idx = lax.broadcasted_iota(I32, (chunk, chunk), 1)
    causal = s_idx <= t_idx
    col_valid = lax.broadcasted_iota(I32, (chunk, 1), 0) < valid
    row_valid = lax.broadcasted_iota(I32, (1, chunk), 1) < valid
    gcol = gcol_ref[...]
    grow = grow_ref[...]
    neg_inf = -jnp.inf

    def gating(h):
        q = q_ref[:, h * DQK_ML:(h + 1) * DQK_ML]
        ks = k_ref[:, h * DQK_ML:(h + 1) * DQK_ML] * (DQK_ML ** -0.5)
        q_bf, ks_bf = q.astype(BF16), ks.astype(BF16)
        qk = lax.dot_general(q_bf, ks_bf, NT_DIMS, preferred_element_type=F32)

        gi, gf = GATE_LANE0 + h, GATE_LANE0 + H_ML + h
        i_col = jnp.where(col_valid, gcol[:, gi:gi + 1], neg_inf)
        i_row = jnp.where(row_valid, grow[h:h + 1, :], neg_inf)
        lf_col = jnp.where(col_valid, _log_sigmoid_pair(gcol[:, gf:gf + 1])[0], 0.0)
        lf_row = jnp.where(row_valid, _log_sigmoid_pair(grow[H_ML + h:H_ML + h + 1, :])[0], 0.0)
        bcum_col = jnp.sum(jnp.where(causal, lf_row, 0.0), axis=1, keepdims=True)
        bcum_row = jnp.sum(jnp.where(t_idx <= s_idx, lf_col, 0.0), axis=0, keepdims=True)

        log_w = jnp.where(causal, bcum_col - bcum_row + i_row, neg_inf)
        m0 = m_scr[h:h + 1, 0:1]
        m_state = bcum_col + m0
        m_new = jnp.maximum(m_state, jnp.max(log_w, axis=1, keepdims=True))
        s_mat = jnp.exp(log_w - m_new) * qk
        g = jnp.exp(m_state - m_new)
        m_end = m_new[chunk - 1:chunk, :]
        b_last = bcum_col[chunk - 1:chunk, :]
        w_end = jnp.exp(b_last - bcum_col + i_col - m_end)
        g_end = jnp.exp(b_last + m0 - m_end)
        return dict(q=q, ks=ks, q_bf=q_bf, ks_bf=ks_bf, s_mat=s_mat, g=g, m_new=m_new,
                    m_end=m_end, w_end=w_end, g_end=g_end)

    def readout(h, t):
        v = v_ref[:, h * DV_ML:(h + 1) * DV_ML]
        c0 = c_scr[h]
        n0 = n_scr[h:h + 1, :]
        num = (jnp.dot(t["s_mat"].astype(BF16), v.astype(BF16), preferred_element_type=F32)
               + t["g"] * lax.dot_general(t["q_bf"], c0.astype(BF16), NT_DIMS,
                                          preferred_element_type=F32))
        den = (jnp.sum(t["s_mat"], axis=1, keepdims=True)
               + t["g"] * jnp.sum(t["q"] * n0, axis=1, keepdims=True))
        return num / jnp.maximum(jnp.abs(den), jnp.exp(-t["m_new"]))

    def update_state(h, t):
        v = v_ref[:, h * DV_ML:(h + 1) * DV_ML]
        vw_bf = (v * t["w_end"]).astype(BF16)
        c_scr[h] = t["g_end"] * c_scr[h] + lax.dot_general(vw_bf, t["ks_bf"], TN_DIMS,
                                                           preferred_element_type=F32)
        n_scr[h:h + 1, :] = (t["g_end"] * n_scr[h:h + 1, :]
                             + jnp.sum(t["ks"] * t["w_end"], axis=0, keepdims=True))
        m_scr[h:h + 1, :] = jnp.broadcast_to(t["m_end"], (1, LANES))

    def emit(h, hh):
        hn = (hh * lax.rsqrt(jnp.mean(hh * hh, axis=-1, keepdims=True) + EPS)
              * gn_ref[:, h * DV_ML:(h + 1) * DV_ML])
        out = hn * jax.nn.sigmoid(og_ref[:, h * DV_ML:(h + 1) * DV_ML])
        h_ref[:, h * DV_ML:(h + 1) * DV_ML] = out.astype(h_ref.dtype)

    terms = [gating(h) for h in range(H_ML)]
    hhs = [readout(h, terms[h]) for h in range(H_ML)]
    for h in range(H_ML):
        update_state(h, terms[h])
    for h in range(H_ML):
        emit(h, hhs[h])

    @pl.when(ci == n_chunks - 1)
    def _():
        c_out[...] = c_scr[...]
        n_out[...] = n_scr[...]
        m_out[...] = m_scr[...]


def _mlstm(proj, gates, c0, n0, m0, norm_ml_out, *, bsz, n_chunks, chunk, valid):
    rows = bsz * n_chunks * chunk
    grow = gates[:, GATE_LANE0:].reshape(bsz, n_chunks, chunk, N_GATES).swapaxes(2, 3)
    m0_b = jnp.broadcast_to(m0[:, :, None], (bsz, H_ML, LANES)).astype(F32)
    qk_blk = ML_QK_W // LANES
    row_map = lambda b, c: (b * n_chunks + c)
    kern = functools.partial(_mlstm_kernel, chunk=chunk, valid=valid, n_chunks=n_chunks)
    state_specs = [pl.BlockSpec((None, H_ML, DV_ML, DQK_ML), lambda b, c: (b, 0, 0, 0)),
                   pl.BlockSpec((None, H_ML, DQK_ML), lambda b, c: (b, 0, 0)),
                   pl.BlockSpec((None, H_ML, LANES), lambda b, c: (b, 0, 0))]
    q_col = (3 * SB_W) // ML_QK_W
    v_col = (3 * SB_W + 2 * ML_QK_W) // ML_V_W
    h, c1, n1, m1 = pl.pallas_call(
        kern,
        out_shape=(jax.ShapeDtypeStruct((rows, ML_V_W), BF16),
                   jax.ShapeDtypeStruct((bsz, H_ML, DV_ML, DQK_ML), F32),
                   jax.ShapeDtypeStruct((bsz, H_ML, DQK_ML), F32),
                   jax.ShapeDtypeStruct((bsz, H_ML, LANES), F32)),
        grid=(bsz, n_chunks),
        in_specs=[pl.BlockSpec((chunk, ML_QK_W), lambda b, c: (row_map(b, c), q_col)),
                  pl.BlockSpec((chunk, ML_QK_W), lambda b, c: (row_map(b, c), q_col + 1)),
                  pl.BlockSpec((chunk, ML_V_W), lambda b, c: (row_map(b, c), v_col)),
                  pl.BlockSpec((chunk, ML_V_W), lambda b, c: (row_map(b, c), v_col + 1)),
                  pl.BlockSpec((chunk, LANES), lambda b, c: (row_map(b, c), 0)),
                  pl.BlockSpec((None, None, N_GATES, chunk), lambda b, c: (b, c, 0, 0))]
                 + state_specs
                 + [pl.BlockSpec((1, ML_V_W), lambda b, c: (0, 0))],
        out_specs=(pl.BlockSpec((chunk, ML_V_W), lambda b, c: (row_map(b, c), 0)),) + tuple(state_specs),
        scratch_shapes=[pltpu.VMEM((H_ML, DV_ML, DQK_ML), F32),
                        pltpu.VMEM((H_ML, DQK_ML), F32),
                        pltpu.VMEM((H_ML, LANES), F32)],
        compiler_params=_cparams(("arbitrary", "arbitrary"), 40),
        name="mlstm",
    )(proj, proj, proj, proj, gates, grow, c0, n0, m0_b, norm_ml_out.reshape(1, ML_V_W))
    del qk_blk
    return h, c1, n1, m1[:, :, 0]


def _outproj_kernel(osb_ref, hml_ref, w_ref, x_ref, gt_ref, sh_ref, sc_ref, g_ref, wr_ref, br_ref,
                    h_ref, xf_ref, idx_ref, gate_ref, cat_scr, h_scr, *, tn, n_col):
    j = pl.program_id(1)

    @pl.when(j == 0)
    def _():
        cat_scr[:, 0:SB_W] = osb_ref[...]
        cat_scr[:, SB_W:SB_W + ML_V_W] = hml_ref[...]

    mix = jnp.dot(cat_scr[...], w_ref[...].astype(BF16), preferred_element_type=F32)
    h_blk = x_ref[...] + gt_ref[...] * mix
    h_ref[...] = h_blk
    for jj in range(n_col):
        @pl.when(j == jj)
        def _(jj=jj):
            h_scr[:, jj * tn:(jj + 1) * tn] = h_blk

    @pl.when(j == n_col - 1)
    def _():
        hf = h_scr[...]
        xn = hf * lax.rsqrt(jnp.mean(hf * hf, axis=-1, keepdims=True) + EPS) * g_ref[...]
        xf = xn * (1.0 + sc_ref[...]) + sh_ref[...]
        xf_ref[...] = xf
        logits = _dot3(xf, wr_ref[...]) + br_ref[...]
        lane = lax.broadcasted_iota(I32, logits.shape, 1)
        lane_f = lane.astype(F32)
        neg_inf = -jnp.inf
        lg = jnp.where(lane < N_EXPERTS, logits, neg_inf)
        vals, idxs = [], []
        for _k in range(TOP_K):
            mx = jnp.max(lg, axis=1, keepdims=True)
            ix = jnp.min(jnp.where(lg == mx, lane_f, float(LANES)), axis=1, keepdims=True)
            vals.append(mx)
            idxs.append(ix)
            lg = jnp.where(lane_f == ix, neg_inf, lg)
        es = [jnp.exp(vk - vals[0]) for vk in vals]
        tot = es[0] + es[1] + es[2] + es[3]
        idx_out = jnp.zeros(logits.shape, F32)
        gate_out = jnp.zeros(logits.shape, F32)
        for k in range(TOP_K):
            idx_out = jnp.where(lane == k, idxs[k], idx_out)
            gate_out = jnp.where(lane == k, es[k] / tot, gate_out)
        idx_ref[...] = idx_out.astype(I32)
        gate_ref[...] = gate_out


def _outproj(o_sb, h_ml, w_out, x2d, gate_a, shift_f, scale_f, norm_ffn, wr_pad, br_pad,
             *, tm, rows_per_group):
    t, d = x2d.shape
    tn = 512
    n_col = d // tn
    kern = functools.partial(_outproj_kernel, tn=tn, n_col=n_col)
    return pl.pallas_call(
        kern,
        out_shape=(jax.ShapeDtypeStruct((t, d), F32),
                   jax.ShapeDtypeStruct((t, d), F32),
                   jax.ShapeDtypeStruct((t, LANES), I32),
                   jax.ShapeDtypeStruct((t, LANES), F32)),
        grid=(t // tm, n_col),
        in_specs=[pl.BlockSpec((tm, SB_W), lambda i, j: (i, 0)),
                  pl.BlockSpec((tm, ML_V_W), lambda i, j: (i, 0)),
                  pl.BlockSpec((SB_W + ML_V_W, tn), lambda i, j: (0, j)),
                  pl.BlockSpec((tm, tn), lambda i, j: (i, j)),
                  _mod_spec(gate_a, tm, rows_per_group, col_tile=tn),
                  _mod_spec(shift_f, tm, rows_per_group),
                  _mod_spec(scale_f, tm, rows_per_group),
                  pl.BlockSpec((1, d), lambda i, j: (0, 0)),
                  pl.BlockSpec((d, LANES), lambda i, j: (0, 0)),
                  pl.BlockSpec((1, LANES), lambda i, j: (0, 0))],
        out_specs=(pl.BlockSpec((tm, tn), lambda i, j: (i, j)),
                   pl.BlockSpec((tm, d), lambda i, j: (i, 0)),
                   pl.BlockSpec((tm, LANES), lambda i, j: (i, 0)),
                   pl.BlockSpec((tm, LANES), lambda i, j: (i, 0))),
        scratch_shapes=[pltpu.VMEM((tm, SB_W + ML_V_W), BF16),
                        pltpu.VMEM((tm, d), F32)],
        compiler_params=_cparams(("arbitrary", "arbitrary"), 48),
        name="outproj",
    )(o_sb, h_ml, w_out, x2d, gate_a, shift_f, scale_f, norm_ffn.reshape(1, d), wr_pad, br_pad)


def _scatter_kernel(pos_ref, last_ref, xp_ref, xs_ref, o_hbm, sem, *, tok_tile, n_p, n_s):
    i = pl.program_id(0)
    fill_rows = min(MOE_TILE, tok_tile)

    @pl.when(i == 0)
    def _():
        def fill_copy(e, part):
            row = last_ref[e] * MOE_TILE + part * fill_rows
            return pltpu.make_async_copy(xp_ref.at[pl.ds(0, fill_rows), :],
                                         o_hbm.at[pl.ds(row, fill_rows), :], sem)

        def fill(e, carry):
            @pl.when(last_ref[e] >= 0)
            def _():
                for part in range(MOE_TILE // fill_rows):
                    fill_copy(e, part).start()
            return carry

        def drain(e, carry):
            @pl.when(last_ref[e] >= 0)
            def _():
                for part in range(MOE_TILE // fill_rows):
                    fill_copy(e, part).wait()
            return carry

        lax.fori_loop(0, N_EXPERTS, fill, 0)
        lax.fori_loop(0, N_EXPERTS, drain, 0)

    def scatter_block(x_ref, n_rows, base):
        def token(tok, carry):
            for k in range(TOP_K):
                p = pos_ref[base + tok * TOP_K + k]
                pltpu.make_async_copy(x_ref.at[pl.ds(tok, 1), :], o_hbm.at[pl.ds(p, 1), :],
                                      sem).start()
            return carry

        lax.fori_loop(0, n_rows, token, 0, unroll=2)
        done = o_hbm.at[pl.ds(0, n_rows * TOP_K), :]
        pltpu.make_async_copy(done, done, sem).wait()

    @pl.when(i < n_p // tok_tile)
    def _():
        scatter_block(xp_ref, tok_tile, i * tok_tile * TOP_K)

    @pl.when(i == n_p // tok_tile)
    def _():
        scatter_block(xs_ref, n_s, n_p * TOP_K)


def _scatter_rows(pos, last_tile, xf_p, xf_s, n_rows):
    n_p, w = xf_p.shape
    n_s = xf_s.shape[0]
    tok_tile = next(t for t in range(512, 0, -SUBLANES) if n_p % t == 0)
    n_p_steps = n_p // tok_tile
    kern = functools.partial(_scatter_kernel, tok_tile=tok_tile, n_p=n_p, n_s=n_s)
    grid_spec = pltpu.PrefetchScalarGridSpec(
        num_scalar_prefetch=2,
        grid=(n_p_steps + 1,),
        in_specs=[pl.BlockSpec((tok_tile, w), lambda i, p, q: (jnp.minimum(i, n_p_steps - 1), 0)),
                  pl.BlockSpec((n_s, w), lambda i, p, q: (0, 0))],
        out_specs=pl.BlockSpec(memory_space=pl.ANY),
        scratch_shapes=[pltpu.SemaphoreType.DMA],
    )
    return pl.pallas_call(
        kern,
        out_shape=jax.ShapeDtypeStruct((n_rows, w), xf_p.dtype),
        grid_spec=grid_spec,
        compiler_params=_cparams(("arbitrary",), 32),
        name="moe_scatter",
    )(pos, last_tile, xf_p, xf_s)


def _moe_kernel(e_ref, start_ref, tiles_ref, x_hbm, wu_ref, bu_ref, wd_ref, bd_ref, y_hbm,
                x_stage, x_bf, acc, wu_bf, wd_perm, wd_bf, sem_in, sem_out, *, n_ff, d_model, n_items):
    del e_ref
    it = pl.program_id(0)
    f = pl.program_id(1)
    n_tiles = tiles_ref[it]
    row0 = pl.multiple_of(start_ref[it] * MOE_TILE, MOE_TILE)
    half = LANES // 2

    def tile_rows(t):
        return pl.ds(pl.multiple_of(t * MOE_TILE, MOE_TILE), MOE_TILE)

    def x_copy(item, t):
        src0 = pl.multiple_of(start_ref[item] * MOE_TILE, MOE_TILE)
        return pltpu.make_async_copy(x_hbm.at[pl.ds(src0 + t * MOE_TILE, MOE_TILE), :],
                                     x_stage.at[tile_rows(t), :], sem_in)

    def y_copy(t):
        return pltpu.make_async_copy(acc.at[tile_rows(t), :],
                                     y_hbm.at[pl.ds(row0 + t * MOE_TILE, MOE_TILE), :], sem_out)

    def for_tiles(count, fn):
        def body(t, carry):
            fn(t)
            return carry
        lax.fori_loop(0, count, body, 0)

    @pl.when(n_tiles > 0)
    def _():
        @pl.when(f == 0)
        def _():
            @pl.when(it == 0)
            def _():
                for_tiles(n_tiles, lambda t: x_copy(it, t).start())

            for_tiles(n_tiles, lambda t: x_copy(it, t).wait())
            b_dn = jnp.broadcast_to(bd_ref[...], (MOE_TILE, d_model))

            def init(t):
                x_bf[tile_rows(t), :] = x_stage[tile_rows(t), :].astype(BF16)
                acc[tile_rows(t), :] = b_dn

            for_tiles(n_tiles, init)
            nxt = jnp.minimum(it + 1, n_items - 1)
            n_next = jnp.where(it + 1 < n_items, tiles_ref[nxt], 0)
            for_tiles(n_next, lambda t: x_copy(nxt, t).start())

        for c in range(MOE_FF_TILE // LANES):
            for s in range(d_model // LANES):
                for par in range(2):
                    src = wd_ref[c * LANES + par * half:c * LANES + (par + 1) * half,
                                 s * LANES:(s + 1) * LANES]
                    wd_perm[s, pl.ds(c * LANES + par, half, stride=2), :] = src
        for s in range(d_model // LANES):
            wd_bf[:, s * LANES:(s + 1) * LANES] = wd_perm[s].astype(BF16)

        b_up = bu_ref[...]

        def span_rows(t, n):
            return pl.ds(pl.multiple_of(t * MOE_TILE, MOE_TILE), n * MOE_TILE)

        def up_proj(t, n, wu):
            return jnp.dot(x_bf[span_rows(t, n), :], wu,
                           preferred_element_type=F32) + b_up

        def activation(hu):
            even = (lax.broadcasted_iota(I32, (hu.shape[0], LANES), 1) % 2) == 0
            acts = []
            for c in range(MOE_FF_TILE // LANES):
                a_blk = hu[:, (2 * c) * LANES:(2 * c + 1) * LANES]
                b_blk = hu[:, (2 * c + 1) * LANES:(2 * c + 2) * LANES]
                gate = jnp.where(even, a_blk, pltpu.roll(b_blk, 1, 1))
                up = jnp.where(even, pltpu.roll(a_blk, LANES - 1, 1), b_blk)
                gate = jnp.minimum(gate, SWIGLU_LIMIT)
                up = jnp.clip(up, -SWIGLU_LIMIT, SWIGLU_LIMIT)
                acts.append((up + 1.0) * gate * jax.nn.sigmoid(SWIGLU_ALPHA * gate))
            return jnp.concatenate(acts, axis=1).astype(BF16)

        def down_proj(t, n, act):
            acc[span_rows(t, n), :] += jnp.dot(act, wd_bf[...], preferred_element_type=F32)

        def chains(spans):
            wu = wu_ref[...].astype(BF16)
            hus = [up_proj(t, n, wu) for t, n in spans]
            acts = [activation(hu) for hu in hus]
            for (t, n), act in zip(spans, acts):
                down_proj(t, n, act)

        def run_tiles(after_tile):
            def quad(p, carry):
                chains([(4 * p, 2), (4 * p + 2, 2)])
                for u in range(4):
                    after_tile(4 * p + u)
                return carry

            n_quads = n_tiles // 4
            lax.fori_loop(0, n_quads, quad, 0)
            rest = n_tiles - 4 * n_quads

            @pl.when(rest >= 2)
            def _():
                chains([(4 * n_quads, 1), (4 * n_quads + 1, 1)])
                after_tile(4 * n_quads)
                after_tile(4 * n_quads + 1)

            @pl.when(rest % 2 == 1)
            def _():
                chains([(n_tiles - 1, 1)])
                after_tile(n_tiles - 1)

        @pl.when(f < n_ff - 1)
        def _():
            run_tiles(lambda t: None)

        @pl.when(f == n_ff - 1)
        def _():
            run_tiles(lambda t: y_copy(t).start())
            for_tiles(n_tiles, lambda t: y_copy(t).wait())


def _moe(item_e, item_start, item_tiles, x_sorted, w_up, b_up, w_down, b_down):
    n_rows = x_sorted.shape[0]
    n_exp, d, ff2_total = w_up.shape
    d_ff = ff2_total // 2
    n_ff = d_ff // MOE_FF_TILE
    n_items = item_e.shape[0]
    item_rows = MOE_ITEM_TILES * MOE_TILE
    kern = functools.partial(_moe_kernel, n_ff=n_ff, d_model=d, n_items=n_items)

    def ff_tile(i, f, n):
        return jnp.where(n[i] > 0, f, n_ff - 1)

    grid_spec = pltpu.PrefetchScalarGridSpec(
        num_scalar_prefetch=3,
        grid=(n_items, n_ff),
        in_specs=[pl.BlockSpec(memory_space=pl.ANY),
                  pl.BlockSpec((None, d, 2 * MOE_FF_TILE),
                               lambda i, f, e, s, n: (e[i], 0, ff_tile(i, f, n))),
                  pl.BlockSpec((None, 1, 2 * MOE_FF_TILE),
                               lambda i, f, e, s, n: (e[i], 0, ff_tile(i, f, n))),
                  pl.BlockSpec((None, MOE_FF_TILE, d),
                               lambda i, f, e, s, n: (e[i], ff_tile(i, f, n), 0)),
                  pl.BlockSpec((None, 1, d), lambda i, f, e, s, n: (e[i], 0, 0))],
        out_specs=pl.BlockSpec(memory_space=pl.ANY),
        scratch_shapes=[pltpu.VMEM((item_rows, d), F32),
                        pltpu.VMEM((item_rows, d), BF16),
                        pltpu.VMEM((item_rows, d), F32),
                        pltpu.VMEM((d, 2 * MOE_FF_TILE), BF16),
                        pltpu.VMEM((d // LANES, MOE_FF_TILE, LANES), F32),
                        pltpu.VMEM((MOE_FF_TILE, d), BF16),
                        pltpu.SemaphoreType.DMA,
                        pltpu.SemaphoreType.DMA],
    )
    return pl.pallas_call(
        kern,
        out_shape=jax.ShapeDtypeStruct((n_rows, d), F32),
        grid_spec=grid_spec,
        compiler_params=_cparams(("arbitrary", "arbitrary"), 56),
        name="moe_experts",
    )(item_e, item_start, item_tiles, x_sorted, w_up,
      b_up.reshape(n_exp, 1, ff2_total), w_down, b_down.reshape(n_exp, 1, d))


def _combine_kernel(pos_ref, y_hbm, h_ref, gate_ref, gt_ref, g_ref, o_ref, buf, sem, *, tok_off):
    i = pl.program_id(0)

    def issue(step, slot):
        base = (step * COMBINE_TOK + tok_off) * TOP_K

        def token(tok, carry):
            for k in range(TOP_K):
                p = pos_ref[base + tok * TOP_K + k]
                pltpu.make_async_copy(y_hbm.at[pl.ds(p, 1), :],
                                      buf.at[slot, pl.ds(k * COMBINE_TOK + tok, 1), :],
                                      sem.at[slot]).start()
            return carry

        lax.fori_loop(0, COMBINE_TOK, token, 0, unroll=2)

    @pl.when(i == 0)
    def _():
        issue(0, 0)

    @pl.when(i + 1 < pl.num_programs(0))
    def _():
        issue(i + 1, (i + 1) % 2)

    slot = i % 2
    pltpu.make_async_copy(y_hbm.at[pl.ds(0, COMBINE_TOK * TOP_K), :], buf.at[slot],
                          sem.at[slot]).wait()
    gates = gate_ref[...]
    ffn = jnp.zeros(h_ref.shape, F32)
    for k in range(TOP_K):
        ffn = ffn + gates[:, k:k + 1] * buf[slot, k * COMBINE_TOK:(k + 1) * COMBINE_TOK, :]
    y = h_ref[...] + gt_ref[...] * ffn
    o_ref[...] = y * lax.rsqrt(jnp.mean(y * y, axis=-1, keepdims=True) + EPS) * g_ref[...]


def _combine(pos, y_rows, h, gate, gate_f, norm_final, *, tok_off, rows_per_group):
    n_tok, d = h.shape
    kern = functools.partial(_combine_kernel, tok_off=tok_off)
    grid_spec = pltpu.PrefetchScalarGridSpec(
        num_scalar_prefetch=1,
        grid=(n_tok // COMBINE_TOK,),
        in_specs=[pl.BlockSpec(memory_space=pl.ANY),
                  pl.BlockSpec((COMBINE_TOK, d), lambda i, p: (i, 0)),
                  pl.BlockSpec((COMBINE_TOK, LANES), lambda i, p: (i, 0)),
                  _mod_spec(gate_f, COMBINE_TOK, rows_per_group),
                  pl.BlockSpec((1, d), lambda i, p: (0, 0))],
        out_specs=pl.BlockSpec((COMBINE_TOK, d), lambda i, p: (i, 0)),
        scratch_shapes=[pltpu.VMEM((2, COMBINE_TOK * TOP_K, d), F32),
                        pltpu.SemaphoreType.DMA((2,))],
    )
    return pl.pallas_call(
        kern,
        out_shape=jax.ShapeDtypeStruct((n_tok, d), F32),
        grid_spec=grid_spec,
        compiler_params=_cparams(("arbitrary",), 32),
        name="moe_combine",
    )(pos, y_rows, h, gate, gate_f, norm_final.reshape(1, d))


def _routing_tables(top_idx):
    n_tok = top_idx.shape[0]
    n_assign = n_tok * TOP_K
    flat_e = top_idx.reshape(-1)
    onehot = (flat_e[:, None] == jnp.arange(N_EXPERTS, dtype=I32)[None, :]).astype(I32)
    running = jnp.cumsum(onehot, axis=0)
    rank = jnp.take_along_axis(running, flat_e[:, None], axis=1)[:, 0] - 1
    counts = running[-1]
    tiles_e = (counts + MOE_TILE - 1) // MOE_TILE
    tile_start_e = jnp.cumsum(tiles_e) - tiles_e
    pos = tile_start_e[flat_e] * MOE_TILE + rank

    n_tiles_max = -(-n_assign // MOE_TILE) + N_EXPERTS
    n_rows = n_tiles_max * MOE_TILE
    last_tile = jnp.where(tiles_e > 0, tile_start_e + tiles_e - 1, -1)

    n_items_max = N_EXPERTS + n_tiles_max // MOE_ITEM_TILES
    items_e = (tiles_e + MOE_ITEM_TILES - 1) // MOE_ITEM_TILES
    items_end = jnp.cumsum(items_e)
    it = jnp.arange(n_items_max, dtype=I32)
    e_of = jnp.minimum(jnp.sum((it[:, None] >= items_end[None, :]).astype(I32), axis=1), N_EXPERTS - 1)
    local = it - (items_end[e_of] - items_e[e_of])
    valid = it < items_end[-1]
    tiles_left = tiles_e[e_of] - local * MOE_ITEM_TILES
    item_tiles = jnp.where(valid, jnp.clip(tiles_left, 0, MOE_ITEM_TILES), 0).astype(I32)
    item_start = jnp.where(valid, tile_start_e[e_of] + local * MOE_ITEM_TILES, 0).astype(I32)
    last_e = e_of[jnp.maximum(items_end[-1] - 1, 0)]
    item_e = jnp.where(valid, e_of, last_e).astype(I32)
    return pos.astype(I32), last_tile.astype(I32), n_rows, item_e, item_start, item_tiles


def _pick_tile(n, pref):
    t = min(n, pref)
    while n % t:
        t //= 2
    return t


def kernel(x_prompt, x_sample, c_prompt, c_sample, cache_k, cache_v, page_table, state_C, state_n, state_m, w_ada, b_ada, norm_mix, norm_ffn, w_in, b_gates, sb_bias, norm_sb_out, norm_ml_out, w_out, w_router, b_router, w_up, b_up, w_down, b_down, norm_final):
    bsz, seq, d = x_prompt.shape
    dbs, dec_seq, _ = x_sample.shape
    depth = w_ada.shape[0]
    assert depth == 1
    n_p, n_s = bsz * seq, dbs * dec_seq
    n_tok = n_p + n_s

    (w_ada, b_ada, norm_mix, norm_ffn, w_in, b_gates, sb_bias, norm_sb_out, norm_ml_out, w_out,
     w_router, b_router, w_up, b_up, w_down, b_down) = [
        a[0] for a in (w_ada, b_ada, norm_mix, norm_ffn, w_in, b_gates, sb_bias, norm_sb_out,
                       norm_ml_out, w_out, w_router, b_router, w_up, b_up, w_down, b_down)]

    n_c = bsz + dbs
    n_c_pad = -(-n_c // SUBLANES) * SUBLANES
    c_all = jnp.concatenate([c_prompt, c_sample, jnp.zeros((n_c_pad - n_c, d), F32)], axis=0)
    mod = _ada(c_all, w_ada, b_ada)
    mods_p = [mod[:bsz, i * d:(i + 1) * d].reshape(bsz, 1, d) for i in range(N_MOD)]
    mods_s = [jnp.repeat(mod[bsz:n_c, i * d:(i + 1) * d], dec_seq, axis=0).reshape(1, n_s, d)
              for i in range(N_MOD)]

    wg_pad = w_in[:, w_in.shape[1] - LANES:]
    bg_pad = jnp.pad(b_gates, (GATE_LANE0, 0)).reshape(1, LANES)
    wr_pad = jnp.pad(w_router, ((0, 0), (0, LANES - N_EXPERTS)))
    br_pad = jnp.pad(b_router, (0, LANES - N_EXPERTS)).reshape(1, LANES)

    xp2 = x_prompt.reshape(n_p, d)
    xs2 = x_sample.reshape(n_s, d)
    tm_p = _pick_tile(seq, 1024)

    proj_p, gates_p, k_p, v_p = _inproj(xp2, mods_p[0], mods_p[1], norm_mix, w_in, wg_pad, bg_pad,
                                        tm=tm_p, rows_per_group=seq)
    osb_p = _sb_prompt(proj_p, sb_bias, norm_sb_out, bsz=bsz, seq=seq)
    chunk_p = _pick_tile(seq, ML_CHUNK)
    hml_p, c_p, nn_p, m_p = _mlstm(
        proj_p, gates_p,
        jnp.zeros((bsz, H_ML, DV_ML, DQK_ML), F32), jnp.zeros((bsz, H_ML, DQK_ML), F32),
        jnp.zeros((bsz, H_ML), F32), norm_ml_out,
        bsz=bsz, n_chunks=seq // chunk_p, chunk=chunk_p, valid=chunk_p)

    proj_s, gates_s, k_s, v_s = _inproj(xs2, mods_s[0], mods_s[1], norm_mix, w_in, wg_pad, bg_pad,
                                        tm=n_s, rows_per_group=n_s)
    osb_s = _sb_paged(proj_s[:, :SB_W].reshape(dbs, dec_seq, SB_W),
                      k_s.reshape(dbs, dec_seq, SB_W), v_s.reshape(dbs, dec_seq, SB_W),
                      cache_k[0], cache_v[0], page_table, sb_bias, norm_sb_out)
    osb_s = (osb_s.reshape(dbs, H_SB, dec_seq, D_SB).swapaxes(1, 2)
             .reshape(n_s, SB_W).astype(BF16))
    chunk_s = -(-dec_seq // BF16_SUBLANES) * BF16_SUBLANES
    pad_rows = lambda a: jnp.pad(a.reshape(dbs, dec_seq, a.shape[-1]),
                                 ((0, 0), (0, chunk_s - dec_seq), (0, 0))).reshape(dbs * chunk_s, a.shape[-1])
    hml_s, c_s, nn_s, m_s = _mlstm(
        pad_rows(proj_s), pad_rows(gates_s), state_C[0], state_n[0], state_m[0], norm_ml_out,
        bsz=dbs, n_chunks=1, chunk=chunk_s, valid=dec_seq)
    hml_s = hml_s.reshape(dbs, chunk_s, ML_V_W)[:, :dec_seq].reshape(n_s, ML_V_W)

    tm_o = _pick_tile(seq, 512)
    h_p, xf_p, idx_p, gate_p = _outproj(
        osb_p, hml_p, w_out, xp2, mods_p[2], mods_p[3], mods_p[4], norm_ffn, wr_pad, br_pad,
        tm=tm_o, rows_per_group=seq)
    h_s, xf_s, idx_s, gate_s = _outproj(
        osb_s, hml_s, w_out, xs2, mods_s[2], mods_s[3], mods_s[4], norm_ffn, wr_pad, br_pad,
        tm=n_s, rows_per_group=n_s)
    top_idx = jnp.concatenate([idx_p[:, :TOP_K], idx_s[:, :TOP_K]], axis=0)

    pos, last_tile, n_rows, item_e, item_start, item_tiles = _routing_tables(top_idx)
    x_sorted = _scatter_rows(pos, last_tile, xf_p, xf_s, n_rows)
    y_rows = _moe(item_e, item_start, item_tiles, x_sorted, w_up, b_up, w_down, b_down)
    y_p = _combine(pos, y_rows, h_p, gate_p, mods_p[5], norm_final, tok_off=0, rows_per_group=seq)
    y_s = _combine(pos, y_rows, h_s, gate_s, mods_s[5], norm_final, tok_off=n_p, rows_per_group=n_s)

    return (y_p.reshape(bsz, seq, d), y_s.reshape(dbs, dec_seq, d),
            k_p.reshape(1, bsz, seq, H_SB, D_SB), v_p.reshape(1, bsz, seq, H_SB, D_SB),
            k_s.reshape(1, dbs, dec_seq, H_SB, D_SB), v_s.reshape(1, dbs, dec_seq, H_SB, D_SB),
            c_p[None], nn_p[None], m_p[None], c_s[None], nn_s[None], m_s[None])
```

```python
import functools

import jax
import jax.numpy as jnp
from jax import lax
from jax.experimental import pallas as pl
from jax.experimental.pallas import tpu as pltpu

F32 = jnp.float32
BF16 = jnp.bfloat16
I32 = jnp.int32

H_SB = 8
D_SB = 128
SB_W = H_SB * D_SB
H_ML = 4
DQK_ML = 128
DV_ML = 256
ML_QK_W = H_ML * DQK_ML
ML_V_W = H_ML * DV_ML
PROJ_MAIN_W = 3 * SB_W + 2 * ML_QK_W + 2 * ML_V_W
N_GATES = 2 * H_ML
N_EXPERTS = 32
TOP_K = 4
N_MOD = 6
SWIGLU_LIMIT = 7.0
SWIGLU_ALPHA = 1.702
EPS = 1e-6

LANES = 128
SUBLANES = 8
BF16_SUBLANES = 16
MIB = 1024 * 1024
GATE_LANE0 = LANES - N_GATES

MOE_TILE = 128
MOE_ITEM_TILES = 12
MOE_FF_TILE = 256
ML_CHUNK = 128
SB_BLOCK = 512
SB_SUB = 256
PAGES_PER_STEP = 16
COMBINE_TOK = 64

NT_DIMS = (((1,), (1,)), ((), ()))
TN_DIMS = (((0,), (0,)), ((), ()))


def _cparams(sem, vmem_mib):
    return pltpu.CompilerParams(dimension_semantics=sem, vmem_limit_bytes=vmem_mib * MIB)


def _log_sigmoid_pair(z):
    t = jnp.log(1.0 + jnp.exp(-jnp.abs(z)))
    return jnp.minimum(z, 0.0) - t, -jnp.maximum(z, 0.0) - t


def _split_bf16(x):
    hi = x.astype(BF16)
    lo = (x - hi.astype(F32)).astype(BF16)
    return hi, lo


def _dot3(x, w):
    xh, xl = _split_bf16(x)
    wh, wl = _split_bf16(w)
    return (jnp.dot(xh, wh, preferred_element_type=F32)
            + (jnp.dot(xl, wh, preferred_element_type=F32)
               + jnp.dot(xh, wl, preferred_element_type=F32)))


def _mod_spec(arr, tile, rows_per_group, col_tile=None):
    per_group = arr.shape[1] == 1
    width = arr.shape[2] if col_tile is None else col_tile

    def index_map(i, *rest):
        col = 0 if col_tile is None else rest[0]
        if per_group:
            return ((i * tile) // rows_per_group, 0, col)
        return (0, i, col)

    return pl.BlockSpec((None, 1 if per_group else tile, width), index_map)


def _ada_kernel(c_ref, w_ref, b_ref, o_ref):
    c = c_ref[...]
    o_ref[...] = _dot3(c * jax.nn.sigmoid(c), w_ref[...]) + b_ref[...]


def _ada(c_all, w_ada, b_ada):
    n, d = c_all.shape
    w_out = w_ada.shape[1]
    tn = 1024
    return pl.pallas_call(
        _ada_kernel,
        out_shape=jax.ShapeDtypeStruct((n, w_out), F32),
        grid=(w_out // tn,),
        in_specs=[pl.BlockSpec((n, d), lambda j: (0, 0)),
                  pl.BlockSpec((d, tn), lambda j: (0, j)),
                  pl.BlockSpec((1, tn), lambda j: (0, j))],
        out_specs=pl.BlockSpec((n, tn), lambda j: (0, j)),
        compiler_params=_cparams(("arbitrary",), 40),
        name="ada",
    )(c_all, w_ada, b_ada.reshape(1, w_out))


def _inproj_kernel(x_ref, sh_ref, sc_ref, g_ref, w_ref, wg_ref, bg_ref,
                   proj_ref, gates_ref, k_ref, v_ref, xm_scr, *, tn):
    j = pl.program_id(1)

    @pl.when(j == 0)
    def _():
        x = x_ref[...]
        xn = x * lax.rsqrt(jnp.mean(x * x, axis=-1, keepdims=True) + EPS) * g_ref[...]
        xm = (xn * (1.0 + sc_ref[...]) + sh_ref[...]).astype(BF16)
        xm_scr[...] = xm
        gates_ref[...] = jnp.dot(xm, wg_ref[...].astype(BF16),
                                 preferred_element_type=F32) + bg_ref[...]

    res = jnp.dot(xm_scr[...], w_ref[...].astype(BF16), preferred_element_type=F32)
    proj_ref[...] = res
    k_tiles = SB_W // tn

    @pl.when((j >= k_tiles) & (j < 2 * k_tiles))
    def _():
        k_ref[...] = res

    @pl.when((j >= 2 * k_tiles) & (j < 3 * k_tiles))
    def _():
        v_ref[...] = res


def _inproj(x2d, shift, scale, norm_w, w_in, wg_pad, bg_pad, *, tm, rows_per_group):
    t, d = x2d.shape
    tn = 512
    k_tiles = SB_W // tn
    mod_spec = _mod_spec(shift, tm, rows_per_group)

    def kv_spec(first):
        return pl.BlockSpec((tm, tn), lambda i, j: (i, jnp.clip(j - first, 0, k_tiles - 1)))

    return pl.pallas_call(
        functools.partial(_inproj_kernel, tn=tn),
        out_shape=(jax.ShapeDtypeStruct((t, PROJ_MAIN_W), F32),
                   jax.ShapeDtypeStruct((t, LANES), F32),
                   jax.ShapeDtypeStruct((t, SB_W), F32),
                   jax.ShapeDtypeStruct((t, SB_W), F32)),
        grid=(t // tm, PROJ_MAIN_W // tn),
        in_specs=[pl.BlockSpec((tm, d), lambda i, j: (i, 0)),
                  mod_spec, mod_spec,
                  pl.BlockSpec((1, d), lambda i, j: (0, 0)),
                  pl.BlockSpec((d, tn), lambda i, j: (0, j)),
                  pl.BlockSpec((d, LANES), lambda i, j: (0, 0)),
                  pl.BlockSpec((1, LANES), lambda i, j: (0, 0))],
        out_specs=(pl.BlockSpec((tm, tn), lambda i, j: (i, j)),
                   pl.BlockSpec((tm, LANES), lambda i, j: (i, 0)),
                   kv_spec(k_tiles), kv_spec(2 * k_tiles)),
        scratch_shapes=[pltpu.VMEM((tm, d), BF16)],
        compiler_params=_cparams(("arbitrary", "arbitrary"), 48),
        name="inproj",
    )(x2d, shift, scale, norm_w.reshape(1, d), w_in, wg_pad, bg_pad)


def _sb_strip(z, mask, from_uu, sub, v_bf, spent):
    pos = jnp.maximum(z, 0.0)
    cost = pos + jnp.log(1.0 + jnp.exp(jnp.minimum(z, 0.0) - pos))
    if mask is not None:
        cost = jnp.where(mask, cost, 0.0)
    hi, lo = _split_bf16(cost)
    n_sub = z.shape[1] // sub
    parts = [None] * n_sub
    for j in reversed(range(n_sub)):
        cols = slice(j * sub, (j + 1) * sub)
        from_s = jnp.dot(jnp.concatenate([hi[:, cols], lo[:, cols]], axis=1), from_uu,
                         preferred_element_type=F32)
        parts[j] = jnp.exp(z[:, cols] - (from_s + spent))
        spent = spent + from_s[:, 0:1]
    a = parts[0] if n_sub == 1 else jnp.concatenate(parts, axis=1)
    if mask is not None:
        a = jnp.where(mask, a, 0.0)
    return jnp.dot(a.astype(BF16), v_bf, preferred_element_type=F32), spent


def _from_matrix2(n):
    j = lax.broadcasted_iota(I32, (2 * n, n), 0) % n
    s = lax.broadcasted_iota(I32, (2 * n, n), 1)
    return jnp.where(j >= s, 1.0, 0.0).astype(BF16)


def _sbp_kernel(bias_ref, q_ref, k_ref, v_ref, g_ref, o_ref, o_scr, carry_scr, *, blk, sub, scale):
    h = pl.program_id(1)
    qi = pl.program_id(2)
    bias = bias_ref[h]
    q = q_ref[...].astype(BF16)
    from_uu = _from_matrix2(sub)

    def strip(k_start, width, diagonal):
        k_bf = k_ref[pl.ds(k_start, width), :].astype(BF16)
        v_bf = v_ref[pl.ds(k_start, width), :].astype(BF16)
        z = lax.dot_general(q, k_bf, NT_DIMS, preferred_element_type=F32) * scale + bias
        mask = None
        if diagonal:
            t = lax.broadcasted_iota(I32, (blk, width), 0)
            s = lax.broadcasted_iota(I32, (blk, width), 1)
            mask = s < t
        o_blk, spent = _sb_strip(z, mask, from_uu, sub, v_bf, carry_scr[:, 0:1])
        o_scr[...] += o_blk
        carry_scr[...] = jnp.broadcast_to(spent, carry_scr.shape)

    o_scr[...] = jnp.zeros_like(o_scr)
    carry_scr[...] = jnp.zeros_like(carry_scr)
    strip(pl.multiple_of(qi * blk, blk), blk, True)

    @pl.when(qi % 2 == 1)
    def _():
        strip(pl.multiple_of((qi - 1) * blk, blk), blk, False)

    def older(n, carry):
        strip(pl.multiple_of((qi // 2 - 1 - n) * (2 * blk), 2 * blk), 2 * blk, False)
        return carry

    lax.fori_loop(0, qi // 2, older, 0)
    o = o_scr[...]
    o = o * lax.rsqrt(jnp.mean(o * o, axis=-1, keepdims=True) + EPS) * g_ref[...]
    o_ref[...] = o.astype(o_ref.dtype)


def _sb_prompt(proj, sb_bias, norm_sb_out, *, bsz, seq):
    blk = min(SB_BLOCK, seq)
    nq = seq // blk
    kern = functools.partial(_sbp_kernel, blk=blk, sub=min(SB_SUB, blk), scale=D_SB ** -0.5)
    return pl.pallas_call(
        kern,
        out_shape=jax.ShapeDtypeStruct((bsz * seq, SB_W), BF16),
        grid=(bsz, H_SB, nq),
        in_specs=[pl.BlockSpec(memory_space=pltpu.SMEM),
                  pl.BlockSpec((blk, D_SB), lambda b, h, i: (b * nq + i, h)),
                  pl.BlockSpec((seq, D_SB), lambda b, h, i: (b, H_SB + h)),
                  pl.BlockSpec((seq, D_SB), lambda b, h, i: (b, 2 * H_SB + h)),
                  pl.BlockSpec((None, 1, D_SB), lambda b, h, i: (h, 0, 0))],
        out_specs=pl.BlockSpec((blk, D_SB), lambda b, h, i: (b * nq + i, h)),
        scratch_shapes=[pltpu.VMEM((blk, D_SB), F32), pltpu.VMEM((blk, LANES), F32)],
        compiler_params=_cparams(("arbitrary", "arbitrary", "arbitrary"), 48),
        name="sb_prompt",
    )(sb_bias, proj, proj, proj, norm_sb_out.reshape(H_SB, 1, D_SB))


def _head_block_mask(rows, dec_seq):
    r = lax.broadcasted_iota(I32, (rows, SB_W), 0)
    c = lax.broadcasted_iota(I32, (rows, SB_W), 1)
    return (r // dec_seq) == (c // D_SB)


def _page_heads(page_ref):
    n_keys = page_ref.shape[0] // H_SB
    cols = [page_ref[pl.ds(h, n_keys, stride=H_SB), :] for h in range(H_SB)]
    return jnp.concatenate(cols, axis=1).astype(BF16)


def _sbd_kernel(pt_ref, qbd_ref, kn_ref, vn_ref, bias_ref, g_ref, *rest,
                dec_seq, page, n_steps, scale):
    del pt_ref
    pages = rest[:2 * PAGES_PER_STEP]
    o_ref = rest[2 * PAGES_PER_STEP]
    o_scr, carry_scr, new_scr = rest[2 * PAGES_PER_STEP + 1:]
    rows = H_SB * dec_seq
    j = pl.program_id(1)
    from_uu = _from_matrix2(page)

    def sweep(k_bf, v_bf, mask, spent):
        width = k_bf.shape[0]
        z = (lax.dot_general(qbd_ref[...], k_bf, NT_DIMS, preferred_element_type=F32) * scale
             + bias_ref[:, 0:width])
        return _sb_strip(z, mask, from_uu, page, v_bf, spent)

    @pl.when(j == 0)
    def _():
        new_scr[...] = jnp.zeros_like(new_scr)
        new_scr[0, 0:dec_seq, :] = kn_ref[...]
        new_scr[1, 0:dec_seq, :] = vn_ref[...]
        r = lax.broadcasted_iota(I32, (rows, page), 0)
        s = lax.broadcasted_iota(I32, (rows, page), 1)
        o_new, spent = sweep(new_scr[0].astype(BF16), new_scr[1].astype(BF16), s < (r % dec_seq),
                             jnp.zeros((rows, 1), F32))
        o_scr[...] = o_new
        carry_scr[...] = jnp.broadcast_to(spent, carry_scr.shape)

    order = range(PAGES_PER_STEP - 1, -1, -1)
    k_bf = jnp.concatenate([_page_heads(pages[i]) for i in order], axis=0)
    v_bf = jnp.concatenate([_page_heads(pages[PAGES_PER_STEP + i]) for i in order], axis=0)
    o_blk, spent = sweep(k_bf, v_bf, None, carry_scr[:, 0:1])
    o_scr[...] += o_blk
    carry_scr[...] = jnp.broadcast_to(spent, carry_scr.shape)

    @pl.when(j == n_steps - 1)
    def _():
        o_full = jnp.where(_head_block_mask(rows, dec_seq), o_scr[...], 0.0)
        o = o_full[:, 0:D_SB]
        for h in range(1, H_SB):
            o = o + o_full[:, h * D_SB:(h + 1) * D_SB]
        o = o * lax.rsqrt(jnp.mean(o * o, axis=-1, keepdims=True) + EPS) * g_ref[...]
        o_ref[...] = o


def _sb_paged(q_s, k_new, v_new, cache_k, cache_v, page_table, sb_bias, norm_sb_out):
    bsz, dec_seq, _ = q_s.shape
    n_pool, page = cache_k.shape[0], cache_k.shape[1]
    n_pages = page_table.shape[1]
    assert n_pages % PAGES_PER_STEP == 0
    n_steps = n_pages // PAGES_PER_STEP
    rows = H_SB * dec_seq
    ck = cache_k.reshape(n_pool, page * H_SB, D_SB)
    cv = cache_v.reshape(n_pool, page * H_SB, D_SB)
    strip_w = PAGES_PER_STEP * page
    bias_rows = jnp.broadcast_to(jnp.repeat(sb_bias, dec_seq)[:, None], (rows, strip_w)).astype(F32)
    g_rows = jnp.repeat(norm_sb_out.reshape(H_SB, D_SB), dec_seq, axis=0)
    q_heads = q_s.reshape(bsz, dec_seq, H_SB, D_SB).swapaxes(1, 2)
    eye = jnp.eye(H_SB, dtype=F32)
    qbd = (q_heads[:, :, :, None, :] * eye[None, :, None, :, None]).reshape(bsz, rows, SB_W).astype(BF16)

    def page_spec(i):
        return pl.BlockSpec(
            (None, page * H_SB, D_SB),
            lambda b, j, pt: (pt[b, n_pages - 1 - (j * PAGES_PER_STEP + i)], 0, 0))

    tok_spec = pl.BlockSpec((None, dec_seq, SB_W), lambda b, j, pt: (b, 0, 0))
    kern = functools.partial(_sbd_kernel, dec_seq=dec_seq, page=page, n_steps=n_steps,
                             scale=D_SB ** -0.5)
    grid_spec = pltpu.PrefetchScalarGridSpec(
        num_scalar_prefetch=1,
        grid=(bsz, n_steps),
        in_specs=[pl.BlockSpec((None, rows, SB_W), lambda b, j, pt: (b, 0, 0)),
                  tok_spec, tok_spec,
                  pl.BlockSpec((rows, strip_w), lambda b, j, pt: (0, 0)),
                  pl.BlockSpec((rows, D_SB), lambda b, j, pt: (0, 0))]
                 + [page_spec(i) for i in range(PAGES_PER_STEP)] * 2,
        out_specs=pl.BlockSpec((None, rows, D_SB), lambda b, j, pt: (b, 0, 0)),
        scratch_shapes=[pltpu.VMEM((rows, SB_W), F32),
                        pltpu.VMEM((rows, LANES), F32),
                        pltpu.VMEM((2, page, SB_W), F32)],
    )
    return pl.pallas_call(
        kern,
        out_shape=jax.ShapeDtypeStruct((bsz, rows, D_SB), F32),
        grid_spec=grid_spec,
        compiler_params=_cparams(("arbitrary", "arbitrary"), 56),
        name="sb_paged",
    )(page_table, qbd, k_new, v_new, bias_rows, g_rows,
      *([ck] * PAGES_PER_STEP), *([cv] * PAGES_PER_STEP))


def _mlstm_kernel(q_ref, k_ref, v_ref, og_ref, gcol_ref, grow_ref, c0_ref, n0_ref, m0_ref, gn_ref,
                  h_ref, c_out, n_out, m_out, c_scr, n_scr, m_scr, *, chunk, valid, n_chunks):
    ci = pl.program_id(1)

    @pl.when(ci == 0)
    def _():
        c_scr[...] = c0_ref[...]
        n_scr[...] = n0_ref[...]
        m_scr[...] = m0_ref[...]

    t_idx = lax.broadcasted_iota(I32, (chunk, chunk), 0)
    s_idx = lax.broadcasted_iota(I32, (chunk, chunk), 1)
    causal = s_idx <= t_idx
    col_valid = lax.broadcasted_iota(I32, (chunk, 1), 0) < valid
    row_valid = lax.broadcasted_iota(I32, (1, chunk), 1) < valid
    gcol = gcol_ref[...]
    grow = grow_ref[...]
    neg_inf = -jnp.inf

    def gating(h):
        q = q_ref[:, h * DQK_ML:(h + 1) * DQK_ML]
        ks = k_ref[:, h * DQK_ML:(h + 1) * DQK_ML] * (DQK_ML ** -0.5)
        q_bf, ks_bf = q.astype(BF16), ks.astype(BF16)
        qk = lax.dot_general(q_bf, ks_bf, NT_DIMS, preferred_element_type=F32)

        gi, gf = GATE_LANE0 + h, GATE_LANE0 + H_ML + h
        i_col = jnp.where(col_valid, gcol[:, gi:gi + 1], neg_inf)
        i_row = jnp.where(row_valid, grow[h:h + 1, :], neg_inf)
        lf_col = jnp.where(col_valid, _log_sigmoid_pair(gcol[:, gf:gf + 1])[0], 0.0)
        lf_row = jnp.where(row_valid, _log_sigmoid_pair(grow[H_ML + h:H_ML + h + 1, :])[0], 0.0)
        bcum_col = jnp.sum(jnp.where(causal, lf_row, 0.0), axis=1, keepdims=True)
        bcum_row = jnp.sum(jnp.where(t_idx <= s_idx, lf_col, 0.0), axis=0, keepdims=True)

        log_w = jnp.where(causal, bcum_col - bcum_row + i_row, neg_inf)
        m0 = m_scr[h:h + 1, 0:1]
        m_state = bcum_col + m0
        m_new = jnp.maximum(m_state, jnp.max(log_w, axis=1, keepdims=True))
        s_mat = jnp.exp(log_w - m_new) * qk
        g = jnp.exp(m_state - m_new)
        m_end = m_new[chunk - 1:chunk, :]
        b_last = bcum_col[chunk - 1:chunk, :]
        w_end = jnp.exp(b_last - bcum_col + i_col - m_end)
        g_end = jnp.exp(b_last + m0 - m_end)
        return dict(q=q, ks=ks, q_bf=q_bf, ks_bf=ks_bf, s_mat=s_mat, g=g, m_new=m_new,
                    m_end=m_end, w_end=w_end, g_end=g_end)

    def readout(h, t):
        v = v_ref[:, h * DV_ML:(h + 1) * DV_ML]
        c0 = c_scr[h]
        n0 = n_scr[h:h + 1, :]
        num = (jnp.dot(t["s_mat"].astype(BF16), v.astype(BF16), preferred_element_type=F32)
               + t["g"] * lax.dot_general(t["q_bf"], c0.astype(BF16), NT_DIMS,
                                          preferred_element_type=F32))
        den = (jnp.sum(t["s_mat"], axis=1, keepdims=True)
               + t["g"] * jnp.sum(t["q"] * n0, axis=1, keepdims=True))
        return num / jnp.maximum(jnp.abs(den), jnp.exp(-t["m_new"]))

    def update_state(h, t):
        v = v_ref[:, h * DV_ML:(h + 1) * DV_ML]
        vw_bf = (v * t["w_end"]).astype(BF16)
        c_scr[h] = t["g_end"] * c_scr[h] + lax.dot_general(vw_bf, t["ks_bf"], TN_DIMS,
                                                           preferred_element_type=F32)
        n_scr[h:h + 1, :] = (t["g_end"] * n_scr[h:h + 1, :]
                             + jnp.sum(t["ks"] * t["w_end"], axis=0, keepdims=True))
        m_scr[h:h + 1, :] = jnp.broadcast_to(t["m_end"], (1, LANES))

    def emit(h, hh):
        hn = (hh * lax.rsqrt(jnp.mean(hh * hh, axis=-1, keepdims=True) + EPS)
              * gn_ref[:, h * DV_ML:(h + 1) * DV_ML])
        out = hn * jax.nn.sigmoid(og_ref[:, h * DV_ML:(h + 1) * DV_ML])
        h_ref[:, h * DV_ML:(h + 1) * DV_ML] = out.astype(h_ref.dtype)

    terms = [gating(h) for h in range(H_ML)]
    hhs = [readout(h, terms[h]) for h in range(H_ML)]
    for h in range(H_ML):
        update_state(h, terms[h])
    for h in range(H_ML):
        emit(h, hhs[h])

    @pl.when(ci == n_chunks - 1)
    def _():
        c_out[...] = c_scr[...]
        n_out[...] = n_scr[...]
        m_out[...] = m_scr[...]


def _mlstm(proj, gates, c0, n0, m0, norm_ml_out, *, bsz, n_chunks, chunk, valid):
    rows = bsz * n_chunks * chunk
    grow = gates[:, GATE_LANE0:].reshape(bsz, n_chunks, chunk, N_GATES).swapaxes(2, 3)
    m0_b = jnp.broadcast_to(m0[:, :, None], (bsz, H_ML, LANES)).astype(F32)
    qk_blk = ML_QK_W // LANES
    row_map = lambda b, c: (b * n_chunks + c)
    kern = functools.partial(_mlstm_kernel, chunk=chunk, valid=valid, n_chunks=n_chunks)
    state_specs = [pl.BlockSpec((None, H_ML, DV_ML, DQK_ML), lambda b, c: (b, 0, 0, 0)),
                   pl.BlockSpec((None, H_ML, DQK_ML), lambda b, c: (b, 0, 0)),
                   pl.BlockSpec((None, H_ML, LANES), lambda b, c: (b, 0, 0))]
    q_col = (3 * SB_W) // ML_QK_W
    v_col = (3 * SB_W + 2 * ML_QK_W) // ML_V_W
    h, c1, n1, m1 = pl.pallas_call(
        kern,
        out_shape=(jax.ShapeDtypeStruct((rows, ML_V_W), BF16),
                   jax.ShapeDtypeStruct((bsz, H_ML, DV_ML, DQK_ML), F32),
                   jax.ShapeDtypeStruct((bsz, H_ML, DQK_ML), F32),
                   jax.ShapeDtypeStruct((bsz, H_ML, LANES), F32)),
        grid=(bsz, n_chunks),
        in_specs=[pl.BlockSpec((chunk, ML_QK_W), lambda b, c: (row_map(b, c), q_col)),
                  pl.BlockSpec((chunk, ML_QK_W), lambda b, c: (row_map(b, c), q_col + 1)),
                  pl.BlockSpec((chunk, ML_V_W), lambda b, c: (row_map(b, c), v_col)),
                  pl.BlockSpec((chunk, ML_V_W), lambda b, c: (row_map(b, c), v_col + 1)),
                  pl.BlockSpec((chunk, LANES), lambda b, c: (row_map(b, c), 0)),
                  pl.BlockSpec((None, None, N_GATES, chunk), lambda b, c: (b, c, 0, 0))]
                 + state_specs
                 + [pl.BlockSpec((1, ML_V_W), lambda b, c: (0, 0))],
        out_specs=(pl.BlockSpec((chunk, ML_V_W), lambda b, c: (row_map(b, c), 0)),) + tuple(state_specs),
        scratch_shapes=[pltpu.VMEM((H_ML, DV_ML, DQK_ML), F32),
                        pltpu.VMEM((H_ML, DQK_ML), F32),
                        pltpu.VMEM((H_ML, LANES), F32)],
        compiler_params=_cparams(("arbitrary", "arbitrary"), 40),
        name="mlstm",
    )(proj, proj, proj, proj, gates, grow, c0, n0, m0_b, norm_ml_out.reshape(1, ML_V_W))
    del qk_blk
    return h, c1, n1, m1[:, :, 0]


def _outproj_kernel(osb_ref, hml_ref, w_ref, x_ref, gt_ref, sh_ref, sc_ref, g_ref, wr_ref, br_ref,
                    h_ref, xf_ref, idx_ref, gate_ref, cat_scr, h_scr, *, tn, n_col):
    j = pl.program_id(1)

    @pl.when(j == 0)
    def _():
        cat_scr[:, 0:SB_W] = osb_ref[...]
        cat_scr[:, SB_W:SB_W + ML_V_W] = hml_ref[...]

    mix = jnp.dot(cat_scr[...], w_ref[...].astype(BF16), preferred_element_type=F32)
    h_blk = x_ref[...] + gt_ref[...] * mix
    h_ref[...] = h_blk
    for jj in range(n_col):
        @pl.when(j == jj)
        def _(jj=jj):
            h_scr[:, jj * tn:(jj + 1) * tn] = h_blk

    @pl.when(j == n_col - 1)
    def _():
        hf = h_scr[...]
        xn = hf * lax.rsqrt(jnp.mean(hf * hf, axis=-1, keepdims=True) + EPS) * g_ref[...]
        xf = xn * (1.0 + sc_ref[...]) + sh_ref[...]
        xf_ref[...] = xf
        logits = _dot3(xf, wr_ref[...]) + br_ref[...]
        lane = lax.broadcasted_iota(I32, logits.shape, 1)
        lane_f = lane.astype(F32)
        neg_inf = -jnp.inf
        lg = jnp.where(lane < N_EXPERTS, logits, neg_inf)
        vals, idxs = [], []
        for _k in range(TOP_K):
            mx = jnp.max(lg, axis=1, keepdims=True)
            ix = jnp.min(jnp.where(lg == mx, lane_f, float(LANES)), axis=1, keepdims=True)
            vals.append(mx)
            idxs.append(ix)
            lg = jnp.where(lane_f == ix, neg_inf, lg)
        es = [jnp.exp(vk - vals[0]) for vk in vals]
        tot = es[0] + es[1] + es[2] + es[3]
        idx_out = jnp.zeros(logits.shape, F32)
        gate_out = jnp.zeros(logits.shape, F32)
        for k in range(TOP_K):
            idx_out = jnp.where(lane == k, idxs[k], idx_out)
            gate_out = jnp.where(lane == k, es[k] / tot, gate_out)
        idx_ref[...] = idx_out.astype(I32)
        gate_ref[...] = gate_out


def _outproj(o_sb, h_ml, w_out, x2d, gate_a, shift_f, scale_f, norm_ffn, wr_pad, br_pad,
             *, tm, rows_per_group):
    t, d = x2d.shape
    tn = 512
    n_col = d // tn
    kern = functools.partial(_outproj_kernel, tn=tn, n_col=n_col)
    return pl.pallas_call(
        kern,
        out_shape=(jax.ShapeDtypeStruct((t, d), F32),
                   jax.ShapeDtypeStruct((t, d), F32),
                   jax.ShapeDtypeStruct((t, LANES), I32),
                   jax.ShapeDtypeStruct((t, LANES), F32)),
        grid=(t // tm, n_col),
        in_specs=[pl.BlockSpec((tm, SB_W), lambda i, j: (i, 0)),
                  pl.BlockSpec((tm, ML_V_W), lambda i, j: (i, 0)),
                  pl.BlockSpec((SB_W + ML_V_W, tn), lambda i, j: (0, j)),
                  pl.BlockSpec((tm, tn), lambda i, j: (i, j)),
                  _mod_spec(gate_a, tm, rows_per_group, col_tile=tn),
                  _mod_spec(shift_f, tm, rows_per_group),
                  _mod_spec(scale_f, tm, rows_per_group),
                  pl.BlockSpec((1, d), lambda i, j: (0, 0)),
                  pl.BlockSpec((d, LANES), lambda i, j: (0, 0)),
                  pl.BlockSpec((1, LANES), lambda i, j: (0, 0))],
        out_specs=(pl.BlockSpec((tm, tn), lambda i, j: (i, j)),
                   pl.BlockSpec((tm, d), lambda i, j: (i, 0)),
                   pl.BlockSpec((tm, LANES), lambda i, j: (i, 0)),
                   pl.BlockSpec((tm, LANES), lambda i, j: (i, 0))),
        scratch_shapes=[pltpu.VMEM((tm, SB_W + ML_V_W), BF16),
                        pltpu.VMEM((tm, d), F32)],
        compiler_params=_cparams(("arbitrary", "arbitrary"), 48),
        name="outproj",
    )(o_sb, h_ml, w_out, x2d, gate_a, shift_f, scale_f, norm_ffn.reshape(1, d), wr_pad, br_pad)


def _scatter_kernel(pos_ref, last_ref, xp_ref, xs_ref, o_hbm, sem, *, tok_tile, n_p, n_s):
    i = pl.program_id(0)
    fill_rows = min(MOE_TILE, tok_tile)

    @pl.when(i == 0)
    def _():
        def fill_copy(e, part):
            row = last_ref[e] * MOE_TILE + part * fill_rows
            return pltpu.make_async_copy(xp_ref.at[pl.ds(0, fill_rows), :],
                                         o_hbm.at[pl.ds(row, fill_rows), :], sem)

        def fill(e, carry):
            @pl.when(last_ref[e] >= 0)
            def _():
                for part in range(MOE_TILE // fill_rows):
                    fill_copy(e, part).start()
            return carry

        def drain(e, carry):
            @pl.when(last_ref[e] >= 0)
            def _():
                for part in range(MOE_TILE // fill_rows):
                    fill_copy(e, part).wait()
            return carry

        lax.fori_loop(0, N_EXPERTS, fill, 0)
        lax.fori_loop(0, N_EXPERTS, drain, 0)

    def scatter_block(x_ref, n_rows, base):
        def token(tok, carry):
            for k in range(TOP_K):
                p = pos_ref[base + tok * TOP_K + k]
                pltpu.make_async_copy(x_ref.at[pl.ds(tok, 1), :], o_hbm.at[pl.ds(p, 1), :],
                                      sem).start(priority=k % 2)
            return carry

        lax.fori_loop(0, n_rows, token, 0, unroll=2)
        done = o_hbm.at[pl.ds(0, n_rows * TOP_K), :]
        pltpu.make_async_copy(done, done, sem).wait()

    @pl.when(i < n_p // tok_tile)
    def _():
        scatter_block(xp_ref, tok_tile, i * tok_tile * TOP_K)

    @pl.when(i == n_p // tok_tile)
    def _():
        scatter_block(xs_ref, n_s, n_p * TOP_K)


def _scatter_rows(pos, last_tile, xf_p, xf_s, n_rows):
    n_p, w = xf_p.shape
    n_s = xf_s.shape[0]
    tok_tile = next(t for t in range(512, 0, -SUBLANES) if n_p % t == 0)
    n_p_steps = n_p // tok_tile
    kern = functools.partial(_scatter_kernel, tok_tile=tok_tile, n_p=n_p, n_s=n_s)
    grid_spec = pltpu.PrefetchScalarGridSpec(
        num_scalar_prefetch=2,
        grid=(n_p_steps + 1,),
        in_specs=[pl.BlockSpec((tok_tile, w), lambda i, p, q: (jnp.minimum(i, n_p_steps - 1), 0)),
                  pl.BlockSpec((n_s, w), lambda i, p, q: (0, 0))],
        out_specs=pl.BlockSpec(memory_space=pl.ANY),
        scratch_shapes=[pltpu.SemaphoreType.DMA],
    )
    return pl.pallas_call(
        kern,
        out_shape=jax.ShapeDtypeStruct((n_rows, w), xf_p.dtype),
        grid_spec=grid_spec,
        compiler_params=_cparams(("arbitrary",), 32),
        name="moe_scatter",
    )(pos, last_tile, xf_p, xf_s)


def _moe_kernel(e_ref, start_ref, tiles_ref, x_hbm, wu_ref, bu_ref, wd_ref, bd_ref, y_hbm,
                x_stage, x_bf, acc, wu_bf, wd_perm, wd_bf, sem_in, sem_out, *, n_ff, d_model, n_items):
    del e_ref
    it = pl.program_id(0)
    f = pl.program_id(1)
    n_tiles = tiles_ref[it]
    row0 = pl.multiple_of(start_ref[it] * MOE_TILE, MOE_TILE)
    half = LANES // 2

    def tile_rows(t):
        return pl.ds(pl.multiple_of(t * MOE_TILE, MOE_TILE), MOE_TILE)

    def x_copy(item, t):
        src0 = pl.multiple_of(start_ref[item] * MOE_TILE, MOE_TILE)
        return pltpu.make_async_copy(x_hbm.at[pl.ds(src0 + t * MOE_TILE, MOE_TILE), :],
                                     x_stage.at[tile_rows(t), :], sem_in)

    def y_copy(t):
        return pltpu.make_async_copy(acc.at[tile_rows(t), :],
                                     y_hbm.at[pl.ds(row0 + t * MOE_TILE, MOE_TILE), :], sem_out)

    def for_tiles(count, fn):
        def body(t, carry):
            fn(t)
            return carry
        lax.fori_loop(0, count, body, 0)

    @pl.when(n_tiles > 0)
    def _():
        @pl.when(f == 0)
        def _():
            @pl.when(it == 0)
            def _():
                for_tiles(n_tiles, lambda t: x_copy(it, t).start())

            for_tiles(n_tiles, lambda t: x_copy(it, t).wait())
            b_dn = jnp.broadcast_to(bd_ref[...], (MOE_TILE, d_model))

            def init(t):
                x_bf[tile_rows(t), :] = x_stage[tile_rows(t), :].astype(BF16)
                acc[tile_rows(t), :] = b_dn

            for_tiles(n_tiles, init)
            nxt = jnp.minimum(it + 1, n_items - 1)
            n_next = jnp.where(it + 1 < n_items, tiles_ref[nxt], 0)
            for_tiles(n_next, lambda t: x_copy(nxt, t).start())

        for c in range(MOE_FF_TILE // LANES):
            for s in range(d_model // LANES):
                for par in range(2):
                    src = wd_ref[c * LANES + par * half:c * LANES + (par + 1) * half,
                                 s * LANES:(s + 1) * LANES]
                    wd_perm[s, pl.ds(c * LANES + par, half, stride=2), :] = src
        for s in range(d_model // LANES):
            wd_bf[:, s * LANES:(s + 1) * LANES] = wd_perm[s].astype(BF16)

        b_up = bu_ref[...]

        def span_rows(t, n):
            return pl.ds(pl.multiple_of(t * MOE_TILE, MOE_TILE), n * MOE_TILE)

        def up_proj(t, n, wu):
            return jnp.dot(x_bf[span_rows(t, n), :], wu,
                           preferred_element_type=F32) + b_up

        def activation(hu):
            even = (lax.broadcasted_iota(I32, (hu.shape[0], LANES), 1) % 2) == 0
            acts = []
            for c in range(MOE_FF_TILE // LANES):
                a_blk = hu[:, (2 * c) * LANES:(2 * c + 1) * LANES]
                b_blk = hu[:, (2 * c + 1) * LANES:(2 * c + 2) * LANES]
                gate = jnp.where(even, a_blk, pltpu.roll(b_blk, 1, 1))
                up = jnp.where(even, pltpu.roll(a_blk, LANES - 1, 1), b_blk)
                gate = jnp.minimum(gate, SWIGLU_LIMIT)
                up = jnp.clip(up, -SWIGLU_LIMIT, SWIGLU_LIMIT)
                acts.append((up + 1.0) * gate * jax.nn.sigmoid(SWIGLU_ALPHA * gate))
            return jnp.concatenate(acts, axis=1).astype(BF16)

        def down_proj(t, n, act):
            acc[span_rows(t, n), :] += jnp.dot(act, wd_bf[...], preferred_element_type=F32)

        def chains(spans):
            wu = wu_ref[...].astype(BF16)
            hus = [up_proj(t, n, wu) for t, n in spans]
            acts = [activation(hu) for hu in hus]
            for (t, n), act in zip(spans, acts):
                down_proj(t, n, act)

        def run_tiles(after_tile):
            def quad(p, carry):
                chains([(4 * p, 2), (4 * p + 2, 2)])
                for u in range(4):
                    after_tile(4 * p + u)
                return carry

            n_quads = n_tiles // 4
            lax.fori_loop(0, n_quads, quad, 0)
            rest = n_tiles - 4 * n_quads

            @pl.when(rest >= 2)
            def _():
                chains([(4 * n_quads, 1), (4 * n_quads + 1, 1)])
                after_tile(4 * n_quads)
                after_tile(4 * n_quads + 1)

            @pl.when(rest % 2 == 1)
            def _():
                chains([(n_tiles - 1, 1)])
                after_tile(n_tiles - 1)

        @pl.when(f < n_ff - 1)
        def _():
            run_tiles(lambda t: None)

        @pl.when(f == n_ff - 1)
        def _():
            run_tiles(lambda t: y_copy(t).start())
            for_tiles(n_tiles, lambda t: y_copy(t).wait())


def _moe(item_e, item_start, item_tiles, x_sorted, w_up, b_up, w_down, b_down):
    n_rows = x_sorted.shape[0]
    n_exp, d, ff2_total = w_up.shape
    d_ff = ff2_total // 2
    n_ff = d_ff // MOE_FF_TILE
    n_items = item_e.shape[0]
    item_rows = MOE_ITEM_TILES * MOE_TILE
    kern = functools.partial(_moe_kernel, n_ff=n_ff, d_model=d, n_items=n_items)

    def ff_tile(i, f, n):
        return jnp.where(n[i] > 0, f, n_ff - 1)

    grid_spec = pltpu.PrefetchScalarGridSpec(
        num_scalar_prefetch=3,
        grid=(n_items, n_ff),
        in_specs=[pl.BlockSpec(memory_space=pl.ANY),
                  pl.BlockSpec((None, d, 2 * MOE_FF_TILE),
                               lambda i, f, e, s, n: (e[i], 0, ff_tile(i, f, n))),
                  pl.BlockSpec((None, 1, 2 * MOE_FF_TILE),
                               lambda i, f, e, s, n: (e[i], 0, ff_tile(i, f, n))),
                  pl.BlockSpec((None, MOE_FF_TILE, d),
                               lambda i, f, e, s, n: (e[i], ff_tile(i, f, n), 0)),
                  pl.BlockSpec((None, 1, d), lambda i, f, e, s, n: (e[i], 0, 0))],
        out_specs=pl.BlockSpec(memory_space=pl.ANY),
        scratch_shapes=[pltpu.VMEM((item_rows, d), F32),
                        pltpu.VMEM((item_rows, d), BF16),
                        pltpu.VMEM((item_rows, d), F32),
                        pltpu.VMEM((d, 2 * MOE_FF_TILE), BF16),
                        pltpu.VMEM((d // LANES, MOE_FF_TILE, LANES), F32),
                        pltpu.VMEM((MOE_FF_TILE, d), BF16),
                        pltpu.SemaphoreType.DMA,
                        pltpu.SemaphoreType.DMA],
    )
    return pl.pallas_call(
        kern,
        out_shape=jax.ShapeDtypeStruct((n_rows, d), F32),
        grid_spec=grid_spec,
        compiler_params=_cparams(("arbitrary", "arbitrary"), 56),
        name="moe_experts",
    )(item_e, item_start, item_tiles, x_sorted, w_up,
      b_up.reshape(n_exp, 1, ff2_total), w_down, b_down.reshape(n_exp, 1, d))


def _combine_kernel(pos_ref, y_hbm, h_ref, gate_ref, gt_ref, g_ref, o_ref, buf, sem, *, tok_off):
    i = pl.program_id(0)

    def issue(step, slot):
        base = (step * COMBINE_TOK + tok_off) * TOP_K

        def token(tok, carry):
            for k in range(TOP_K):
                p = pos_ref[base + tok * TOP_K + k]
                pltpu.make_async_copy(y_hbm.at[pl.ds(p, 1), :],
                                      buf.at[slot, pl.ds(k * COMBINE_TOK + tok, 1), :],
                                      sem.at[slot]).start(priority=k % 2)
            return carry

        lax.fori_loop(0, COMBINE_TOK, token, 0, unroll=2)

    @pl.when(i == 0)
    def _():
        issue(0, 0)

    @pl.when(i + 1 < pl.num_programs(0))
    def _():
        issue(i + 1, (i + 1) % 2)

    slot = i % 2
    pltpu.make_async_copy(y_hbm.at[pl.ds(0, COMBINE_TOK * TOP_K), :], buf.at[slot],
                          sem.at[slot]).wait()
    gates = gate_ref[...]
    ffn = jnp.zeros(h_ref.shape, F32)
    for k in range(TOP_K):
        ffn = ffn + gates[:, k:k + 1] * buf[slot, k * COMBINE_TOK:(k + 1) * COMBINE_TOK, :]
    y = h_ref[...] + gt_ref[...] * ffn
    o_ref[...] = y * lax.rsqrt(jnp.mean(y * y, axis=-1, keepdims=True) + EPS) * g_ref[...]


def _combine(pos, y_rows, h, gate, gate_f, norm_final, *, tok_off, rows_per_group):
    n_tok, d = h.shape
    kern = functools.partial(_combine_kernel, tok_off=tok_off)
    grid_spec = pltpu.PrefetchScalarGridSpec(
        num_scalar_prefetch=1,
        grid=(n_tok // COMBINE_TOK,),
        in_specs=[pl.BlockSpec(memory_space=pl.ANY),
                  pl.BlockSpec((COMBINE_TOK, d), lambda i, p: (i, 0)),
                  pl.BlockSpec((COMBINE_TOK, LANES), lambda i, p: (i, 0)),
                  _mod_spec(gate_f, COMBINE_TOK, rows_per_group),
                  pl.BlockSpec((1, d), lambda i, p: (0, 0))],
        out_specs=pl.BlockSpec((COMBINE_TOK, d), lambda i, p: (i, 0)),
        scratch_shapes=[pltpu.VMEM((2, COMBINE_TOK * TOP_K, d), F32),
                        pltpu.SemaphoreType.DMA((2,))],
    )
    return pl.pallas_call(
        kern,
        out_shape=jax.ShapeDtypeStruct((n_tok, d), F32),
        grid_spec=grid_spec,
        compiler_params=_cparams(("arbitrary",), 32),
        name="moe_combine",
    )(pos, y_rows, h, gate, gate_f, norm_final.reshape(1, d))


def _routing_tables(top_idx):
    n_tok = top_idx.shape[0]
    n_assign = n_tok * TOP_K
    flat_e = top_idx.reshape(-1)
    onehot = (flat_e[:, None] == jnp.arange(N_EXPERTS, dtype=I32)[None, :]).astype(I32)
    running = jnp.cumsum(onehot, axis=0)
    rank = jnp.take_along_axis(running, flat_e[:, None], axis=1)[:, 0] - 1
    counts = running[-1]
    tiles_e = (counts + MOE_TILE - 1) // MOE_TILE
    tile_start_e = jnp.cumsum(tiles_e) - tiles_e
    pos = tile_start_e[flat_e] * MOE_TILE + rank

    n_tiles_max = -(-n_assign // MOE_TILE) + N_EXPERTS
    n_rows = n_tiles_max * MOE_TILE
    last_tile = jnp.where(tiles_e > 0, tile_start_e + tiles_e - 1, -1)

    n_items_max = N_EXPERTS + n_tiles_max // MOE_ITEM_TILES
    items_e = (tiles_e + MOE_ITEM_TILES - 1) // MOE_ITEM_TILES
    items_end = jnp.cumsum(items_e)
    it = jnp.arange(n_items_max, dtype=I32)
    e_of = jnp.minimum(jnp.sum((it[:, None] >= items_end[None, :]).astype(I32), axis=1), N_EXPERTS - 1)
    local = it - (items_end[e_of] - items_e[e_of])
    valid = it < items_end[-1]
    tiles_left = tiles_e[e_of] - local * MOE_ITEM_TILES
    item_tiles = jnp.where(valid, jnp.clip(tiles_left, 0, MOE_ITEM_TILES), 0).astype(I32)
    item_start = jnp.where(valid, tile_start_e[e_of] + local * MOE_ITEM_TILES, 0).astype(I32)
    last_e = e_of[jnp.maximum(items_end[-1] - 1, 0)]
    item_e = jnp.where(valid, e_of, last_e).astype(I32)
    return pos.astype(I32), last_tile.astype(I32), n_rows, item_e, item_start, item_tiles


def _pick_tile(n, pref):
    t = min(n, pref)
    while n % t:
        t //= 2
    return t


def kernel(x_prompt, x_sample, c_prompt, c_sample, cache_k, cache_v, page_table, state_C, state_n, state_m, w_ada, b_ada, norm_mix, norm_ffn, w_in, b_gates, sb_bias, norm_sb_out, norm_ml_out, w_out, w_router, b_router, w_up, b_up, w_down, b_down, norm_final):
    bsz, seq, d = x_prompt.shape
    dbs, dec_seq, _ = x_sample.shape
    depth = w_ada.shape[0]
    assert depth == 1
    n_p, n_s = bsz * seq, dbs * dec_seq
    n_tok = n_p + n_s

    (w_ada, b_ada, norm_mix, norm_ffn, w_in, b_gates, sb_bias, norm_sb_out, norm_ml_out, w_out,
     w_router, b_router, w_up, b_up, w_down, b_down) = [
        a[0] for a in (w_ada, b_ada, norm_mix, norm_ffn, w_in, b_gates, sb_bias, norm_sb_out,
                       norm_ml_out, w_out, w_router, b_router, w_up, b_up, w_down, b_down)]

    n_c = bsz + dbs
    n_c_pad = -(-n_c // SUBLANES) * SUBLANES
    c_all = jnp.concatenate([c_prompt, c_sample, jnp.zeros((n_c_pad - n_c, d), F32)], axis=0)
    mod = _ada(c_all, w_ada, b_ada)
    mods_p = [mod[:bsz, i * d:(i + 1) * d].reshape(bsz, 1, d) for i in range(N_MOD)]
    mods_s = [jnp.repeat(mod[bsz:n_c, i * d:(i + 1) * d], dec_seq, axis=0).reshape(1, n_s, d)
              for i in range(N_MOD)]

    wg_pad = w_in[:, w_in.shape[1] - LANES:]
    bg_pad = jnp.pad(b_gates, (GATE_LANE0, 0)).reshape(1, LANES)
    wr_pad = jnp.pad(w_router, ((0, 0), (0, LANES - N_EXPERTS)))
    br_pad = jnp.pad(b_router, (0, LANES - N_EXPERTS)).reshape(1, LANES)

    xp2 = x_prompt.reshape(n_p, d)
    xs2 = x_sample.reshape(n_s, d)
    tm_p = _pick_tile(seq, 1024)

    proj_p, gates_p, k_p, v_p = _inproj(xp2, mods_p[0], mods_p[1], norm_mix, w_in, wg_pad, bg_pad,
                                        tm=tm_p, rows_per_group=seq)
    osb_p = _sb_prompt(proj_p, sb_bias, norm_sb_out, bsz=bsz, seq=seq)
    chunk_p = _pick_tile(seq, ML_CHUNK)
    hml_p, c_p, nn_p, m_p = _mlstm(
        proj_p, gates_p,
        jnp.zeros((bsz, H_ML, DV_ML, DQK_ML), F32), jnp.zeros((bsz, H_ML, DQK_ML), F32),
        jnp.zeros((bsz, H_ML), F32), norm_ml_out,
        bsz=bsz, n_chunks=seq // chunk_p, chunk=chunk_p, valid=chunk_p)

    proj_s, gates_s, k_s, v_s = _inproj(xs2, mods_s[0], mods_s[1], norm_mix, w_in, wg_pad, bg_pad,
                                        tm=n_s, rows_per_group=n_s)
    osb_s = _sb_paged(proj_s[:, :SB_W].reshape(dbs, dec_seq, SB_W),
                      k_s.reshape(dbs, dec_seq, SB_W), v_s.reshape(dbs, dec_seq, SB_W),
                      cache_k[0], cache_v[0], page_table, sb_bias, norm_sb_out)
    osb_s = (osb_s.reshape(dbs, H_SB, dec_seq, D_SB).swapaxes(1, 2)
             .reshape(n_s, SB_W).astype(BF16))
    chunk_s = -(-dec_seq // BF16_SUBLANES) * BF16_SUBLANES
    pad_rows = lambda a: jnp.pad(a.reshape(dbs, dec_seq, a.shape[-1]),
                                 ((0, 0), (0, chunk_s - dec_seq), (0, 0))).reshape(dbs * chunk_s, a.shape[-1])
    hml_s, c_s, nn_s, m_s = _mlstm(
        pad_rows(proj_s), pad_rows(gates_s), state_C[0], state_n[0], state_m[0], norm_ml_out,
        bsz=dbs, n_chunks=1, chunk=chunk_s, valid=dec_seq)
    hml_s = hml_s.reshape(dbs, chunk_s, ML_V_W)[:, :dec_seq].reshape(n_s, ML_V_W)

    tm_o = _pick_tile(seq, 512)
    h_p, xf_p, idx_p, gate_p = _outproj(
        osb_p, hml_p, w_out, xp2, mods_p[2], mods_p[3], mods_p[4], norm_ffn, wr_pad, br_pad,
        tm=tm_o, rows_per_group=seq)
    h_s, xf_s, idx_s, gate_s = _outproj(
        osb_s, hml_s, w_out, xs2, mods_s[2], mods_s[3], mods_s[4], norm_ffn, wr_pad, br_pad,
        tm=n_s, rows_per_group=n_s)
    top_idx = jnp.concatenate([idx_p[:, :TOP_K], idx_s[:, :TOP_K]], axis=0)

    pos, last_tile, n_rows, item_e, item_start, item_tiles = _routing_tables(top_idx)
    x_sorted = _scatter_rows(pos, last_tile, xf_p, xf_s, n_rows)
    y_rows = _moe(item_e, item_start, item_tiles, x_sorted, w_up, b_up, w_down, b_down)
    y_p = _combine(pos, y_rows, h_p, gate_p, mods_p[5], norm_final, tok_off=0, rows_per_group=seq)
    y_s = _combine(pos, y_rows, h_s, gate_s, mods_s[5], norm_final, tok_off=n_p, rows_per_group=n_s)

    return (y_p.reshape(bsz, seq, d), y_s.reshape(dbs, dec_seq, d),
            k_p.reshape(1, bsz, seq, H_SB, D_SB), v_p.reshape(1, bsz, seq, H_SB, D_SB),
            k_s.reshape(1, dbs, dec_seq, H_SB, D_SB), v_s.reshape(1, dbs, dec_seq, H_SB, D_SB),
            c_p[None], nn_p[None], m_p[None], c_s[None], nn_s[None], m_s[None])
```

```python
import functools

import jax
import jax.numpy as jnp
from jax import lax
from jax.experimental import pallas as pl
from jax.experimental.pallas import tpu as pltpu

F32 = jnp.float32
BF16 = jnp.bfloat16
I32 = jnp.int32

H_SB = 8
D_SB = 128
SB_W = H_SB * D_SB
H_ML = 4
DQK_ML = 128
DV_ML = 256
ML_QK_W = H_ML * DQK_ML
ML_V_W = H_ML * DV_ML
PROJ_MAIN_W = 3 * SB_W + 2 * ML_QK_W + 2 * ML_V_W
N_GATES = 2 * H_ML
N_EXPERTS = 32
TOP_K = 4
N_MOD = 6
SWIGLU_LIMIT = 7.0
SWIGLU_ALPHA = 1.702
EPS = 1e-6

LANES = 128
SUBLANES = 8
BF16_SUBLANES = 16
MIB = 1024 * 1024
GATE_LANE0 = LANES - N_GATES

MOE_TILE = 128
MOE_ITEM_TILES = 12
MOE_FF_TILE = 256
ML_CHUNK = 128
SB_BLOCK = 512
SB_SUB = 256
PAGES_PER_STEP = 16
COMBINE_TOK = 64

NT_DIMS = (((1,), (1,)), ((), ()))
TN_DIMS = (((0,), (0,)), ((), ()))


def _cparams(sem, vmem_mib):
    return pltpu.CompilerParams(dimension_semantics=sem, vmem_limit_bytes=vmem_mib * MIB)


def _log_sigmoid_pair(z):
    t = jnp.log(1.0 + jnp.exp(-jnp.abs(z)))
    return jnp.minimum(z, 0.0) - t, -jnp.maximum(z, 0.0) - t


def _split_bf16(x):
    hi = x.astype(BF16)
    lo = (x - hi.astype(F32)).astype(BF16)
    return hi, lo


def _dot3(x, w):
    xh, xl = _split_bf16(x)
    wh, wl = _split_bf16(w)
    return (jnp.dot(xh, wh, preferred_element_type=F32)
            + (jnp.dot(xl, wh, preferred_element_type=F32)
               + jnp.dot(xh, wl, preferred_element_type=F32)))


def _mod_spec(arr, tile, rows_per_group, col_tile=None):
    per_group = arr.shape[1] == 1
    width = arr.shape[2] if col_tile is None else col_tile

    def index_map(i, *rest):
        col = 0 if col_tile is None else rest[0]
        if per_group:
            return ((i * tile) // rows_per_group, 0, col)
        return (0, i, col)

    return pl.BlockSpec((None, 1 if per_group else tile, width), index_map)


def _ada_kernel(c_ref, w_ref, b_ref, o_ref):
    c = c_ref[...]
    o_ref[...] = _dot3(c * jax.nn.sigmoid(c), w_ref[...]) + b_ref[...]


def _ada(c_all, w_ada, b_ada):
    n, d = c_all.shape
    w_out = w_ada.shape[1]
    tn = 1024
    return pl.pallas_call(
        _ada_kernel,
        out_shape=jax.ShapeDtypeStruct((n, w_out), F32),
        grid=(w_out // tn,),
        in_specs=[pl.BlockSpec((n, d), lambda j: (0, 0)),
                  pl.BlockSpec((d, tn), lambda j: (0, j)),
                  pl.BlockSpec((1, tn), lambda j: (0, j))],
        out_specs=pl.BlockSpec((n, tn), lambda j: (0, j)),
        compiler_params=_cparams(("arbitrary",), 40),
        name="ada",
    )(c_all, w_ada, b_ada.reshape(1, w_out))


def _inproj_kernel(x_ref, sh_ref, sc_ref, g_ref, w_ref, wg_ref, bg_ref,
                   proj_ref, gates_ref, k_ref, v_ref, xm_scr, *, tn):
    j = pl.program_id(1)

    @pl.when(j == 0)
    def _():
        x = x_ref[...]
        xn = x * lax.rsqrt(jnp.mean(x * x, axis=-1, keepdims=True) + EPS) * g_ref[...]
        xm = (xn * (1.0 + sc_ref[...]) + sh_ref[...]).astype(BF16)
        xm_scr[...] = xm
        gates_ref[...] = jnp.dot(xm, wg_ref[...].astype(BF16),
                                 preferred_element_type=F32) + bg_ref[...]

    res = jnp.dot(xm_scr[...], w_ref[...].astype(BF16), preferred_element_type=F32)
    proj_ref[...] = res
    k_tiles = SB_W // tn

    @pl.when((j >= k_tiles) & (j < 2 * k_tiles))
    def _():
        k_ref[...] = res

    @pl.when((j >= 2 * k_tiles) & (j < 3 * k_tiles))
    def _():
        v_ref[...] = res


def _inproj(x2d, shift, scale, norm_w, w_in, wg_pad, bg_pad, *, tm, rows_per_group):
    t, d = x2d.shape
    tn = 512
    k_tiles = SB_W // tn
    mod_spec = _mod_spec(shift, tm, rows_per_group)

    def kv_spec(first):
        return pl.BlockSpec((tm, tn), lambda i, j: (i, jnp.clip(j - first, 0, k_tiles - 1)))

    return pl.pallas_call(
        functools.partial(_inproj_kernel, tn=tn),
        out_shape=(jax.ShapeDtypeStruct((t, PROJ_MAIN_W), F32),
                   jax.ShapeDtypeStruct((t, LANES), F32),
                   jax.ShapeDtypeStruct((t, SB_W), F32),
                   jax.ShapeDtypeStruct((t, SB_W), F32)),
        grid=(t // tm, PROJ_MAIN_W // tn),
        in_specs=[pl.BlockSpec((tm, d), lambda i, j: (i, 0)),
                  mod_spec, mod_spec,
                  pl.BlockSpec((1, d), lambda i, j: (0, 0)),
                  pl.BlockSpec((d, tn), lambda i, j: (0, j)),
                  pl.BlockSpec((d, LANES), lambda i, j: (0, 0)),
                  pl.BlockSpec((1, LANES), lambda i, j: (0, 0))],
        out_specs=(pl.BlockSpec((tm, tn), lambda i, j: (i, j)),
                   pl.BlockSpec((tm, LANES), lambda i, j: (i, 0)),
                   kv_spec(k_tiles), kv_spec(2 * k_tiles)),
        scratch_shapes=[pltpu.VMEM((tm, d), BF16)],
        compiler_params=_cparams(("arbitrary", "arbitrary"), 48),
        name="inproj",
    )(x2d, shift, scale, norm_w.reshape(1, d), w_in, wg_pad, bg_pad)


def _sb_strip(z, mask, from_uu, sub, v_bf, spent):
    pos = jnp.maximum(z, 0.0)
    cost = pos + jnp.log(1.0 + jnp.exp(jnp.minimum(z, 0.0) - pos))
    if mask is not None:
        cost = jnp.where(mask, cost, 0.0)
    hi, lo = _split_bf16(cost)
    n_sub = z.shape[1] // sub
    parts = [None] * n_sub
    for j in reversed(range(n_sub)):
        cols = slice(j * sub, (j + 1) * sub)
        from_s = jnp.dot(jnp.concatenate([hi[:, cols], lo[:, cols]], axis=1), from_uu,
                         preferred_element_type=F32)
        parts[j] = jnp.exp(z[:, cols] - (from_s + spent))
        spent = spent + from_s[:, 0:1]
    a = parts[0] if n_sub == 1 else jnp.concatenate(parts, axis=1)
    if mask is not None:
        a = jnp.where(mask, a, 0.0)
    return jnp.dot(a.astype(BF16), v_bf, preferred_element_type=F32), spent


def _from_matrix2(n):
    j = lax.broadcasted_iota(I32, (2 * n, n), 0) % n
    s = lax.broadcasted_iota(I32, (2 * n, n), 1)
    return jnp.where(j >= s, 1.0, 0.0).astype(BF16)


def _sbp_kernel(bias_ref, q_ref, k_ref, v_ref, g_ref, o_ref, o_scr, carry_scr, *, blk, sub, scale):
    h = pl.program_id(1)
    qi = pl.program_id(2)
    bias = bias_ref[h]
    q = q_ref[...].astype(BF16)
    from_uu = _from_matrix2(sub)

    def strip(k_start, width, diagonal):
        k_bf = k_ref[pl.ds(k_start, width), :].astype(BF16)
        v_bf = v_ref[pl.ds(k_start, width), :].astype(BF16)
        z = lax.dot_general(q, k_bf, NT_DIMS, preferred_element_type=F32) * scale + bias
        mask = None
        if diagonal:
            t = lax.broadcasted_iota(I32, (blk, width), 0)
            s = lax.broadcasted_iota(I32, (blk, width), 1)
            mask = s < t
        o_blk, spent = _sb_strip(z, mask, from_uu, sub, v_bf, carry_scr[:, 0:1])
        o_scr[...] += o_blk
        carry_scr[...] = jnp.broadcast_to(spent, carry_scr.shape)

    o_scr[...] = jnp.zeros_like(o_scr)
    carry_scr[...] = jnp.zeros_like(carry_scr)
    strip(pl.multiple_of(qi * blk, blk), blk, True)

    @pl.when(qi % 2 == 1)
    def _():
        strip(pl.multiple_of((qi - 1) * blk, blk), blk, False)

    def older(n, carry):
        strip(pl.multiple_of((qi // 2 - 1 - n) * (2 * blk), 2 * blk), 2 * blk, False)
        return carry

    lax.fori_loop(0, qi // 2, older, 0)
    o = o_scr[...]
    o = o * lax.rsqrt(jnp.mean(o * o, axis=-1, keepdims=True) + EPS) * g_ref[...]
    o_ref[...] = o.astype(o_ref.dtype)


def _sb_prompt(proj, sb_bias, norm_sb_out, *, bsz, seq):
    blk = min(SB_BLOCK, seq)
    nq = seq // blk
    kern = functools.partial(_sbp_kernel, blk=blk, sub=min(SB_SUB, blk), scale=D_SB ** -0.5)
    return pl.pallas_call(
        kern,
        out_shape=jax.ShapeDtypeStruct((bsz * seq, SB_W), BF16),
        grid=(bsz, H_SB, nq),
        in_specs=[pl.BlockSpec(memory_space=pltpu.SMEM),
                  pl.BlockSpec((blk, D_SB), lambda b, h, i: (b * nq + i, h)),
                  pl.BlockSpec((seq, D_SB), lambda b, h, i: (b, H_SB + h)),
                  pl.BlockSpec((seq, D_SB), lambda b, h, i: (b, 2 * H_SB + h)),
                  pl.BlockSpec((None, 1, D_SB), lambda b, h, i: (h, 0, 0))],
        out_specs=pl.BlockSpec((blk, D_SB), lambda b, h, i: (b * nq + i, h)),
        scratch_shapes=[pltpu.VMEM((blk, D_SB), F32), pltpu.VMEM((blk, LANES), F32)],
        compiler_params=_cparams(("arbitrary", "arbitrary", "arbitrary"), 48),
        name="sb_prompt",
    )(sb_bias, proj, proj, proj, norm_sb_out.reshape(H_SB, 1, D_SB))


def _head_block_mask(rows, dec_seq):
    r = lax.broadcasted_iota(I32, (rows, SB_W), 0)
    c = lax.broadcasted_iota(I32, (rows, SB_W), 1)
    return (r // dec_seq) == (c // D_SB)


def _page_heads(page_ref):
    n_keys = page_ref.shape[0] // H_SB
    cols = [page_ref[pl.ds(h, n_keys, stride=H_SB), :] for h in range(H_SB)]
    return jnp.concatenate(cols, axis=1).astype(BF16)


def _sbd_kernel(pt_ref, qbd_ref, kn_ref, vn_ref, bias_ref, g_ref, *rest,
                dec_seq, page, n_steps, scale):
    del pt_ref
    pages = rest[:2 * PAGES_PER_STEP]
    o_ref = rest[2 * PAGES_PER_STEP]
    o_scr, carry_scr, new_scr = rest[2 * PAGES_PER_STEP + 1:]
    rows = H_SB * dec_seq
    j = pl.program_id(1)
    from_uu = _from_matrix2(page)

    def sweep(k_bf, v_bf, mask, spent):
        width = k_bf.shape[0]
        z = (lax.dot_general(qbd_ref[...], k_bf, NT_DIMS, preferred_element_type=F32) * scale
             + bias_ref[:, 0:width])
        return _sb_strip(z, mask, from_uu, page, v_bf, spent)

    @pl.when(j == 0)
    def _():
        new_scr[...] = jnp.zeros_like(new_scr)
        new_scr[0, 0:dec_seq, :] = kn_ref[...]
        new_scr[1, 0:dec_seq, :] = vn_ref[...]
        r = lax.broadcasted_iota(I32, (rows, page), 0)
        s = lax.broadcasted_iota(I32, (rows, page), 1)
        o_new, spent = sweep(new_scr[0].astype(BF16), new_scr[1].astype(BF16), s < (r % dec_seq),
                             jnp.zeros((rows, 1), F32))
        o_scr[...] = o_new
        carry_scr[...] = jnp.broadcast_to(spent, carry_scr.shape)

    order = range(PAGES_PER_STEP - 1, -1, -1)
    k_bf = jnp.concatenate([_page_heads(pages[i]) for i in order], axis=0)
    v_bf = jnp.concatenate([_page_heads(pages[PAGES_PER_STEP + i]) for i in order], axis=0)
    o_blk, spent = sweep(k_bf, v_bf, None, carry_scr[:, 0:1])
    o_scr[...] += o_blk
    carry_scr[...] = jnp.broadcast_to(spent, carry_scr.shape)

    @pl.when(j == n_steps - 1)
    def _():
        o_full = jnp.where(_head_block_mask(rows, dec_seq), o_scr[...], 0.0)
        o = o_full[:, 0:D_SB]
        for h in range(1, H_SB):
            o = o + o_full[:, h * D_SB:(h + 1) * D_SB]
        o = o * lax.rsqrt(jnp.mean(o * o, axis=-1, keepdims=True) + EPS) * g_ref[...]
        o_ref[...] = o


def _sb_paged(q_s, k_new, v_new, cache_k, cache_v, page_table, sb_bias, norm_sb_out):
    bsz, dec_seq, _ = q_s.shape
    n_pool, page = cache_k.shape[0], cache_k.shape[1]
    n_pages = page_table.shape[1]
    assert n_pages % PAGES_PER_STEP == 0
    n_steps = n_pages // PAGES_PER_STEP
    rows = H_SB * dec_seq
    ck = cache_k.reshape(n_pool, page * H_SB, D_SB)
    cv = cache_v.reshape(n_pool, page * H_SB, D_SB)
    strip_w = PAGES_PER_STEP * page
    bias_rows = jnp.broadcast_to(jnp.repeat(sb_bias, dec_seq)[:, None], (rows, strip_w)).astype(F32)
    g_rows = jnp.repeat(norm_sb_out.reshape(H_SB, D_SB), dec_seq, axis=0)
    q_heads = q_s.reshape(bsz, dec_seq, H_SB, D_SB).swapaxes(1, 2)
    eye = jnp.eye(H_SB, dtype=F32)
    qbd = (q_heads[:, :, :, None, :] * eye[None, :, None, :, None]).reshape(bsz, rows, SB_W).astype(BF16)

    def page_spec(i):
        return pl.BlockSpec(
            (None, page * H_SB, D_SB),
            lambda b, j, pt: (pt[b, n_pages - 1 - (j * PAGES_PER_STEP + i)], 0, 0))

    tok_spec = pl.BlockSpec((None, dec_seq, SB_W), lambda b, j, pt: (b, 0, 0))
    kern = functools.partial(_sbd_kernel, dec_seq=dec_seq, page=page, n_steps=n_steps,
                             scale=D_SB ** -0.5)
    grid_spec = pltpu.PrefetchScalarGridSpec(
        num_scalar_prefetch=1,
        grid=(bsz, n_steps),
        in_specs=[pl.BlockSpec((None, rows, SB_W), lambda b, j, pt: (b, 0, 0)),
                  tok_spec, tok_spec,
                  pl.BlockSpec((rows, strip_w), lambda b, j, pt: (0, 0)),
                  pl.BlockSpec((rows, D_SB), lambda b, j, pt: (0, 0))]
                 + [page_spec(i) for i in range(PAGES_PER_STEP)] * 2,
        out_specs=pl.BlockSpec((None, rows, D_SB), lambda b, j, pt: (b, 0, 0)),
        scratch_shapes=[pltpu.VMEM((rows, SB_W), F32),
                        pltpu.VMEM((rows, LANES), F32),
                        pltpu.VMEM((2, page, SB_W), F32)],
    )
    return pl.pallas_call(
        kern,
        out_shape=jax.ShapeDtypeStruct((bsz, rows, D_SB), F32),
        grid_spec=grid_spec,
        compiler_params=_cparams(("arbitrary", "arbitrary"), 56),
        name="sb_paged",
    )(page_table, qbd, k_new, v_new, bias_rows, g_rows,
      *([ck] * PAGES_PER_STEP), *([cv] * PAGES_PER_STEP))


def _mlstm_kernel(q_ref, k_ref, v_ref, og_ref, gcol_ref, grow_ref, c0_ref, n0_ref, m0_ref, gn_ref,
                  h_ref, c_out, n_out, m_out, c_scr, n_scr, m_scr, *, chunk, valid, n_chunks):
    ci = pl.program_id(1)

    @pl.when(ci == 0)
    def _():
        c_scr[...] = c0_ref[...]
        n_scr[...] = n0_ref[...]
        m_scr[...] = m0_ref[...]

    t_idx = lax.broadcasted_iota(I32, (chunk, chunk), 0)
    s_idx = lax.broadcasted_iota(I32, (chunk, chunk), 1)
    causal = s_idx <= t_idx
    col_valid = lax.broadcasted_iota(I32, (chunk, 1), 0) < valid
    row_valid = lax.broadcasted_iota(I32, (1, chunk), 1) < valid
    gcol = gcol_ref[...]
    grow = grow_ref[...]
    neg_inf = -jnp.inf

    def gating(h):
        q = q_ref[:, h * DQK_ML:(h + 1) * DQK_ML]
        ks = k_ref[:, h * DQK_ML:(h + 1) * DQK_ML] * (DQK_ML ** -0.5)
        q_bf, ks_bf = q.astype(BF16), ks.astype(BF16)
        qk = lax.dot_general(q_bf, ks_bf, NT_DIMS, preferred_element_type=F32)

        gi, gf = GATE_LANE0 + h, GATE_LANE0 + H_ML + h
        i_col = jnp.where(col_valid, gcol[:, gi:gi + 1], neg_inf)
        i_row = jnp.where(row_valid, grow[h:h + 1, :], neg_inf)
        lf_col = jnp.where(col_valid, _log_sigmoid_pair(gcol[:, gf:gf + 1])[0], 0.0)
        lf_row = jnp.where(row_valid, _log_sigmoid_pair(grow[H_ML + h:H_ML + h + 1, :])[0], 0.0)
        bcum_col = jnp.sum(jnp.where(causal, lf_row, 0.0), axis=1, keepdims=True)
        bcum_row = jnp.sum(jnp.where(t_idx <= s_idx, lf_col, 0.0), axis=0, keepdims=True)

        log_w = jnp.where(causal, bcum_col - bcum_row + i_row, neg_inf)
        m0 = m_scr[h:h + 1, 0:1]
        m_state = bcum_col + m0
        m_new = jnp.maximum(m_state, jnp.max(log_w, axis=1, keepdims=True))
        s_mat = jnp.exp(log_w - m_new) * qk
        g = jnp.exp(m_state - m_new)
        m_end = m_new[chunk - 1:chunk, :]
        b_last = bcum_col[chunk - 1:chunk, :]
        w_end = jnp.exp(b_last - bcum_col + i_col - m_end)
        g_end = jnp.exp(b_last + m0 - m_end)
        return dict(q=q, ks=ks, q_bf=q_bf, ks_bf=ks_bf, s_mat=s_mat, g=g, m_new=m_new,
                    m_end=m_end, w_end=w_end, g_end=g_end)

    def readout(h, t):
        v = v_ref[:, h * DV_ML:(h + 1) * DV_ML]
        c0 = c_scr[h]
        n0 = n_scr[h:h + 1, :]
        num = (jnp.dot(t["s_mat"].astype(BF16), v.astype(BF16), preferred_element_type=F32)
               + t["g"] * lax.dot_general(t["q_bf"], c0.astype(BF16), NT_DIMS,
                                          preferred_element_type=F32))
        den = (jnp.sum(t["s_mat"], axis=1, keepdims=True)
               + t["g"] * jnp.sum(t["q"] * n0, axis=1, keepdims=True))
        return num / jnp.maximum(jnp.abs(den), jnp.exp(-t["m_new"]))

    def update_state(h, t):
        v = v_ref[:, h * DV_ML:(h + 1) * DV_ML]
        vw_bf = (v * t["w_end"]).astype(BF16)
        c_scr[h] = t["g_end"] * c_scr[h] + lax.dot_general(vw_bf, t["ks_bf"], TN_DIMS,
                                                           preferred_element_type=F32)
        n_scr[h:h + 1, :] = (t["g_end"] * n_scr[h:h + 1, :]
                             + jnp.sum(t["ks"] * t["w_end"], axis=0, keepdims=True))
        m_scr[h:h + 1, :] = jnp.broadcast_to(t["m_end"], (1, LANES))

    def emit(h, hh):
        hn = (hh * lax.rsqrt(jnp.mean(hh * hh, axis=-1, keepdims=True) + EPS)
              * gn_ref[:, h * DV_ML:(h + 1) * DV_ML])
        out = hn * jax.nn.sigmoid(og_ref[:, h * DV_ML:(h + 1) * DV_ML])
        h_ref[:, h * DV_ML:(h + 1) * DV_ML] = out.astype(h_ref.dtype)

    terms = [gating(h) for h in range(H_ML)]
    hhs = [readout(h, terms[h]) for h in range(H_ML)]
    for h in range(H_ML):
        update_state(h, terms[h])
    for h in range(H_ML):
        emit(h, hhs[h])

    @pl.when(ci == n_chunks - 1)
    def _():
        c_out[...] = c_scr[...]
        n_out[...] = n_scr[...]
        m_out[...] = m_scr[...]


def _mlstm(proj, gates, c0, n0, m0, norm_ml_out, *, bsz, n_chunks, chunk, valid):
    rows = bsz * n_chunks * chunk
    grow = gates[:, GATE_LANE0:].reshape(bsz, n_chunks, chunk, N_GATES).swapaxes(2, 3)
    m0_b = jnp.broadcast_to(m0[:, :, None], (bsz, H_ML, LANES)).astype(F32)
    qk_blk = ML_QK_W // LANES
    row_map = lambda b, c: (b * n_chunks + c)
    kern = functools.partial(_mlstm_kernel, chunk=chunk, valid=valid, n_chunks=n_chunks)
    state_specs = [pl.BlockSpec((None, H_ML, DV_ML, DQK_ML), lambda b, c: (b, 0, 0, 0)),
                   pl.BlockSpec((None, H_ML, DQK_ML), lambda b, c: (b, 0, 0)),
                   pl.BlockSpec((None, H_ML, LANES), lambda b, c: (b, 0, 0))]
    q_col = (3 * SB_W) // ML_QK_W
    v_col = (3 * SB_W + 2 * ML_QK_W) // ML_V_W
    h, c1, n1, m1 = pl.pallas_call(
        kern,
        out_shape=(jax.ShapeDtypeStruct((rows, ML_V_W), BF16),
                   jax.ShapeDtypeStruct((bsz, H_ML, DV_ML, DQK_ML), F32),
                   jax.ShapeDtypeStruct((bsz, H_ML, DQK_ML), F32),
                   jax.ShapeDtypeStruct((bsz, H_ML, LANES), F32)),
        grid=(bsz, n_chunks),
        in_specs=[pl.BlockSpec((chunk, ML_QK_W), lambda b, c: (row_map(b, c), q_col)),
                  pl.BlockSpec((chunk, ML_QK_W), lambda b, c: (row_map(b, c), q_col + 1)),
                  pl.BlockSpec((chunk, ML_V_W), lambda b, c: (row_map(b, c), v_col)),
                  pl.BlockSpec((chunk, ML_V_W), lambda b, c: (row_map(b, c), v_col + 1)),
                  pl.BlockSpec((chunk, LANES), lambda b, c: (row_map(b, c), 0)),
                  pl.BlockSpec((None, None, N_GATES, chunk), lambda b, c: (b, c, 0, 0))]
                 + state_specs
                 + [pl.BlockSpec((1, ML_V_W), lambda b, c: (0, 0))],
        out_specs=(pl.BlockSpec((chunk, ML_V_W), lambda b, c: (row_map(b, c), 0)),) + tuple(state_specs),
        scratch_shapes=[pltpu.VMEM((H_ML, DV_ML, DQK_ML), F32),
                        pltpu.VMEM((H_ML, DQK_ML), F32),
                        pltpu.VMEM((H_ML, LANES), F32)],
        compiler_params=_cparams(("arbitrary", "arbitrary"), 40),
        name="mlstm",
    )(proj, proj, proj, proj, gates, grow, c0, n0, m0_b, norm_ml_out.reshape(1, ML_V_W))
    del qk_blk
    return h, c1, n1, m1[:, :, 0]


def _outproj_kernel(osb_ref, hml_ref, w_ref, x_ref, gt_ref, sh_ref, sc_ref, g_ref, wr_ref, br_ref,
                    h_ref, xf_ref, idx_ref, gate_ref, cat_scr, h_scr, *, tn, n_col):
    j = pl.program_id(1)

    @pl.when(j == 0)
    def _():
        cat_scr[:, 0:SB_W] = osb_ref[...]
        cat_scr[:, SB_W:SB_W + ML_V_W] = hml_ref[...]

    mix = jnp.dot(cat_scr[...], w_ref[...].astype(BF16), preferred_element_type=F32)
    h_blk = x_ref[...] + gt_ref[...] * mix
    h_ref[...] = h_blk
    for jj in range(n_col):
        @pl.when(j == jj)
        def _(jj=jj):
            h_scr[:, jj * tn:(jj + 1) * tn] = h_blk

    @pl.when(j == n_col - 1)
    def _():
        hf = h_scr[...]
        xn = hf * lax.rsqrt(jnp.mean(hf * hf, axis=-1, keepdims=True) + EPS) * g_ref[...]
        xf = xn * (1.0 + sc_ref[...]) + sh_ref[...]
        xf_ref[...] = xf
        logits = _dot3(xf, wr_ref[...]) + br_ref[...]
        lane = lax.broadcasted_iota(I32, logits.shape, 1)
        lane_f = lane.astype(F32)
        neg_inf = -jnp.inf
        lg = jnp.where(lane < N_EXPERTS, logits, neg_inf)
        vals, idxs = [], []
        for _k in range(TOP_K):
            mx = jnp.max(lg, axis=1, keepdims=True)
            ix = jnp.min(jnp.where(lg == mx, lane_f, float(LANES)), axis=1, keepdims=True)
            vals.append(mx)
            idxs.append(ix)
            lg = jnp.where(lane_f == ix, neg_inf, lg)
        es = [jnp.exp(vk - vals[0]) for vk in vals]
        tot = es[0] + es[1] + es[2] + es[3]
        idx_out = jnp.zeros(logits.shape, F32)
        gate_out = jnp.zeros(logits.shape, F32)
        for k in range(TOP_K):
            idx_out = jnp.where(lane == k, idxs[k], idx_out)
            gate_out = jnp.where(lane == k, es[k] / tot, gate_out)
        idx_ref[...] = idx_out.astype(I32)
        gate_ref[...] = gate_out


def _outproj(o_sb, h_ml, w_out, x2d, gate_a, shift_f, scale_f, norm_ffn, wr_pad, br_pad,
             *, tm, rows_per_group):
    t, d = x2d.shape
    tn = 512
    n_col = d // tn
    kern = functools.partial(_outproj_kernel, tn=tn, n_col=n_col)
    return pl.pallas_call(
        kern,
        out_shape=(jax.ShapeDtypeStruct((t, d), F32),
                   jax.ShapeDtypeStruct((t, d), F32),
                   jax.ShapeDtypeStruct((t, LANES), I32),
                   jax.ShapeDtypeStruct((t, LANES), F32)),
        grid=(t // tm, n_col),
        in_specs=[pl.BlockSpec((tm, SB_W), lambda i, j: (i, 0)),
                  pl.BlockSpec((tm, ML_V_W), lambda i, j: (i, 0)),
                  pl.BlockSpec((SB_W + ML_V_W, tn), lambda i, j: (0, j)),
                  pl.BlockSpec((tm, tn), lambda i, j: (i, j)),
                  _mod_spec(gate_a, tm, rows_per_group, col_tile=tn),
                  _mod_spec(shift_f, tm, rows_per_group),
                  _mod_spec(scale_f, tm, rows_per_group),
                  pl.BlockSpec((1, d), lambda i, j: (0, 0)),
                  pl.BlockSpec((d, LANES), lambda i, j: (0, 0)),
                  pl.BlockSpec((1, LANES), lambda i, j: (0, 0))],
        out_specs=(pl.BlockSpec((tm, tn), lambda i, j: (i, j)),
                   pl.BlockSpec((tm, d), lambda i, j: (i, 0)),
                   pl.BlockSpec((tm, LANES), lambda i, j: (i, 0)),
                   pl.BlockSpec((tm, LANES), lambda i, j: (i, 0))),
        scratch_shapes=[pltpu.VMEM((tm, SB_W + ML_V_W), BF16),
                        pltpu.VMEM((tm, d), F32)],
        compiler_params=_cparams(("arbitrary", "arbitrary"), 48),
        name="outproj",
    )(o_sb, h_ml, w_out, x2d, gate_a, shift_f, scale_f, norm_ffn.reshape(1, d), wr_pad, br_pad)


def _scatter_kernel(pos_ref, last_ref, xp_ref, xs_ref, o_hbm, sem, *, tok_tile, n_p, n_s):
    i = pl.program_id(0)
    fill_rows = min(MOE_TILE, tok_tile)

    @pl.when(i == 0)
    def _():
        def fill_copy(e, part):
            row = last_ref[e] * MOE_TILE + part * fill_rows
            return pltpu.make_async_copy(xp_ref.at[pl.ds(0, fill_rows), :],
                                         o_hbm.at[pl.ds(row, fill_rows), :], sem)

        def fill(e, carry):
            @pl.when(last_ref[e] >= 0)
            def _():
                for part in range(MOE_TILE // fill_rows):
                    fill_copy(e, part).start()
            return carry

        def drain(e, carry):
            @pl.when(last_ref[e] >= 0)
            def _():
                for part in range(MOE_TILE // fill_rows):
                    fill_copy(e, part).wait()
            return carry

        lax.fori_loop(0, N_EXPERTS, fill, 0)
        lax.fori_loop(0, N_EXPERTS, drain, 0)

    def scatter_block(x_ref, n_rows, base):
        def token(tok, carry):
            for k in range(TOP_K):
                p = pos_ref[base + tok * TOP_K + k]
                pltpu.make_async_copy(x_ref.at[pl.ds(tok, 1), :], o_hbm.at[pl.ds(p, 1), :],
                                      sem).start()
            return carry

        lax.fori_loop(0, n_rows, token, 0, unroll=8)
        done = o_hbm.at[pl.ds(0, n_rows * TOP_K), :]
        pltpu.make_async_copy(done, done, sem).wait()

    @pl.when(i < n_p // tok_tile)
    def _():
        scatter_block(xp_ref, tok_tile, i * tok_tile * TOP_K)

    @pl.when(i == n_p // tok_tile)
    def _():
        scatter_block(xs_ref, n_s, n_p * TOP_K)


def _scatter_rows(pos, last_tile, xf_p, xf_s, n_rows):
    n_p, w = xf_p.shape
    n_s = xf_s.shape[0]
    tok_tile = next(t for t in range(512, 0, -SUBLANES) if n_p % t == 0)
    n_p_steps = n_p // tok_tile
    kern = functools.partial(_scatter_kernel, tok_tile=tok_tile, n_p=n_p, n_s=n_s)
    grid_spec = pltpu.PrefetchScalarGridSpec(
        num_scalar_prefetch=2,
        grid=(n_p_steps + 1,),
        in_specs=[pl.BlockSpec((tok_tile, w), lambda i, p, q: (jnp.minimum(i, n_p_steps - 1), 0)),
                  pl.BlockSpec((n_s, w), lambda i, p, q: (0, 0))],
        out_specs=pl.BlockSpec(memory_space=pl.ANY),
        scratch_shapes=[pltpu.SemaphoreType.DMA],
    )
    return pl.pallas_call(
        kern,
        out_shape=jax.ShapeDtypeStruct((n_rows, w), xf_p.dtype),
        grid_spec=grid_spec,
        compiler_params=_cparams(("arbitrary",), 32),
        name="moe_scatter",
    )(pos, last_tile, xf_p, xf_s)


def _moe_kernel(e_ref, start_ref, tiles_ref, x_hbm, wu_ref, bu_ref, wd_ref, bd_ref, y_hbm,
                x_stage, x_bf, acc, wu_bf, wd_perm, wd_bf, sem_in, sem_out, *, n_ff, d_model, n_items):
    del e_ref
    it = pl.program_id(0)
    f = pl.program_id(1)
    n_tiles = tiles_ref[it]
    row0 = pl.multiple_of(start_ref[it] * MOE_TILE, MOE_TILE)
    half = LANES // 2

    def tile_rows(t):
        return pl.ds(pl.multiple_of(t * MOE_TILE, MOE_TILE), MOE_TILE)

    def x_copy(item, t):
        src0 = pl.multiple_of(start_ref[item] * MOE_TILE, MOE_TILE)
        return pltpu.make_async_copy(x_hbm.at[pl.ds(src0 + t * MOE_TILE, MOE_TILE), :],
                                     x_stage.at[tile_rows(t), :], sem_in)

    def y_copy(t):
        return pltpu.make_async_copy(acc.at[tile_rows(t), :],
                                     y_hbm.at[pl.ds(row0 + t * MOE_TILE, MOE_TILE), :], sem_out)

    def for_tiles(count, fn):
        def body(t, carry):
            fn(t)
            return carry
        lax.fori_loop(0, count, body, 0)

    @pl.when(n_tiles > 0)
    def _():
        @pl.when(f == 0)
        def _():
            @pl.when(it == 0)
            def _():
                for_tiles(n_tiles, lambda t: x_copy(it, t).start())

            for_tiles(n_tiles, lambda t: x_copy(it, t).wait())
            b_dn = jnp.broadcast_to(bd_ref[...], (MOE_TILE, d_model))

            def init(t):
                x_bf[tile_rows(t), :] = x_stage[tile_rows(t), :].astype(BF16)
                acc[tile_rows(t), :] = b_dn

            for_tiles(n_tiles, init)
            nxt = jnp.minimum(it + 1, n_items - 1)
            n_next = jnp.where(it + 1 < n_items, tiles_ref[nxt], 0)
            for_tiles(n_next, lambda t: x_copy(nxt, t).start())

        for c in range(MOE_FF_TILE // LANES):
            for s in range(d_model // LANES):
                for par in range(2):
                    src = wd_ref[c * LANES + par * half:c * LANES + (par + 1) * half,
                                 s * LANES:(s + 1) * LANES]
                    wd_perm[s, pl.ds(c * LANES + par, half, stride=2), :] = src
        for s in range(d_model // LANES):
            wd_bf[:, s * LANES:(s + 1) * LANES] = wd_perm[s].astype(BF16)

        b_up = bu_ref[...]

        def span_rows(t, n):
            return pl.ds(pl.multiple_of(t * MOE_TILE, MOE_TILE), n * MOE_TILE)

        def up_proj(t, n, wu):
            return jnp.dot(x_bf[span_rows(t, n), :], wu,
                           preferred_element_type=F32) + b_up

        def activation(hu):
            even = (lax.broadcasted_iota(I32, (hu.shape[0], LANES), 1) % 2) == 0
            acts = []
            for c in range(MOE_FF_TILE // LANES):
                a_blk = hu[:, (2 * c) * LANES:(2 * c + 1) * LANES]
                b_blk = hu[:, (2 * c + 1) * LANES:(2 * c + 2) * LANES]
                gate = jnp.where(even, a_blk, pltpu.roll(b_blk, 1, 1))
                up = jnp.where(even, pltpu.roll(a_blk, LANES - 1, 1), b_blk)
                gate = jnp.minimum(gate, SWIGLU_LIMIT)
                up = jnp.clip(up, -SWIGLU_LIMIT, SWIGLU_LIMIT)
                acts.append((up + 1.0) * gate * jax.nn.sigmoid(SWIGLU_ALPHA * gate))
            return jnp.concatenate(acts, axis=1).astype(BF16)

        def down_proj(t, n, act):
            acc[span_rows(t, n), :] += jnp.dot(act, wd_bf[...], preferred_element_type=F32)

        def chains(spans):
            wu = wu_ref[...].astype(BF16)
            hus = [up_proj(t, n, wu) for t, n in spans]
            acts = [activation(hu) for hu in hus]
            for (t, n), act in zip(spans, acts):
                down_proj(t, n, act)

        def run_tiles(after_tile):
            def quad(p, carry):
                chains([(4 * p, 2), (4 * p + 2, 2)])
                for u in range(4):
                    after_tile(4 * p + u)
                return carry

            n_quads = n_tiles // 4
            lax.fori_loop(0, n_quads, quad, 0)
            rest = n_tiles - 4 * n_quads

            @pl.when(rest >= 2)
            def _():
                chains([(4 * n_quads, 1), (4 * n_quads + 1, 1)])
                after_tile(4 * n_quads)
                after_tile(4 * n_quads + 1)

            @pl.when(rest % 2 == 1)
            def _():
                chains([(n_tiles - 1, 1)])
                after_tile(n_tiles - 1)

        @pl.when(f < n_ff - 1)
        def _():
            run_tiles(lambda t: None)

        @pl.when(f == n_ff - 1)
        def _():
            run_tiles(lambda t: y_copy(t).start())
            for_tiles(n_tiles, lambda t: y_copy(t).wait())


def _moe(item_e, item_start, item_tiles, x_sorted, w_up, b_up, w_down, b_down):
    n_rows = x_sorted.shape[0]
    n_exp, d, ff2_total = w_up.shape
    d_ff = ff2_total // 2
    n_ff = d_ff // MOE_FF_TILE
    n_items = item_e.shape[0]
    item_rows = MOE_ITEM_TILES * MOE_TILE
    kern = functools.partial(_moe_kernel, n_ff=n_ff, d_model=d, n_items=n_items)

    def ff_tile(i, f, n):
        return jnp.where(n[i] > 0, f, n_ff - 1)

    grid_spec = pltpu.PrefetchScalarGridSpec(
        num_scalar_prefetch=3,
        grid=(n_items, n_ff),
        in_specs=[pl.BlockSpec(memory_space=pl.ANY),
                  pl.BlockSpec((None, d, 2 * MOE_FF_TILE),
                               lambda i, f, e, s, n: (e[i], 0, ff_tile(i, f, n))),
                  pl.BlockSpec((None, 1, 2 * MOE_FF_TILE),
                               lambda i, f, e, s, n: (e[i], 0, ff_tile(i, f, n))),
                  pl.BlockSpec((None, MOE_FF_TILE, d),
                               lambda i, f, e, s, n: (e[i], ff_tile(i, f, n), 0)),
                  pl.BlockSpec((None, 1, d), lambda i, f, e, s, n: (e[i], 0, 0))],
        out_specs=pl.BlockSpec(memory_space=pl.ANY),
        scratch_shapes=[pltpu.VMEM((item_rows, d), F32),
                        pltpu.VMEM((item_rows, d), BF16),
                        pltpu.VMEM((item_rows, d), F32),
                        pltpu.VMEM((d, 2 * MOE_FF_TILE), BF16),
                        pltpu.VMEM((d // LANES, MOE_FF_TILE, LANES), F32),
                        pltpu.VMEM((MOE_FF_TILE, d), BF16),
                        pltpu.SemaphoreType.DMA,
                        pltpu.SemaphoreType.DMA],
    )
    return pl.pallas_call(
        kern,
        out_shape=jax.ShapeDtypeStruct((n_rows, d), F32),
        grid_spec=grid_spec,
        compiler_params=_cparams(("arbitrary", "arbitrary"), 56),
        name="moe_experts",
    )(item_e, item_start, item_tiles, x_sorted, w_up,
      b_up.reshape(n_exp, 1, ff2_total), w_down, b_down.reshape(n_exp, 1, d))


def _combine_kernel(pos_ref, y_hbm, h_ref, gate_ref, gt_ref, g_ref, o_ref, buf, sem, *, tok_off):
    i = pl.program_id(0)

    def issue(step, slot):
        base = (step * COMBINE_TOK + tok_off) * TOP_K

        def token(tok, carry):
            for k in range(TOP_K):
                p = pos_ref[base + tok * TOP_K + k]
                pltpu.make_async_copy(y_hbm.at[pl.ds(p, 1), :],
                                      buf.at[slot, pl.ds(k * COMBINE_TOK + tok, 1), :],
                                      sem.at[slot]).start()
            return carry

        lax.fori_loop(0, COMBINE_TOK, token, 0, unroll=8)

    @pl.when(i == 0)
    def _():
        issue(0, 0)

    @pl.when(i + 1 < pl.num_programs(0))
    def _():
        issue(i + 1, (i + 1) % 2)

    slot = i % 2
    pltpu.make_async_copy(y_hbm.at[pl.ds(0, COMBINE_TOK * TOP_K), :], buf.at[slot],
                          sem.at[slot]).wait()
    gates = gate_ref[...]
    ffn = jnp.zeros(h_ref.shape, F32)
    for k in range(TOP_K):
        ffn = ffn + gates[:, k:k + 1] * buf[slot, k * COMBINE_TOK:(k + 1) * COMBINE_TOK, :]
    y = h_ref[...] + gt_ref[...] * ffn
    o_ref[...] = y * lax.rsqrt(jnp.mean(y * y, axis=-1, keepdims=True) + EPS) * g_ref[...]


def _combine(pos, y_rows, h, gate, gate_f, norm_final, *, tok_off, rows_per_group):
    n_tok, d = h.shape
    kern = functools.partial(_combine_kernel, tok_off=tok_off)
    grid_spec = pltpu.PrefetchScalarGridSpec(
        num_scalar_prefetch=1,
        grid=(n_tok // COMBINE_TOK,),
        in_specs=[pl.BlockSpec(memory_space=pl.ANY),
                  pl.BlockSpec((COMBINE_TOK, d), lambda i, p: (i, 0)),
                  pl.BlockSpec((COMBINE_TOK, LANES), lambda i, p: (i, 0)),
                  _mod_spec(gate_f, COMBINE_TOK, rows_per_group),
                  pl.BlockSpec((1, d), lambda i, p: (0, 0))],
        out_specs=pl.BlockSpec((COMBINE_TOK, d), lambda i, p: (i, 0)),
        scratch_shapes=[pltpu.VMEM((2, COMBINE_TOK * TOP_K, d), F32),
                        pltpu.SemaphoreType.DMA((2,))],
    )
    return pl.pallas_call(
        kern,
        out_shape=jax.ShapeDtypeStruct((n_tok, d), F32),
        grid_spec=grid_spec,
        compiler_params=_cparams(("arbitrary",), 32),
        name="moe_combine",
    )(pos, y_rows, h, gate, gate_f, norm_final.reshape(1, d))


def _routing_tables(top_idx):
    n_tok = top_idx.shape[0]
    n_assign = n_tok * TOP_K
    flat_e = top_idx.reshape(-1)
    onehot = (flat_e[:, None] == jnp.arange(N_EXPERTS, dtype=I32)[None, :]).astype(I32)
    running = jnp.cumsum(onehot, axis=0)
    rank = jnp.take_along_axis(running, flat_e[:, None], axis=1)[:, 0] - 1
    counts = running[-1]
    tiles_e = (counts + MOE_TILE - 1) // MOE_TILE
    tile_start_e = jnp.cumsum(tiles_e) - tiles_e
    pos = tile_start_e[flat_e] * MOE_TILE + rank

    n_tiles_max = -(-n_assign // MOE_TILE) + N_EXPERTS
    n_rows = n_tiles_max * MOE_TILE
    last_tile = jnp.where(tiles_e > 0, tile_start_e + tiles_e - 1, -1)

    n_items_max = (n_tiles_max + N_EXPERTS * (MOE_ITEM_TILES - 1)) // MOE_ITEM_TILES
    items_e = (tiles_e + MOE_ITEM_TILES - 1) // MOE_ITEM_TILES
    items_end = jnp.cumsum(items_e)
    it = jnp.arange(n_items_max, dtype=I32)
    e_of = jnp.minimum(jnp.sum((it[:, None] >= items_end[None, :]).astype(I32), axis=1), N_EXPERTS - 1)
    local = it - (items_end[e_of] - items_e[e_of])
    valid = it < items_end[-1]
    tiles_left = tiles_e[e_of] - local * MOE_ITEM_TILES
    item_tiles = jnp.where(valid, jnp.clip(tiles_left, 0, MOE_ITEM_TILES), 0).astype(I32)
    item_start = jnp.where(valid, tile_start_e[e_of] + local * MOE_ITEM_TILES, 0).astype(I32)
    last_e = e_of[jnp.maximum(items_end[-1] - 1, 0)]
    item_e = jnp.where(valid, e_of, last_e).astype(I32)
    return pos.astype(I32), last_tile.astype(I32), n_rows, item_e, item_start, item_tiles


def _pick_tile(n, pref):
    t = min(n, pref)
    while n % t:
        t //= 2
    return t


def kernel(x_prompt, x_sample, c_prompt, c_sample, cache_k, cache_v, page_table, state_C, state_n, state_m, w_ada, b_ada, norm_mix, norm_ffn, w_in, b_gates, sb_bias, norm_sb_out, norm_ml_out, w_out, w_router, b_router, w_up, b_up, w_down, b_down, norm_final):
    bsz, seq, d = x_prompt.shape
    dbs, dec_seq, _ = x_sample.shape
    depth = w_ada.shape[0]
    assert depth == 1
    n_p, n_s = bsz * seq, dbs * dec_seq
    n_tok = n_p + n_s

    (w_ada, b_ada, norm_mix, norm_ffn, w_in, b_gates, sb_bias, norm_sb_out, norm_ml_out, w_out,
     w_router, b_router, w_up, b_up, w_down, b_down) = [
        a[0] for a in (w_ada, b_ada, norm_mix, norm_ffn, w_in, b_gates, sb_bias, norm_sb_out,
                       norm_ml_out, w_out, w_router, b_router, w_up, b_up, w_down, b_down)]

    n_c = bsz + dbs
    n_c_pad = -(-n_c // SUBLANES) * SUBLANES
    c_all = jnp.concatenate([c_prompt, c_sample, jnp.zeros((n_c_pad - n_c, d), F32)], axis=0)
    mod = _ada(c_all, w_ada, b_ada)
    mods_p = [mod[:bsz, i * d:(i + 1) * d].reshape(bsz, 1, d) for i in range(N_MOD)]
    mods_s = [jnp.repeat(mod[bsz:n_c, i * d:(i + 1) * d], dec_seq, axis=0).reshape(1, n_s, d)
              for i in range(N_MOD)]

    wg_pad = w_in[:, w_in.shape[1] - LANES:]
    bg_pad = jnp.pad(b_gates, (GATE_LANE0, 0)).reshape(1, LANES)
    wr_pad = jnp.pad(w_router, ((0, 0), (0, LANES - N_EXPERTS)))
    br_pad = jnp.pad(b_router, (0, LANES - N_EXPERTS)).reshape(1, LANES)

    xp2 = x_prompt.reshape(n_p, d)
    xs2 = x_sample.reshape(n_s, d)
    tm_p = _pick_tile(seq, 1024)

    proj_p, gates_p, k_p, v_p = _inproj(xp2, mods_p[0], mods_p[1], norm_mix, w_in, wg_pad, bg_pad,
                                        tm=tm_p, rows_per_group=seq)
    osb_p = _sb_prompt(proj_p, sb_bias, norm_sb_out, bsz=bsz, seq=seq)
    chunk_p = _pick_tile(seq, ML_CHUNK)
    hml_p, c_p, nn_p, m_p = _mlstm(
        proj_p, gates_p,
        jnp.zeros((bsz, H_ML, DV_ML, DQK_ML), F32), jnp.zeros((bsz, H_ML, DQK_ML), F32),
        jnp.zeros((bsz, H_ML), F32), norm_ml_out,
        bsz=bsz, n_chunks=seq // chunk_p, chunk=chunk_p, valid=chunk_p)

    proj_s, gates_s, k_s, v_s = _inproj(xs2, mods_s[0], mods_s[1], norm_mix, w_in, wg_pad, bg_pad,
                                        tm=n_s, rows_per_group=n_s)
    osb_s = _sb_paged(proj_s[:, :SB_W].reshape(dbs, dec_seq, SB_W),
                      k_s.reshape(dbs, dec_seq, SB_W), v_s.reshape(dbs, dec_seq, SB_W),
                      cache_k[0], cache_v[0], page_table, sb_bias, norm_sb_out)
    osb_s = (osb_s.reshape(dbs, H_SB, dec_seq, D_SB).swapaxes(1, 2)
             .reshape(n_s, SB_W).astype(BF16))
    chunk_s = -(-dec_seq // BF16_SUBLANES) * BF16_SUBLANES
    pad_rows = lambda a: jnp.pad(a.reshape(dbs, dec_seq, a.shape[-1]),
                                 ((0, 0), (0, chunk_s - dec_seq), (0, 0))).reshape(dbs * chunk_s, a.shape[-1])
    hml_s, c_s, nn_s, m_s = _mlstm(
        pad_rows(proj_s), pad_rows(gates_s), state_C[0], state_n[0], state_m[0], norm_ml_out,
        bsz=dbs, n_chunks=1, chunk=chunk_s, valid=dec_seq)
    hml_s = hml_s.reshape(dbs, chunk_s, ML_V_W)[:, :dec_seq].reshape(n_s, ML_V_W)

    tm_o = _pick_tile(seq, 512)
    h_p, xf_p, idx_p, gate_p = _outproj(
        osb_p, hml_p, w_out, xp2, mods_p[2], mods_p[3], mods_p[4], norm_ffn, wr_pad, br_pad,
        tm=tm_o, rows_per_group=seq)
    h_s, xf_s, idx_s, gate_s = _outproj(
        osb_s, hml_s, w_out, xs2, mods_s[2], mods_s[3], mods_s[4], norm_ffn, wr_pad, br_pad,
        tm=n_s, rows_per_group=n_s)
    top_idx = jnp.concatenate([idx_p[:, :TOP_K], idx_s[:, :TOP_K]], axis=0)

    pos, last_tile, n_rows, item_e, item_start, item_tiles = _routing_tables(top_idx)
    x_sorted = _scatter_rows(pos, last_tile, xf_p, xf_s, n_rows)
    y_rows = _moe(item_e, item_start, item_tiles, x_sorted, w_up, b_up, w_down, b_down)
    y_p = _combine(pos, y_rows, h_p, gate_p, mods_p[5], norm_final, tok_off=0, rows_per_group=seq)
    y_s = _combine(pos, y_rows, h_s, gate_s, mods_s[5], norm_final, tok_off=n_p, rows_per_group=n_s)

    return (y_p.reshape(bsz, seq, d), y_s.reshape(dbs, dec_seq, d),
            k_p.reshape(1, bsz, seq, H_SB, D_SB), v_p.reshape(1, bsz, seq, H_SB, D_SB),
            k_s.reshape(1, dbs, dec_seq, H_SB, D_SB), v_s.reshape(1, dbs, dec_seq, H_SB, D_SB),
            c_p[None], nn_p[None], m_p[None], c_s[None], nn_s[None], m_s[None])
```

```python
import functools

import jax
import jax.numpy as jnp
from jax import lax
from jax.experimental import pallas as pl
from jax.experimental.pallas import tpu as pltpu

F32 = jnp.float32
BF16 = jnp.bfloat16
I32 = jnp.int32

H_SB = 8
D_SB = 128
SB_W = H_SB * D_SB
H_ML = 4
DQK_ML = 128
DV_ML = 256
ML_QK_W = H_ML * DQK_ML
ML_V_W = H_ML * DV_ML
PROJ_MAIN_W = 3 * SB_W + 2 * ML_QK_W + 2 * ML_V_W
N_GATES = 2 * H_ML
N_EXPERTS = 32
TOP_K = 4
N_MOD = 6
SWIGLU_LIMIT = 7.0
SWIGLU_ALPHA = 1.702
EPS = 1e-6

LANES = 128
SUBLANES = 8
BF16_SUBLANES = 16
MIB = 1024 * 1024
GATE_LANE0 = LANES - N_GATES

MOE_TILE = 128
MOE_ITEM_TILES = 12
MOE_FF_TILE = 256
ML_CHUNK = 128
SB_BLOCK = 512
SB_SUB = 256
PAGES_PER_STEP = 16
COMBINE_TOK = 64

NT_DIMS = (((1,), (1,)), ((), ()))
TN_DIMS = (((0,), (0,)), ((), ()))


def _cparams(sem, vmem_mib):
    return pltpu.CompilerParams(dimension_semantics=sem, vmem_limit_bytes=vmem_mib * MIB)


def _log_sigmoid_pair(z):
    t = jnp.log(1.0 + jnp.exp(-jnp.abs(z)))
    return jnp.minimum(z, 0.0) - t, -jnp.maximum(z, 0.0) - t


def _split_bf16(x):
    hi = x.astype(BF16)
    lo = (x - hi.astype(F32)).astype(BF16)
    return hi, lo


def _dot3(x, w):
    xh, xl = _split_bf16(x)
    wh, wl = _split_bf16(w)
    return (jnp.dot(xh, wh, preferred_element_type=F32)
            + (jnp.dot(xl, wh, preferred_element_type=F32)
               + jnp.dot(xh, wl, preferred_element_type=F32)))


def _mod_spec(arr, tile, rows_per_group, col_tile=None):
    per_group = arr.shape[1] == 1
    width = arr.shape[2] if col_tile is None else col_tile

    def index_map(i, *rest):
        col = 0 if col_tile is None else rest[0]
        if per_group:
            return ((i * tile) // rows_per_group, 0, col)
        return (0, i, col)

    return pl.BlockSpec((None, 1 if per_group else tile, width), index_map)


def _ada_kernel(c_ref, w_ref, b_ref, o_ref):
    c = c_ref[...]
    o_ref[...] = _dot3(c * jax.nn.sigmoid(c), w_ref[...]) + b_ref[...]


def _ada(c_all, w_ada, b_ada):
    n, d = c_all.shape
    w_out = w_ada.shape[1]
    tn = 1024
    return pl.pallas_call(
        _ada_kernel,
        out_shape=jax.ShapeDtypeStruct((n, w_out), F32),
        grid=(w_out // tn,),
        in_specs=[pl.BlockSpec((n, d), lambda j: (0, 0)),
                  pl.BlockSpec((d, tn), lambda j: (0, j)),
                  pl.BlockSpec((1, tn), lambda j: (0, j))],
        out_specs=pl.BlockSpec((n, tn), lambda j: (0, j)),
        compiler_params=_cparams(("arbitrary",), 40),
        name="ada",
    )(c_all, w_ada, b_ada.reshape(1, w_out))


def _inproj_kernel(x_ref, sh_ref, sc_ref, g_ref, w_ref, wg_ref, bg_ref,
                   proj_ref, gates_ref, k_ref, v_ref, xm_scr, *, tn):
    j = pl.program_id(1)

    @pl.when(j == 0)
    def _():
        x = x_ref[...]
        xn = x * lax.rsqrt(jnp.mean(x * x, axis=-1, keepdims=True) + EPS) * g_ref[...]
        xm = (xn * (1.0 + sc_ref[...]) + sh_ref[...]).astype(BF16)
        xm_scr[...] = xm
        gates_ref[...] = lax.dot_general(xm, wg_ref[...].astype(BF16), NT_DIMS,
                                         preferred_element_type=F32) + bg_ref[...]

    res = lax.dot_general(xm_scr[...], w_ref[...].astype(BF16), NT_DIMS,
                          preferred_element_type=F32)
    proj_ref[...] = res
    k_tiles = SB_W // tn

    @pl.when((j >= k_tiles) & (j < 2 * k_tiles))
    def _():
        k_ref[...] = res

    @pl.when((j >= 2 * k_tiles) & (j < 3 * k_tiles))
    def _():
        v_ref[...] = res


def _inproj(x2d, shift, scale, norm_w, w_in, wg_pad, bg_pad, *, tm, rows_per_group):
    t, d = x2d.shape
    tn = 512
    k_tiles = SB_W // tn
    mod_spec = _mod_spec(shift, tm, rows_per_group)

    def kv_spec(first):
        return pl.BlockSpec((tm, tn), lambda i, j: (i, jnp.clip(j - first, 0, k_tiles - 1)))

    return pl.pallas_call(
        functools.partial(_inproj_kernel, tn=tn),
        out_shape=(jax.ShapeDtypeStruct((t, PROJ_MAIN_W), F32),
                   jax.ShapeDtypeStruct((t, LANES), F32),
                   jax.ShapeDtypeStruct((t, SB_W), F32),
                   jax.ShapeDtypeStruct((t, SB_W), F32)),
        grid=(t // tm, PROJ_MAIN_W // tn),
        in_specs=[pl.BlockSpec((tm, d), lambda i, j: (i, 0)),
                  mod_spec, mod_spec,
                  pl.BlockSpec((1, d), lambda i, j: (0, 0)),
                  pl.BlockSpec((tn, d), lambda i, j: (j, 0)),
                  pl.BlockSpec((LANES, d), lambda i, j: (0, 0)),
                  pl.BlockSpec((1, LANES), lambda i, j: (0, 0))],
        out_specs=(pl.BlockSpec((tm, tn), lambda i, j: (i, j)),
                   pl.BlockSpec((tm, LANES), lambda i, j: (i, 0)),
                   kv_spec(k_tiles), kv_spec(2 * k_tiles)),
        scratch_shapes=[pltpu.VMEM((tm, d), BF16)],
        compiler_params=_cparams(("arbitrary", "arbitrary"), 48),
        name="inproj",
    )(x2d, shift, scale, norm_w.reshape(1, d), w_in, wg_pad, bg_pad)


def _sb_strip(z, mask, from_uu, sub, v_bf, spent):
    pos = jnp.maximum(z, 0.0)
    cost = pos + jnp.log(1.0 + jnp.exp(jnp.minimum(z, 0.0) - pos))
    if mask is not None:
        cost = jnp.where(mask, cost, 0.0)
    hi, lo = _split_bf16(cost)
    n_sub = z.shape[1] // sub
    parts = [None] * n_sub
    for j in reversed(range(n_sub)):
        cols = slice(j * sub, (j + 1) * sub)
        from_s = jnp.dot(jnp.concatenate([hi[:, cols], lo[:, cols]], axis=1), from_uu,
                         preferred_element_type=F32)
        parts[j] = jnp.exp(z[:, cols] - (from_s + spent))
        spent = spent + from_s[:, 0:1]
    a = parts[0] if n_sub == 1 else jnp.concatenate(parts, axis=1)
    if mask is not None:
        a = jnp.where(mask, a, 0.0)
    return jnp.dot(a.astype(BF16), v_bf, preferred_element_type=F32), spent


def _from_matrix2(n):
    j = lax.broadcasted_iota(I32, (2 * n, n), 0) % n
    s = lax.broadcasted_iota(I32, (2 * n, n), 1)
    return jnp.where(j >= s, 1.0, 0.0).astype(BF16)


def _sbp_kernel(bias_ref, q_ref, k_ref, v_ref, g_ref, o_ref, o_scr, carry_scr, *, blk, sub, scale):
    h = pl.program_id(1)
    qi = pl.program_id(2)
    bias = bias_ref[h]
    q = q_ref[...].astype(BF16)
    from_uu = _from_matrix2(sub)

    def strip(k_start, width, diagonal):
        k_bf = k_ref[pl.ds(k_start, width), :].astype(BF16)
        v_bf = v_ref[pl.ds(k_start, width), :].astype(BF16)
        z = lax.dot_general(q, k_bf, NT_DIMS, preferred_element_type=F32) * scale + bias
        mask = None
        if diagonal:
            t = lax.broadcasted_iota(I32, (blk, width), 0)
            s = lax.broadcasted_iota(I32, (blk, width), 1)
            mask = s < t
        o_blk, spent = _sb_strip(z, mask, from_uu, sub, v_bf, carry_scr[:, 0:1])
        o_scr[...] += o_blk
        carry_scr[...] = jnp.broadcast_to(spent, carry_scr.shape)

    o_scr[...] = jnp.zeros_like(o_scr)
    carry_scr[...] = jnp.zeros_like(carry_scr)
    strip(pl.multiple_of(qi * blk, blk), blk, True)

    @pl.when(qi % 2 == 1)
    def _():
        strip(pl.multiple_of((qi - 1) * blk, blk), blk, False)

    def older(n, carry):
        strip(pl.multiple_of((qi // 2 - 1 - n) * (2 * blk), 2 * blk), 2 * blk, False)
        return carry

    lax.fori_loop(0, qi // 2, older, 0)
    o = o_scr[...]
    o = o * lax.rsqrt(jnp.mean(o * o, axis=-1, keepdims=True) + EPS) * g_ref[...]
    o_ref[...] = o.astype(o_ref.dtype)


def _sb_prompt(proj, sb_bias, norm_sb_out, *, bsz, seq):
    blk = min(SB_BLOCK, seq)
    nq = seq // blk
    kern = functools.partial(_sbp_kernel, blk=blk, sub=min(SB_SUB, blk), scale=D_SB ** -0.5)
    return pl.pallas_call(
        kern,
        out_shape=jax.ShapeDtypeStruct((bsz * seq, SB_W), BF16),
        grid=(bsz, H_SB, nq),
        in_specs=[pl.BlockSpec(memory_space=pltpu.SMEM),
                  pl.BlockSpec((blk, D_SB), lambda b, h, i: (b * nq + i, h)),
                  pl.BlockSpec((seq, D_SB), lambda b, h, i: (b, H_SB + h)),
                  pl.BlockSpec((seq, D_SB), lambda b, h, i: (b, 2 * H_SB + h)),
                  pl.BlockSpec((None, 1, D_SB), lambda b, h, i: (h, 0, 0))],
        out_specs=pl.BlockSpec((blk, D_SB), lambda b, h, i: (b * nq + i, h)),
        scratch_shapes=[pltpu.VMEM((blk, D_SB), F32), pltpu.VMEM((blk, LANES), F32)],
        compiler_params=_cparams(("arbitrary", "arbitrary", "arbitrary"), 48),
        name="sb_prompt",
    )(sb_bias, proj, proj, proj, norm_sb_out.reshape(H_SB, 1, D_SB))


def _head_block_mask(rows, dec_seq):
    r = lax.broadcasted_iota(I32, (rows, SB_W), 0)
    c = lax.broadcasted_iota(I32, (rows, SB_W), 1)
    return (r // dec_seq) == (c // D_SB)


def _page_heads(page_ref):
    n_keys = page_ref.shape[0] // H_SB
    cols = [page_ref[pl.ds(h, n_keys, stride=H_SB), :] for h in range(H_SB)]
    return jnp.concatenate(cols, axis=1).astype(BF16)


def _sbd_kernel(pt_ref, qbd_ref, kn_ref, vn_ref, bias_ref, g_ref, *rest,
                dec_seq, page, n_steps, scale):
    del pt_ref
    pages = rest[:2 * PAGES_PER_STEP]
    o_ref = rest[2 * PAGES_PER_STEP]
    o_scr, carry_scr, new_scr = rest[2 * PAGES_PER_STEP + 1:]
    rows = H_SB * dec_seq
    j = pl.program_id(1)
    from_uu = _from_matrix2(page)

    def sweep(k_bf, v_bf, mask, spent):
        width = k_bf.shape[0]
        z = (lax.dot_general(qbd_ref[...], k_bf, NT_DIMS, preferred_element_type=F32) * scale
             + bias_ref[:, 0:width])
        return _sb_strip(z, mask, from_uu, page, v_bf, spent)

    @pl.when(j == 0)
    def _():
        new_scr[...] = jnp.zeros_like(new_scr)
        new_scr[0, 0:dec_seq, :] = kn_ref[...]
        new_scr[1, 0:dec_seq, :] = vn_ref[...]
        r = lax.broadcasted_iota(I32, (rows, page), 0)
        s = lax.broadcasted_iota(I32, (rows, page), 1)
        o_new, spent = sweep(new_scr[0].astype(BF16), new_scr[1].astype(BF16), s < (r % dec_seq),
                             jnp.zeros((rows, 1), F32))
        o_scr[...] = o_new
        carry_scr[...] = jnp.broadcast_to(spent, carry_scr.shape)

    order = range(PAGES_PER_STEP - 1, -1, -1)
    k_bf = jnp.concatenate([_page_heads(pages[i]) for i in order], axis=0)
    v_bf = jnp.concatenate([_page_heads(pages[PAGES_PER_STEP + i]) for i in order], axis=0)
    o_blk, spent = sweep(k_bf, v_bf, None, carry_scr[:, 0:1])
    o_scr[...] += o_blk
    carry_scr[...] = jnp.broadcast_to(spent, carry_scr.shape)

    @pl.when(j == n_steps - 1)
    def _():
        o_full = jnp.where(_head_block_mask(rows, dec_seq), o_scr[...], 0.0)
        o = o_full[:, 0:D_SB]
        for h in range(1, H_SB):
            o = o + o_full[:, h * D_SB:(h + 1) * D_SB]
        o = o * lax.rsqrt(jnp.mean(o * o, axis=-1, keepdims=True) + EPS) * g_ref[...]
        o_ref[...] = o


def _sb_paged(q_s, k_new, v_new, cache_k, cache_v, page_table, sb_bias, norm_sb_out):
    bsz, dec_seq, _ = q_s.shape
    n_pool, page = cache_k.shape[0], cache_k.shape[1]
    n_pages = page_table.shape[1]
    assert n_pages % PAGES_PER_STEP == 0
    n_steps = n_pages // PAGES_PER_STEP
    rows = H_SB * dec_seq
    ck = cache_k.reshape(n_pool, page * H_SB, D_SB)
    cv = cache_v.reshape(n_pool, page * H_SB, D_SB)
    strip_w = PAGES_PER_STEP * page
    bias_rows = jnp.broadcast_to(jnp.repeat(sb_bias, dec_seq)[:, None], (rows, strip_w)).astype(F32)
    g_rows = jnp.repeat(norm_sb_out.reshape(H_SB, D_SB), dec_seq, axis=0)
    q_heads = q_s.reshape(bsz, dec_seq, H_SB, D_SB).swapaxes(1, 2)
    eye = jnp.eye(H_SB, dtype=F32)
    qbd = (q_heads[:, :, :, None, :] * eye[None, :, None, :, None]).reshape(bsz, rows, SB_W).astype(BF16)

    def page_spec(i):
        return pl.BlockSpec(
            (None, page * H_SB, D_SB),
            lambda b, j, pt: (pt[b, n_pages - 1 - (j * PAGES_PER_STEP + i)], 0, 0))

    tok_spec = pl.BlockSpec((None, dec_seq, SB_W), lambda b, j, pt: (b, 0, 0))
    kern = functools.partial(_sbd_kernel, dec_seq=dec_seq, page=page, n_steps=n_steps,
                             scale=D_SB ** -0.5)
    grid_spec = pltpu.PrefetchScalarGridSpec(
        num_scalar_prefetch=1,
        grid=(bsz, n_steps),
        in_specs=[pl.BlockSpec((None, rows, SB_W), lambda b, j, pt: (b, 0, 0)),
                  tok_spec, tok_spec,
                  pl.BlockSpec((rows, strip_w), lambda b, j, pt: (0, 0)),
                  pl.BlockSpec((rows, D_SB), lambda b, j, pt: (0, 0))]
                 + [page_spec(i) for i in range(PAGES_PER_STEP)] * 2,
        out_specs=pl.BlockSpec((None, rows, D_SB), lambda b, j, pt: (b, 0, 0)),
        scratch_shapes=[pltpu.VMEM((rows, SB_W), F32),
                        pltpu.VMEM((rows, LANES), F32),
                        pltpu.VMEM((2, page, SB_W), F32)],
    )
    return pl.pallas_call(
        kern,
        out_shape=jax.ShapeDtypeStruct((bsz, rows, D_SB), F32),
        grid_spec=grid_spec,
        compiler_params=_cparams(("arbitrary", "arbitrary"), 56),
        name="sb_paged",
    )(page_table, qbd, k_new, v_new, bias_rows, g_rows,
      *([ck] * PAGES_PER_STEP), *([cv] * PAGES_PER_STEP))


def _mlstm_kernel(q_ref, k_ref, v_ref, og_ref, gcol_ref, grow_ref, c0_ref, n0_ref, m0_ref, gn_ref,
                  h_ref, c_out, n_out, m_out, c_scr, n_scr, m_scr, *, chunk, valid, n_chunks):
    ci = pl.program_id(1)

    @pl.when(ci == 0)
    def _():
        c_scr[...] = c0_ref[...]
        n_scr[...] = n0_ref[...]
        m_scr[...] = m0_ref[...]

    t_idx = lax.broadcasted_iota(I32, (chunk, chunk), 0)
    s_idx = lax.broadcasted_iota(I32, (chunk, chunk), 1)
    causal = s_idx <= t_idx
    col_valid = lax.broadcasted_iota(I32, (chunk, 1), 0) < valid
    row_valid = lax.broadcasted_iota(I32, (1, chunk), 1) < valid
    gcol = gcol_ref[...]
    grow = grow_ref[...]
    neg_inf = -jnp.inf

    def gating(h):
        q = q_ref[:, h * DQK_ML:(h + 1) * DQK_ML]
        ks = k_ref[:, h * DQK_ML:(h + 1) * DQK_ML] * (DQK_ML ** -0.5)
        q_bf, ks_bf = q.astype(BF16), ks.astype(BF16)
        qk = lax.dot_general(q_bf, ks_bf, NT_DIMS, preferred_element_type=F32)

        gi, gf = GATE_LANE0 + h, GATE_LANE0 + H_ML + h
        i_col = jnp.where(col_valid, gcol[:, gi:gi + 1], neg_inf)
        i_row = jnp.where(row_valid, grow[h:h + 1, :], neg_inf)
        lf_col = jnp.where(col_valid, _log_sigmoid_pair(gcol[:, gf:gf + 1])[0], 0.0)
        lf_row = jnp.where(row_valid, _log_sigmoid_pair(grow[H_ML + h:H_ML + h + 1, :])[0], 0.0)
        bcum_col = jnp.sum(jnp.where(causal, lf_row, 0.0), axis=1, keepdims=True)
        bcum_row = jnp.sum(jnp.where(t_idx <= s_idx, lf_col, 0.0), axis=0, keepdims=True)

        log_w = jnp.where(causal, bcum_col - bcum_row + i_row, neg_inf)
        m0 = m_scr[h:h + 1, 0:1]
        m_state = bcum_col + m0
        m_new = jnp.maximum(m_state, jnp.max(log_w, axis=1, keepdims=True))
        s_mat = jnp.exp(log_w - m_new) * qk
        g = jnp.exp(m_state - m_new)
        m_end = m_new[chunk - 1:chunk, :]
        b_last = bcum_col[chunk - 1:chunk, :]
        w_end = jnp.exp(b_last - bcum_col + i_col - m_end)
        g_end = jnp.exp(b_last + m0 - m_end)
        return dict(q=q, ks=ks, q_bf=q_bf, ks_bf=ks_bf, s_mat=s_mat, g=g, m_new=m_new,
                    m_end=m_end, w_end=w_end, g_end=g_end)

    def readout(h, t):
        v = v_ref[:, h * DV_ML:(h + 1) * DV_ML]
        c0 = c_scr[h]
        n0 = n_scr[h:h + 1, :]
        num = (jnp.dot(t["s_mat"].astype(BF16), v.astype(BF16), preferred_element_type=F32)
               + t["g"] * lax.dot_general(t["q_bf"], c0.astype(BF16), NT_DIMS,
                                          preferred_element_type=F32))
        den = (jnp.sum(t["s_mat"], axis=1, keepdims=True)
               + t["g"] * jnp.sum(t["q"] * n0, axis=1, keepdims=True))
        return num / jnp.maximum(jnp.abs(den), jnp.exp(-t["m_new"]))

    def update_state(h, t):
        v = v_ref[:, h * DV_ML:(h + 1) * DV_ML]
        vw_bf = (v * t["w_end"]).astype(BF16)
        c_scr[h] = t["g_end"] * c_scr[h] + lax.dot_general(vw_bf, t["ks_bf"], TN_DIMS,
                                                           preferred_element_type=F32)
        n_scr[h:h + 1, :] = (t["g_end"] * n_scr[h:h + 1, :]
                             + jnp.sum(t["ks"] * t["w_end"], axis=0, keepdims=True))
        m_scr[h:h + 1, :] = jnp.broadcast_to(t["m_end"], (1, LANES))

    def emit(h, hh):
        hn = (hh * lax.rsqrt(jnp.mean(hh * hh, axis=-1, keepdims=True) + EPS)
              * gn_ref[:, h * DV_ML:(h + 1) * DV_ML])
        out = hn * jax.nn.sigmoid(og_ref[:, h * DV_ML:(h + 1) * DV_ML])
        h_ref[:, h * DV_ML:(h + 1) * DV_ML] = out.astype(h_ref.dtype)

    terms = [gating(h) for h in range(H_ML)]
    hhs = [readout(h, terms[h]) for h in range(H_ML)]
    for h in range(H_ML):
        update_state(h, terms[h])
    for h in range(H_ML):
        emit(h, hhs[h])

    @pl.when(ci == n_chunks - 1)
    def _():
        c_out[...] = c_scr[...]
        n_out[...] = n_scr[...]
        m_out[...] = m_scr[...]


def _mlstm(proj, gates, c0, n0, m0, norm_ml_out, *, bsz, n_chunks, chunk, valid):
    rows = bsz * n_chunks * chunk
    grow = gates[:, GATE_LANE0:].reshape(bsz, n_chunks, chunk, N_GATES).swapaxes(2, 3)
    m0_b = jnp.broadcast_to(m0[:, :, None], (bsz, H_ML, LANES)).astype(F32)
    qk_blk = ML_QK_W // LANES
    row_map = lambda b, c: (b * n_chunks + c)
    kern = functools.partial(_mlstm_kernel, chunk=chunk, valid=valid, n_chunks=n_chunks)
    state_specs = [pl.BlockSpec((None, H_ML, DV_ML, DQK_ML), lambda b, c: (b, 0, 0, 0)),
                   pl.BlockSpec((None, H_ML, DQK_ML), lambda b, c: (b, 0, 0)),
                   pl.BlockSpec((None, H_ML, LANES), lambda b, c: (b, 0, 0))]
    q_col = (3 * SB_W) // ML_QK_W
    v_col = (3 * SB_W + 2 * ML_QK_W) // ML_V_W
    h, c1, n1, m1 = pl.pallas_call(
        kern,
        out_shape=(jax.ShapeDtypeStruct((rows, ML_V_W), BF16),
                   jax.ShapeDtypeStruct((bsz, H_ML, DV_ML, DQK_ML), F32),
                   jax.ShapeDtypeStruct((bsz, H_ML, DQK_ML), F32),
                   jax.ShapeDtypeStruct((bsz, H_ML, LANES), F32)),
        grid=(bsz, n_chunks),
        in_specs=[pl.BlockSpec((chunk, ML_QK_W), lambda b, c: (row_map(b, c), q_col)),
                  pl.BlockSpec((chunk, ML_QK_W), lambda b, c: (row_map(b, c), q_col + 1)),
                  pl.BlockSpec((chunk, ML_V_W), lambda b, c: (row_map(b, c), v_col)),
                  pl.BlockSpec((chunk, ML_V_W), lambda b, c: (row_map(b, c), v_col + 1)),
                  pl.BlockSpec((chunk, LANES), lambda b, c: (row_map(b, c), 0)),
                  pl.BlockSpec((None, None, N_GATES, chunk), lambda b, c: (b, c, 0, 0))]
                 + state_specs
                 + [pl.BlockSpec((1, ML_V_W), lambda b, c: (0, 0))],
        out_specs=(pl.BlockSpec((chunk, ML_V_W), lambda b, c: (row_map(b, c), 0)),) + tuple(state_specs),
        scratch_shapes=[pltpu.VMEM((H_ML, DV_ML, DQK_ML), F32),
                        pltpu.VMEM((H_ML, DQK_ML), F32),
                        pltpu.VMEM((H_ML, LANES), F32)],
        compiler_params=_cparams(("arbitrary", "arbitrary"), 40),
        name="mlstm",
    )(proj, proj, proj, proj, gates, grow, c0, n0, m0_b, norm_ml_out.reshape(1, ML_V_W))
    del qk_blk
    return h, c1, n1, m1[:, :, 0]


def _outproj_kernel(osb_ref, hml_ref, w_ref, x_ref, gt_ref, sh_ref, sc_ref, g_ref, wr_ref, br_ref,
                    h_ref, xf_ref, idx_ref, gate_ref, cat_scr, h_scr, *, tn, n_col):
    j = pl.program_id(1)

    @pl.when(j == 0)
    def _():
        cat_scr[:, 0:SB_W] = osb_ref[...]
        cat_scr[:, SB_W:SB_W + ML_V_W] = hml_ref[...]

    mix = jnp.dot(cat_scr[...], w_ref[...].astype(BF16), preferred_element_type=F32)
    h_blk = x_ref[...] + gt_ref[...] * mix
    h_ref[...] = h_blk
    for jj in range(n_col):
        @pl.when(j == jj)
        def _(jj=jj):
            h_scr[:, jj * tn:(jj + 1) * tn] = h_blk

    @pl.when(j == n_col - 1)
    def _():
        hf = h_scr[...]
        xn = hf * lax.rsqrt(jnp.mean(hf * hf, axis=-1, keepdims=True) + EPS) * g_ref[...]
        xf = xn * (1.0 + sc_ref[...]) + sh_ref[...]
        xf_ref[...] = xf
        logits = _dot3(xf, wr_ref[...]) + br_ref[...]
        lane = lax.broadcasted_iota(I32, logits.shape, 1)
        lane_f = lane.astype(F32)
        neg_inf = -jnp.inf
        lg = jnp.where(lane < N_EXPERTS, logits, neg_inf)
        vals, idxs = [], []
        for _k in range(TOP_K):
            mx = jnp.max(lg, axis=1, keepdims=True)
            ix = jnp.min(jnp.where(lg == mx, lane_f, float(LANES)), axis=1, keepdims=True)
            vals.append(mx)
            idxs.append(ix)
            lg = jnp.where(lane_f == ix, neg_inf, lg)
        es = [jnp.exp(vk - vals[0]) for vk in vals]
        tot = es[0] + es[1] + es[2] + es[3]
        idx_out = jnp.zeros(logits.shape, F32)
        gate_out = jnp.zeros(logits.shape, F32)
        for k in range(TOP_K):
            idx_out = jnp.where(lane == k, idxs[k], idx_out)
            gate_out = jnp.where(lane == k, es[k] / tot, gate_out)
        idx_ref[...] = idx_out.astype(I32)
        gate_ref[...] = gate_out


def _outproj(o_sb, h_ml, w_out, x2d, gate_a, shift_f, scale_f, norm_ffn, wr_pad, br_pad,
             *, tm, rows_per_group):
    t, d = x2d.shape
    tn = 512
    n_col = d // tn
    kern = functools.partial(_outproj_kernel, tn=tn, n_col=n_col)
    return pl.pallas_call(
        kern,
        out_shape=(jax.ShapeDtypeStruct((t, d), F32),
                   jax.ShapeDtypeStruct((t, d), F32),
                   jax.ShapeDtypeStruct((t, LANES), I32),
                   jax.ShapeDtypeStruct((t, LANES), F32)),
        grid=(t // tm, n_col),
        in_specs=[pl.BlockSpec((tm, SB_W), lambda i, j: (i, 0)),
                  pl.BlockSpec((tm, ML_V_W), lambda i, j: (i, 0)),
                  pl.BlockSpec((SB_W + ML_V_W, tn), lambda i, j: (0, j)),
                  pl.BlockSpec((tm, tn), lambda i, j: (i, j)),
                  _mod_spec(gate_a, tm, rows_per_group, col_tile=tn),
                  _mod_spec(shift_f, tm, rows_per_group),
                  _mod_spec(scale_f, tm, rows_per_group),
                  pl.BlockSpec((1, d), lambda i, j: (0, 0)),
                  pl.BlockSpec((d, LANES), lambda i, j: (0, 0)),
                  pl.BlockSpec((1, LANES), lambda i, j: (0, 0))],
        out_specs=(pl.BlockSpec((tm, tn), lambda i, j: (i, j)),
                   pl.BlockSpec((tm, d), lambda i, j: (i, 0)),
                   pl.BlockSpec((tm, LANES), lambda i, j: (i, 0)),
                   pl.BlockSpec((tm, LANES), lambda i, j: (i, 0))),
        scratch_shapes=[pltpu.VMEM((tm, SB_W + ML_V_W), BF16),
                        pltpu.VMEM((tm, d), F32)],
        compiler_params=_cparams(("arbitrary", "arbitrary"), 48),
        name="outproj",
    )(o_sb, h_ml, w_out, x2d, gate_a, shift_f, scale_f, norm_ffn.reshape(1, d), wr_pad, br_pad)


def _scatter_kernel(pos_ref, last_ref, xp_ref, xs_ref, o_hbm, sem, *, tok_tile, n_p, n_s):
    i = pl.program_id(0)
    fill_rows = min(MOE_TILE, tok_tile)

    @pl.when(i == 0)
    def _():
        def fill_copy(e, part):
            row = last_ref[e] * MOE_TILE + part * fill_rows
            return pltpu.make_async_copy(xp_ref.at[pl.ds(0, fill_rows), :],
                                         o_hbm.at[pl.ds(row, fill_rows), :], sem)

        def fill(e, carry):
            @pl.when(last_ref[e] >= 0)
            def _():
                for part in range(MOE_TILE // fill_rows):
                    fill_copy(e, part).start()
            return carry

        def drain(e, carry):
            @pl.when(last_ref[e] >= 0)
            def _():
                for part in range(MOE_TILE // fill_rows):
                    fill_copy(e, part).wait()
            return carry

        lax.fori_loop(0, N_EXPERTS, fill, 0)
        lax.fori_loop(0, N_EXPERTS, drain, 0)

    def scatter_block(x_ref, n_rows, base):
        def token(tok, carry):
            for k in range(TOP_K):
                p = pos_ref[base + tok * TOP_K + k]
                pltpu.make_async_copy(x_ref.at[pl.ds(tok, 1), :], o_hbm.at[pl.ds(p, 1), :],
                                      sem).start()
            return carry

        lax.fori_loop(0, n_rows, token, 0, unroll=8)
        done = o_hbm.at[pl.ds(0, n_rows * TOP_K), :]
        pltpu.make_async_copy(done, done, sem).wait()

    @pl.when(i < n_p // tok_tile)
    def _():
        scatter_block(xp_ref, tok_tile, i * tok_tile * TOP_K)

    @pl.when(i == n_p // tok_tile)
    def _():
        scatter_block(xs_ref, n_s, n_p * TOP_K)


def _scatter_rows(pos, last_tile, xf_p, xf_s, n_rows):
    n_p, w = xf_p.shape
    n_s = xf_s.shape[0]
    tok_tile = next(t for t in range(512, 0, -SUBLANES) if n_p % t == 0)
    n_p_steps = n_p // tok_tile
    kern = functools.partial(_scatter_kernel, tok_tile=tok_tile, n_p=n_p, n_s=n_s)
    grid_spec = pltpu.PrefetchScalarGridSpec(
        num_scalar_prefetch=2,
        grid=(n_p_steps + 1,),
        in_specs=[pl.BlockSpec((tok_tile, w), lambda i, p, q: (jnp.minimum(i, n_p_steps - 1), 0)),
                  pl.BlockSpec((n_s, w), lambda i, p, q: (0, 0))],
        out_specs=pl.BlockSpec(memory_space=pl.ANY),
        scratch_shapes=[pltpu.SemaphoreType.DMA],
    )
    return pl.pallas_call(
        kern,
        out_shape=jax.ShapeDtypeStruct((n_rows, w), xf_p.dtype),
        grid_spec=grid_spec,
        compiler_params=_cparams(("arbitrary",), 32),
        name="moe_scatter",
    )(pos, last_tile, xf_p, xf_s)


def _moe_kernel(e_ref, start_ref, tiles_ref, x_hbm, wu_ref, bu_ref, wd_ref, bd_ref, y_hbm,
                x_stage, x_bf, acc, wu_bf, wd_perm, wd_bf, sem_in, sem_out, *, n_ff, d_model, n_items):
    del e_ref
    it = pl.program_id(0)
    f = pl.program_id(1)
    n_tiles = tiles_ref[it]
    row0 = pl.multiple_of(start_ref[it] * MOE_TILE, MOE_TILE)
    half = LANES // 2

    def tile_rows(t):
        return pl.ds(pl.multiple_of(t * MOE_TILE, MOE_TILE), MOE_TILE)

    def x_copy(item, t):
        src0 = pl.multiple_of(start_ref[item] * MOE_TILE, MOE_TILE)
        return pltpu.make_async_copy(x_hbm.at[pl.ds(src0 + t * MOE_TILE, MOE_TILE), :],
                                     x_stage.at[tile_rows(t), :], sem_in)

    def y_copy(t):
        return pltpu.make_async_copy(acc.at[tile_rows(t), :],
                                     y_hbm.at[pl.ds(row0 + t * MOE_TILE, MOE_TILE), :], sem_out)

    def for_tiles(count, fn):
        def body(t, carry):
            fn(t)
            return carry
        lax.fori_loop(0, count, body, 0)

    @pl.when(n_tiles > 0)
    def _():
        @pl.when(f == 0)
        def _():
            @pl.when(it == 0)
            def _():
                for_tiles(n_tiles, lambda t: x_copy(it, t).start())

            for_tiles(n_tiles, lambda t: x_copy(it, t).wait())
            b_dn = jnp.broadcast_to(bd_ref[...], (MOE_TILE, d_model))

            def init(t):
                x_bf[tile_rows(t), :] = x_stage[tile_rows(t), :].astype(BF16)
                acc[tile_rows(t), :] = b_dn

            for_tiles(n_tiles, init)
            nxt = jnp.minimum(it + 1, n_items - 1)
            n_next = jnp.where(it + 1 < n_items, tiles_ref[nxt], 0)
            for_tiles(n_next, lambda t: x_copy(nxt, t).start())

        for c in range(MOE_FF_TILE // LANES):
            for s in range(d_model // LANES):
                for par in range(2):
                    src = wd_ref[c * LANES + par * half:c * LANES + (par + 1) * half,
                                 s * LANES:(s + 1) * LANES]
                    wd_perm[s, pl.ds(c * LANES + par, half, stride=2), :] = src
        for s in range(d_model // LANES):
            wd_bf[:, s * LANES:(s + 1) * LANES] = wd_perm[s].astype(BF16)

        b_up = bu_ref[...]

        def span_rows(t, n):
            return pl.ds(pl.multiple_of(t * MOE_TILE, MOE_TILE), n * MOE_TILE)

        def up_proj(t, n, wu):
            return jnp.dot(x_bf[span_rows(t, n), :], wu,
                           preferred_element_type=F32) + b_up

        def activation(hu):
            even = (lax.broadcasted_iota(I32, (hu.shape[0], LANES), 1) % 2) == 0
            acts = []
            for c in range(MOE_FF_TILE // LANES):
                a_blk = hu[:, (2 * c) * LANES:(2 * c + 1) * LANES]
                b_blk = hu[:, (2 * c + 1) * LANES:(2 * c + 2) * LANES]
                gate = jnp.where(even, a_blk, pltpu.roll(b_blk, 1, 1))
                up = jnp.where(even, pltpu.roll(a_blk, LANES - 1, 1), b_blk)
                gate = jnp.minimum(gate, SWIGLU_LIMIT)
                up = jnp.clip(up, -SWIGLU_LIMIT, SWIGLU_LIMIT)
                acts.append((up + 1.0) * gate * jax.nn.sigmoid(SWIGLU_ALPHA * gate))
            return jnp.concatenate(acts, axis=1).astype(BF16)

        def down_proj(t, n, act):
            acc[span_rows(t, n), :] += jnp.dot(act, wd_bf[...], preferred_element_type=F32)

        def chains(spans):
            wu = wu_ref[...].astype(BF16)
            hus = [up_proj(t, n, wu) for t, n in spans]
            acts = [activation(hu) for hu in hus]
            for (t, n), act in zip(spans, acts):
                down_proj(t, n, act)

        def run_tiles(after_tile):
            def quad(p, carry):
                chains([(4 * p, 2), (4 * p + 2, 2)])
                for u in range(4):
                    after_tile(4 * p + u)
                return carry

            n_quads = n_tiles // 4
            lax.fori_loop(0, n_quads, quad, 0)
            rest = n_tiles - 4 * n_quads

            @pl.when(rest >= 2)
            def _():
                chains([(4 * n_quads, 1), (4 * n_quads + 1, 1)])
                after_tile(4 * n_quads)
                after_tile(4 * n_quads + 1)

            @pl.when(rest % 2 == 1)
            def _():
                chains([(n_tiles - 1, 1)])
                after_tile(n_tiles - 1)

        @pl.when(f < n_ff - 1)
        def _():
            run_tiles(lambda t: None)

        @pl.when(f == n_ff - 1)
        def _():
            run_tiles(lambda t: y_copy(t).start())
            for_tiles(n_tiles, lambda t: y_copy(t).wait())


def _moe(item_e, item_start, item_tiles, x_sorted, w_up, b_up, w_down, b_down):
    n_rows = x_sorted.shape[0]
    n_exp, d, ff2_total = w_up.shape
    d_ff = ff2_total // 2
    n_ff = d_ff // MOE_FF_TILE
    n_items = item_e.shape[0]
    item_rows = MOE_ITEM_TILES * MOE_TILE
    kern = functools.partial(_moe_kernel, n_ff=n_ff, d_model=d, n_items=n_items)

    def ff_tile(i, f, n):
        return jnp.where(n[i] > 0, f, n_ff - 1)

    grid_spec = pltpu.PrefetchScalarGridSpec(
        num_scalar_prefetch=3,
        grid=(n_items, n_ff),
        in_specs=[pl.BlockSpec(memory_space=pl.ANY),
                  pl.BlockSpec((None, d, 2 * MOE_FF_TILE),
                               lambda i, f, e, s, n: (e[i], 0, ff_tile(i, f, n))),
                  pl.BlockSpec((None, 1, 2 * MOE_FF_TILE),
                               lambda i, f, e, s, n: (e[i], 0, ff_tile(i, f, n))),
                  pl.BlockSpec((None, MOE_FF_TILE, d),
                               lambda i, f, e, s, n: (e[i], ff_tile(i, f, n), 0)),
                  pl.BlockSpec((None, 1, d), lambda i, f, e, s, n: (e[i], 0, 0))],
        out_specs=pl.BlockSpec(memory_space=pl.ANY),
        scratch_shapes=[pltpu.VMEM((item_rows, d), F32),
                        pltpu.VMEM((item_rows, d), BF16),
                        pltpu.VMEM((item_rows, d), F32),
                        pltpu.VMEM((d, 2 * MOE_FF_TILE), BF16),
                        pltpu.VMEM((d // LANES, MOE_FF_TILE, LANES), F32),
                        pltpu.VMEM((MOE_FF_TILE, d), BF16),
                        pltpu.SemaphoreType.DMA,
                        pltpu.SemaphoreType.DMA],
    )
    return pl.pallas_call(
        kern,
        out_shape=jax.ShapeDtypeStruct((n_rows, d), F32),
        grid_spec=grid_spec,
        compiler_params=_cparams(("arbitrary", "arbitrary"), 56),
        name="moe_experts",
    )(item_e, item_start, item_tiles, x_sorted, w_up,
      b_up.reshape(n_exp, 1, ff2_total), w_down, b_down.reshape(n_exp, 1, d))


def _combine_kernel(pos_ref, y_hbm, h_ref, gate_ref, gt_ref, g_ref, o_ref, buf, sem, *, tok_off):
    i = pl.program_id(0)

    def issue(step, slot):
        base = (step * COMBINE_TOK + tok_off) * TOP_K

        def token(tok, carry):
            for k in range(TOP_K):
                p = pos_ref[base + tok * TOP_K + k]
                pltpu.make_async_copy(y_hbm.at[pl.ds(p, 1), :],
                                      buf.at[slot, pl.ds(k * COMBINE_TOK + tok, 1), :],
                                      sem.at[slot]).start()
            return carry

        lax.fori_loop(0, COMBINE_TOK, token, 0, unroll=8)

    @pl.when(i == 0)
    def _():
        issue(0, 0)

    @pl.when(i + 1 < pl.num_programs(0))
    def _():
        issue(i + 1, (i + 1) % 2)

    slot = i % 2
    pltpu.make_async_copy(y_hbm.at[pl.ds(0, COMBINE_TOK * TOP_K), :], buf.at[slot],
                          sem.at[slot]).wait()
    gates = gate_ref[...]
    ffn = jnp.zeros(h_ref.shape, F32)
    for k in range(TOP_K):
        ffn = ffn + gates[:, k:k + 1] * buf[slot, k * COMBINE_TOK:(k + 1) * COMBINE_TOK, :]
    y = h_ref[...] + gt_ref[...] * ffn
    o_ref[...] = y * lax.rsqrt(jnp.mean(y * y, axis=-1, keepdims=True) + EPS) * g_ref[...]


def _combine(pos, y_rows, h, gate, gate_f, norm_final, *, tok_off, rows_per_group):
    n_tok, d = h.shape
    kern = functools.partial(_combine_kernel, tok_off=tok_off)
    grid_spec = pltpu.PrefetchScalarGridSpec(
        num_scalar_prefetch=1,
        grid=(n_tok // COMBINE_TOK,),
        in_specs=[pl.BlockSpec(memory_space=pl.ANY),
                  pl.BlockSpec((COMBINE_TOK, d), lambda i, p: (i, 0)),
                  pl.BlockSpec((COMBINE_TOK, LANES), lambda i, p: (i, 0)),
                  _mod_spec(gate_f, COMBINE_TOK, rows_per_group),
                  pl.BlockSpec((1, d), lambda i, p: (0, 0))],
        out_specs=pl.BlockSpec((COMBINE_TOK, d), lambda i, p: (i, 0)),
        scratch_shapes=[pltpu.VMEM((2, COMBINE_TOK * TOP_K, d), F32),
                        pltpu.SemaphoreType.DMA((2,))],
    )
    return pl.pallas_call(
        kern,
        out_shape=jax.ShapeDtypeStruct((n_tok, d), F32),
        grid_spec=grid_spec,
        compiler_params=_cparams(("arbitrary",), 32),
        name="moe_combine",
    )(pos, y_rows, h, gate, gate_f, norm_final.reshape(1, d))


def _routing_tables(top_idx):
    n_tok = top_idx.shape[0]
    n_assign = n_tok * TOP_K
    flat_e = top_idx.reshape(-1)
    onehot = (flat_e[:, None] == jnp.arange(N_EXPERTS, dtype=I32)[None, :]).astype(I32)
    running = jnp.cumsum(onehot, axis=0)
    rank = jnp.take_along_axis(running, flat_e[:, None], axis=1)[:, 0] - 1
    counts = running[-1]
    tiles_e = (counts + MOE_TILE - 1) // MOE_TILE
    tile_start_e = jnp.cumsum(tiles_e) - tiles_e
    pos = tile_start_e[flat_e] * MOE_TILE + rank

    n_tiles_max = -(-n_assign // MOE_TILE) + N_EXPERTS
    n_rows = n_tiles_max * MOE_TILE
    last_tile = jnp.where(tiles_e > 0, tile_start_e + tiles_e - 1, -1)

    n_items_max = (n_tiles_max + N_EXPERTS * (MOE_ITEM_TILES - 1)) // MOE_ITEM_TILES
    items_e = (tiles_e + MOE_ITEM_TILES - 1) // MOE_ITEM_TILES
    items_end = jnp.cumsum(items_e)
    it = jnp.arange(n_items_max, dtype=I32)
    e_of = jnp.minimum(jnp.sum((it[:, None] >= items_end[None, :]).astype(I32), axis=1), N_EXPERTS - 1)
    local = it - (items_end[e_of] - items_e[e_of])
    valid = it < items_end[-1]
    tiles_left = tiles_e[e_of] - local * MOE_ITEM_TILES
    item_tiles = jnp.where(valid, jnp.clip(tiles_left, 0, MOE_ITEM_TILES), 0).astype(I32)
    item_start = jnp.where(valid, tile_start_e[e_of] + local * MOE_ITEM_TILES, 0).astype(I32)
    last_e = e_of[jnp.maximum(items_end[-1] - 1, 0)]
    item_e = jnp.where(valid, e_of, last_e).astype(I32)
    return pos.astype(I32), last_tile.astype(I32), n_rows, item_e, item_start, item_tiles


def _pick_tile(n, pref):
    t = min(n, pref)
    while n % t:
        t //= 2
    return t


def kernel(x_prompt, x_sample, c_prompt, c_sample, cache_k, cache_v, page_table, state_C, state_n, state_m, w_ada, b_ada, norm_mix, norm_ffn, w_in, b_gates, sb_bias, norm_sb_out, norm_ml_out, w_out, w_router, b_router, w_up, b_up, w_down, b_down, norm_final):
    bsz, seq, d = x_prompt.shape
    dbs, dec_seq, _ = x_sample.shape
    depth = w_ada.shape[0]
    assert depth == 1
    n_p, n_s = bsz * seq, dbs * dec_seq
    n_tok = n_p + n_s

    (w_ada, b_ada, norm_mix, norm_ffn, w_in, b_gates, sb_bias, norm_sb_out, norm_ml_out, w_out,
     w_router, b_router, w_up, b_up, w_down, b_down) = [
        a[0] for a in (w_ada, b_ada, norm_mix, norm_ffn, w_in, b_gates, sb_bias, norm_sb_out,
                       norm_ml_out, w_out, w_router, b_router, w_up, b_up, w_down, b_down)]

    n_c = bsz + dbs
    n_c_pad = -(-n_c // SUBLANES) * SUBLANES
    c_all = jnp.concatenate([c_prompt, c_sample, jnp.zeros((n_c_pad - n_c, d), F32)], axis=0)
    mod = _ada(c_all, w_ada, b_ada)
    mods_p = [mod[:bsz, i * d:(i + 1) * d].reshape(bsz, 1, d) for i in range(N_MOD)]
    mods_s = [jnp.repeat(mod[bsz:n_c, i * d:(i + 1) * d], dec_seq, axis=0).reshape(1, n_s, d)
              for i in range(N_MOD)]

    w_in = w_in.T
    wg_pad = w_in[w_in.shape[0] - LANES:, :]
    bg_pad = jnp.pad(b_gates, (GATE_LANE0, 0)).reshape(1, LANES)
    wr_pad = jnp.pad(w_router, ((0, 0), (0, LANES - N_EXPERTS)))
    br_pad = jnp.pad(b_router, (0, LANES - N_EXPERTS)).reshape(1, LANES)

    xp2 = x_prompt.reshape(n_p, d)
    xs2 = x_sample.reshape(n_s, d)
    tm_p = _pick_tile(seq, 1024)

    proj_p, gates_p, k_p, v_p = _inproj(xp2, mods_p[0], mods_p[1], norm_mix, w_in, wg_pad, bg_pad,
                                        tm=tm_p, rows_per_group=seq)
    osb_p = _sb_prompt(proj_p, sb_bias, norm_sb_out, bsz=bsz, seq=seq)
    chunk_p = _pick_tile(seq, ML_CHUNK)
    hml_p, c_p, nn_p, m_p = _mlstm(
        proj_p, gates_p,
        jnp.zeros((bsz, H_ML, DV_ML, DQK_ML), F32), jnp.zeros((bsz, H_ML, DQK_ML), F32),
        jnp.zeros((bsz, H_ML), F32), norm_ml_out,
        bsz=bsz, n_chunks=seq // chunk_p, chunk=chunk_p, valid=chunk_p)

    proj_s, gates_s, k_s, v_s = _inproj(xs2, mods_s[0], mods_s[1], norm_mix, w_in, wg_pad, bg_pad,
                                        tm=n_s, rows_per_group=n_s)
    osb_s = _sb_paged(proj_s[:, :SB_W].reshape(dbs, dec_seq, SB_W),
                      k_s.reshape(dbs, dec_seq, SB_W), v_s.reshape(dbs, dec_seq, SB_W),
                      cache_k[0], cache_v[0], page_table, sb_bias, norm_sb_out)
    osb_s = (osb_s.reshape(dbs, H_SB, dec_seq, D_SB).swapaxes(1, 2)
             .reshape(n_s, SB_W).astype(BF16))
    chunk_s = -(-dec_seq // BF16_SUBLANES) * BF16_SUBLANES
    pad_rows = lambda a: jnp.pad(a.reshape(dbs, dec_seq, a.shape[-1]),
                                 ((0, 0), (0, chunk_s - dec_seq), (0, 0))).reshape(dbs * chunk_s, a.shape[-1])
    hml_s, c_s, nn_s, m_s = _mlstm(
        pad_rows(proj_s), pad_rows(gates_s), state_C[0], state_n[0], state_m[0], norm_ml_out,
        bsz=dbs, n_chunks=1, chunk=chunk_s, valid=dec_seq)
    hml_s = hml_s.reshape(dbs, chunk_s, ML_V_W)[:, :dec_seq].reshape(n_s, ML_V_W)

    tm_o = _pick_tile(seq, 512)
    h_p, xf_p, idx_p, gate_p = _outproj(
        osb_p, hml_p, w_out, xp2, mods_p[2], mods_p[3], mods_p[4], norm_ffn, wr_pad, br_pad,
        tm=tm_o, rows_per_group=seq)
    h_s, xf_s, idx_s, gate_s = _outproj(
        osb_s, hml_s, w_out, xs2, mods_s[2], mods_s[3], mods_s[4], norm_ffn, wr_pad, br_pad,
        tm=n_s, rows_per_group=n_s)
    top_idx = jnp.concatenate([idx_p[:, :TOP_K], idx_s[:, :TOP_K]], axis=0)

    pos, last_tile, n_rows, item_e, item_start, item_tiles = _routing_tables(top_idx)
    x_sorted = _scatter_rows(pos, last_tile, xf_p, xf_s, n_rows)
    y_rows = _moe(item_e, item_start, item_tiles, x_sorted, w_up, b_up, w_down, b_down)
    y_p = _combine(pos, y_rows, h_p, gate_p, mods_p[5], norm_final, tok_off=0, rows_per_group=seq)
    y_s = _combine(pos, y_rows, h_s, gate_s, mods_s[5], norm_final, tok_off=n_p, rows_per_group=n_s)

    return (y_p.reshape(bsz, seq, d), y_s.reshape(dbs, dec_seq, d),
            k_p.reshape(1, bsz, seq, H_SB, D_SB), v_p.reshape(1, bsz, seq, H_SB, D_SB),
            k_s.reshape(1, dbs, dec_seq, H_SB, D_SB), v_s.reshape(1, dbs, dec_seq, H_SB, D_SB),
            c_p[None], nn_p[None], m_p[None], c_s[None], nn_s[None], m_s[None])
```

```python
import functools

import jax
import jax.numpy as jnp
from jax import lax
from jax.experimental import pallas as pl
from jax.experimental.pallas import tpu as pltpu

F32 = jnp.float32
BF16 = jnp.bfloat16
I32 = jnp.int32

H_SB = 8
D_SB = 128
SB_W = H_SB * D_SB
H_ML = 4
DQK_ML = 128
DV_ML = 256
ML_QK_W = H_ML * DQK_ML
ML_V_W = H_ML * DV_ML
PROJ_MAIN_W = 3 * SB_W + 2 * ML_QK_W + 2 * ML_V_W
N_GATES = 2 * H_ML
N_EXPERTS = 32
TOP_K = 4
N_MOD = 6
SWIGLU_LIMIT = 7.0
SWIGLU_ALPHA = 1.702
EPS = 1e-6

LANES = 128
SUBLANES = 8
BF16_SUBLANES = 16
MIB = 1024 * 1024
GATE_LANE0 = LANES - N_GATES

MOE_TILE = 128
MOE_ITEM_TILES = 12
MOE_FF_TILE = 256
ML_CHUNK = 128
SB_BLOCK = 512
SB_SUB = 256
PAGES_PER_STEP = 8
PAGE_RING = 3
COMBINE_TOK = 64

NT_DIMS = (((1,), (1,)), ((), ()))
TN_DIMS = (((0,), (0,)), ((), ()))


def _cparams(sem, vmem_mib):
    return pltpu.CompilerParams(dimension_semantics=sem, vmem_limit_bytes=vmem_mib * MIB)


def _log_sigmoid_pair(z):
    t = jnp.log(1.0 + jnp.exp(-jnp.abs(z)))
    return jnp.minimum(z, 0.0) - t, -jnp.maximum(z, 0.0) - t


def _split_bf16(x):
    hi = x.astype(BF16)
    lo = (x - hi.astype(F32)).astype(BF16)
    return hi, lo


def _dot3(x, w):
    xh, xl = _split_bf16(x)
    wh, wl = _split_bf16(w)
    return (jnp.dot(xh, wh, preferred_element_type=F32)
            + (jnp.dot(xl, wh, preferred_element_type=F32)
               + jnp.dot(xh, wl, preferred_element_type=F32)))


def _mod_spec(arr, tile, rows_per_group, col_tile=None):
    per_group = arr.shape[1] == 1
    width = arr.shape[2] if col_tile is None else col_tile

    def index_map(i, *rest):
        col = 0 if col_tile is None else rest[0]
        if per_group:
            return ((i * tile) // rows_per_group, 0, col)
        return (0, i, col)

    return pl.BlockSpec((None, 1 if per_group else tile, width), index_map)


def _ada_kernel(c_ref, w_ref, b_ref, o_ref):
    c = c_ref[...]
    o_ref[...] = _dot3(c * jax.nn.sigmoid(c), w_ref[...]) + b_ref[...]


def _ada(c_all, w_ada, b_ada):
    n, d = c_all.shape
    w_out = w_ada.shape[1]
    tn = 1024
    return pl.pallas_call(
        _ada_kernel,
        out_shape=jax.ShapeDtypeStruct((n, w_out), F32),
        grid=(w_out // tn,),
        in_specs=[pl.BlockSpec((n, d), lambda j: (0, 0)),
                  pl.BlockSpec((d, tn), lambda j: (0, j)),
                  pl.BlockSpec((1, tn), lambda j: (0, j))],
        out_specs=pl.BlockSpec((n, tn), lambda j: (0, j)),
        compiler_params=_cparams(("arbitrary",), 40),
        name="ada",
    )(c_all, w_ada, b_ada.reshape(1, w_out))


def _inproj_kernel(x_ref, sh_ref, sc_ref, g_ref, w_ref, wg_ref, bg_ref,
                   proj_ref, gates_ref, k_ref, v_ref, xm_scr, *, tn):
    j = pl.program_id(1)

    @pl.when(j == 0)
    def _():
        x = x_ref[...]
        xn = x * lax.rsqrt(jnp.mean(x * x, axis=-1, keepdims=True) + EPS) * g_ref[...]
        xm = (xn * (1.0 + sc_ref[...]) + sh_ref[...]).astype(BF16)
        xm_scr[...] = xm
        gates_ref[...] = lax.dot_general(xm, wg_ref[...].astype(BF16), NT_DIMS,
                                         preferred_element_type=F32) + bg_ref[...]

    res = lax.dot_general(xm_scr[...], w_ref[...].astype(BF16), NT_DIMS,
                          preferred_element_type=F32)
    proj_ref[...] = res
    k_tiles = SB_W // tn

    @pl.when((j >= k_tiles) & (j < 2 * k_tiles))
    def _():
        k_ref[...] = res

    @pl.when((j >= 2 * k_tiles) & (j < 3 * k_tiles))
    def _():
        v_ref[...] = res


def _inproj(x2d, shift, scale, norm_w, w_in, wg_pad, bg_pad, *, tm, rows_per_group):
    t, d = x2d.shape
    tn = 512
    k_tiles = SB_W // tn
    mod_spec = _mod_spec(shift, tm, rows_per_group)

    def kv_spec(first):
        return pl.BlockSpec((tm, tn), lambda i, j: (i, jnp.clip(j - first, 0, k_tiles - 1)))

    return pl.pallas_call(
        functools.partial(_inproj_kernel, tn=tn),
        out_shape=(jax.ShapeDtypeStruct((t, PROJ_MAIN_W), F32),
                   jax.ShapeDtypeStruct((t, LANES), F32),
                   jax.ShapeDtypeStruct((t, SB_W), F32),
                   jax.ShapeDtypeStruct((t, SB_W), F32)),
        grid=(t // tm, PROJ_MAIN_W // tn),
        in_specs=[pl.BlockSpec((tm, d), lambda i, j: (i, 0)),
                  mod_spec, mod_spec,
                  pl.BlockSpec((1, d), lambda i, j: (0, 0)),
                  pl.BlockSpec((tn, d), lambda i, j: (j, 0)),
                  pl.BlockSpec((LANES, d), lambda i, j: (0, 0)),
                  pl.BlockSpec((1, LANES), lambda i, j: (0, 0))],
        out_specs=(pl.BlockSpec((tm, tn), lambda i, j: (i, j)),
                   pl.BlockSpec((tm, LANES), lambda i, j: (i, 0)),
                   kv_spec(k_tiles), kv_spec(2 * k_tiles)),
        scratch_shapes=[pltpu.VMEM((tm, d), BF16)],
        compiler_params=_cparams(("arbitrary", "arbitrary"), 48),
        name="inproj",
    )(x2d, shift, scale, norm_w.reshape(1, d), w_in, wg_pad, bg_pad)


def _sb_strip(z, mask, from_uu, sub, v_bf, spent):
    pos = jnp.maximum(z, 0.0)
    cost = pos + jnp.log(1.0 + jnp.exp(jnp.minimum(z, 0.0) - pos))
    if mask is not None:
        cost = jnp.where(mask, cost, 0.0)
    hi, lo = _split_bf16(cost)
    n_sub = z.shape[1] // sub
    parts = [None] * n_sub
    for j in reversed(range(n_sub)):
        cols = slice(j * sub, (j + 1) * sub)
        from_s = jnp.dot(jnp.concatenate([hi[:, cols], lo[:, cols]], axis=1), from_uu,
                         preferred_element_type=F32)
        parts[j] = jnp.exp(z[:, cols] - (from_s + spent))
        spent = spent + from_s[:, 0:1]
    a = parts[0] if n_sub == 1 else jnp.concatenate(parts, axis=1)
    if mask is not None:
        a = jnp.where(mask, a, 0.0)
    return jnp.dot(a.astype(BF16), v_bf, preferred_element_type=F32), spent


def _from_matrix2(n):
    j = lax.broadcasted_iota(I32, (2 * n, n), 0) % n
    s = lax.broadcasted_iota(I32, (2 * n, n), 1)
    return jnp.where(j >= s, 1.0, 0.0).astype(BF16)


def _sbp_kernel(bias_ref, q_ref, k_ref, v_ref, g_ref, o_ref, o_scr, carry_scr, *, blk, sub, scale):
    h = pl.program_id(1)
    qi = pl.program_id(2)
    bias = bias_ref[h]
    q = q_ref[...].astype(BF16)
    from_uu = _from_matrix2(sub)

    def strip(k_start, width, diagonal):
        k_bf = k_ref[pl.ds(k_start, width), :].astype(BF16)
        v_bf = v_ref[pl.ds(k_start, width), :].astype(BF16)
        z = lax.dot_general(q, k_bf, NT_DIMS, preferred_element_type=F32) * scale + bias
        mask = None
        if diagonal:
            t = lax.broadcasted_iota(I32, (blk, width), 0)
            s = lax.broadcasted_iota(I32, (blk, width), 1)
            mask = s < t
        o_blk, spent = _sb_strip(z, mask, from_uu, sub, v_bf, carry_scr[:, 0:1])
        o_scr[...] += o_blk
        carry_scr[...] = jnp.broadcast_to(spent, carry_scr.shape)

    o_scr[...] = jnp.zeros_like(o_scr)
    carry_scr[...] = jnp.zeros_like(carry_scr)
    strip(pl.multiple_of(qi * blk, blk), blk, True)

    @pl.when(qi % 2 == 1)
    def _():
        strip(pl.multiple_of((qi - 1) * blk, blk), blk, False)

    def older(n, carry):
        strip(pl.multiple_of((qi // 2 - 1 - n) * (2 * blk), 2 * blk), 2 * blk, False)
        return carry

    lax.fori_loop(0, qi // 2, older, 0)
    o = o_scr[...]
    o = o * lax.rsqrt(jnp.mean(o * o, axis=-1, keepdims=True) + EPS) * g_ref[...]
    o_ref[...] = o.astype(o_ref.dtype)


def _sb_prompt(proj, sb_bias, norm_sb_out, *, bsz, seq):
    blk = min(SB_BLOCK, seq)
    nq = seq // blk
    kern = functools.partial(_sbp_kernel, blk=blk, sub=min(SB_SUB, blk), scale=D_SB ** -0.5)
    return pl.pallas_call(
        kern,
        out_shape=jax.ShapeDtypeStruct((bsz * seq, SB_W), BF16),
        grid=(bsz, H_SB, nq),
        in_specs=[pl.BlockSpec(memory_space=pltpu.SMEM),
                  pl.BlockSpec((blk, D_SB), lambda b, h, i: (b * nq + i, h)),
                  pl.BlockSpec((seq, D_SB), lambda b, h, i: (b, H_SB + h)),
                  pl.BlockSpec((seq, D_SB), lambda b, h, i: (b, 2 * H_SB + h)),
                  pl.BlockSpec((None, 1, D_SB), lambda b, h, i: (h, 0, 0))],
        out_specs=pl.BlockSpec((blk, D_SB), lambda b, h, i: (b * nq + i, h)),
        scratch_shapes=[pltpu.VMEM((blk, D_SB), F32), pltpu.VMEM((blk, LANES), F32)],
        compiler_params=_cparams(("arbitrary", "arbitrary", "arbitrary"), 48),
        name="sb_prompt",
    )(sb_bias, proj, proj, proj, norm_sb_out.reshape(H_SB, 1, D_SB))


def _head_block_mask(rows, dec_seq):
    r = lax.broadcasted_iota(I32, (rows, SB_W), 0)
    c = lax.broadcasted_iota(I32, (rows, SB_W), 1)
    return (r // dec_seq) == (c // D_SB)


def _page_heads(page_ref):
    n_keys = page_ref.shape[0] // H_SB
    cols = [page_ref[pl.ds(h, n_keys, stride=H_SB), :] for h in range(H_SB)]
    return jnp.concatenate(cols, axis=1).astype(BF16)


def _sbd_kernel(pt_ref, qbd_ref, kn_ref, vn_ref, bias_ref, g_ref, ck_hbm, cv_hbm, o_ref,
                o_scr, carry_scr, new_scr, kbuf, vbuf, sem, *,
                dec_seq, page, n_steps, n_pages, n_seq, scale):
    rows = H_SB * dec_seq
    j = pl.program_id(1)
    step = pl.program_id(0) * n_steps + j
    n_total = n_seq * n_steps
    from_uu = _from_matrix2(page)

    def step_copies(s):
        b_s = s // n_steps
        j_s = s - b_s * n_steps
        slot = s % PAGE_RING
        copies = []
        for i in range(PAGES_PER_STEP):
            pg = pt_ref[b_s, n_pages - 1 - (j_s * PAGES_PER_STEP + i)]
            copies.append(pltpu.make_async_copy(ck_hbm.at[pg], kbuf.at[slot, i], sem.at[slot]))
            copies.append(pltpu.make_async_copy(cv_hbm.at[pg], vbuf.at[slot, i], sem.at[slot]))
        return copies

    @pl.when(step == 0)
    def _():
        for ahead in range(PAGE_RING - 1):
            @pl.when(ahead < n_total)
            def _(ahead=ahead):
                for cp in step_copies(ahead):
                    cp.start()

    @pl.when(step + PAGE_RING - 1 < n_total)
    def _():
        for cp in step_copies(step + PAGE_RING - 1):
            cp.start()

    for cp in step_copies(step):
        cp.wait()
    slot = step % PAGE_RING
    pages = ([kbuf.at[slot, i] for i in range(PAGES_PER_STEP)]
             + [vbuf.at[slot, i] for i in range(PAGES_PER_STEP)])

    def sweep(k_bf, v_bf, mask, spent):
        width = k_bf.shape[0]
        z = (lax.dot_general(qbd_ref[...], k_bf, NT_DIMS, preferred_element_type=F32) * scale
             + bias_ref[:, 0:width])
        return _sb_strip(z, mask, from_uu, page, v_bf, spent)

    @pl.when(j == 0)
    def _():
        new_scr[...] = jnp.zeros_like(new_scr)
        new_scr[0, 0:dec_seq, :] = kn_ref[...]
        new_scr[1, 0:dec_seq, :] = vn_ref[...]
        r = lax.broadcasted_iota(I32, (rows, page), 0)
        s = lax.broadcasted_iota(I32, (rows, page), 1)
        o_new, spent = sweep(new_scr[0].astype(BF16), new_scr[1].astype(BF16), s < (r % dec_seq),
                             jnp.zeros((rows, 1), F32))
        o_scr[...] = o_new
        carry_scr[...] = jnp.broadcast_to(spent, carry_scr.shape)

    order = range(PAGES_PER_STEP - 1, -1, -1)
    k_bf = jnp.concatenate([_page_heads(pages[i]) for i in order], axis=0)
    v_bf = jnp.concatenate([_page_heads(pages[PAGES_PER_STEP + i]) for i in order], axis=0)
    o_blk, spent = sweep(k_bf, v_bf, None, carry_scr[:, 0:1])
    o_scr[...] += o_blk
    carry_scr[...] = jnp.broadcast_to(spent, carry_scr.shape)

    @pl.when(j == n_steps - 1)
    def _():
        o_full = jnp.where(_head_block_mask(rows, dec_seq), o_scr[...], 0.0)
        o = o_full[:, 0:D_SB]
        for h in range(1, H_SB):
            o = o + o_full[:, h * D_SB:(h + 1) * D_SB]
        o = o * lax.rsqrt(jnp.mean(o * o, axis=-1, keepdims=True) + EPS) * g_ref[...]
        o_ref[...] = o


def _sb_paged(q_s, k_new, v_new, cache_k, cache_v, page_table, sb_bias, norm_sb_out):
    bsz, dec_seq, _ = q_s.shape
    n_pool, page = cache_k.shape[0], cache_k.shape[1]
    n_pages = page_table.shape[1]
    assert n_pages % PAGES_PER_STEP == 0
    n_steps = n_pages // PAGES_PER_STEP
    rows = H_SB * dec_seq
    ck = cache_k.reshape(n_pool, page * H_SB, D_SB)
    cv = cache_v.reshape(n_pool, page * H_SB, D_SB)
    strip_w = PAGES_PER_STEP * page
    bias_rows = jnp.broadcast_to(jnp.repeat(sb_bias, dec_seq)[:, None], (rows, strip_w)).astype(F32)
    g_rows = jnp.repeat(norm_sb_out.reshape(H_SB, D_SB), dec_seq, axis=0)
    q_heads = q_s.reshape(bsz, dec_seq, H_SB, D_SB).swapaxes(1, 2)
    eye = jnp.eye(H_SB, dtype=F32)
    qbd = (q_heads[:, :, :, None, :] * eye[None, :, None, :, None]).reshape(bsz, rows, SB_W).astype(BF16)

    tok_spec = pl.BlockSpec((None, dec_seq, SB_W), lambda b, j, pt: (b, 0, 0))
    kern = functools.partial(_sbd_kernel, dec_seq=dec_seq, page=page, n_steps=n_steps,
                             n_pages=n_pages, n_seq=bsz, scale=D_SB ** -0.5)
    page_buf = pltpu.VMEM((PAGE_RING, PAGES_PER_STEP, page * H_SB, D_SB), F32)
    grid_spec = pltpu.PrefetchScalarGridSpec(
        num_scalar_prefetch=1,
        grid=(bsz, n_steps),
        in_specs=[pl.BlockSpec((None, rows, SB_W), lambda b, j, pt: (b, 0, 0)),
                  tok_spec, tok_spec,
                  pl.BlockSpec((rows, strip_w), lambda b, j, pt: (0, 0)),
                  pl.BlockSpec((rows, D_SB), lambda b, j, pt: (0, 0)),
                  pl.BlockSpec(memory_space=pl.ANY),
                  pl.BlockSpec(memory_space=pl.ANY)],
        out_specs=pl.BlockSpec((None, rows, D_SB), lambda b, j, pt: (b, 0, 0)),
        scratch_shapes=[pltpu.VMEM((rows, SB_W), F32),
                        pltpu.VMEM((rows, LANES), F32),
                        pltpu.VMEM((2, page, SB_W), F32),
                        page_buf, page_buf,
                        pltpu.SemaphoreType.DMA((PAGE_RING,))],
    )
    return pl.pallas_call(
        kern,
        out_shape=jax.ShapeDtypeStruct((bsz, rows, D_SB), F32),
        grid_spec=grid_spec,
        compiler_params=_cparams(("arbitrary", "arbitrary"), 56),
        name="sb_paged",
    )(page_table, qbd, k_new, v_new, bias_rows, g_rows, ck, cv)


def _mlstm_kernel(q_ref, k_ref, v_ref, og_ref, gcol_ref, grow_ref, c0_ref, n0_ref, m0_ref, gn_ref,
                  h_ref, c_out, n_out, m_out, c_scr, n_scr, m_scr, *, chunk, valid, n_chunks):
    ci = pl.program_id(1)

    @pl.when(ci == 0)
    def _():
        c_scr[...] = c0_ref[...]
        n_scr[...] = n0_ref[...]
        m_scr[...] = m0_ref[...]

    t_idx = lax.broadcasted_iota(I32, (chunk, chunk), 0)
    s_idx = lax.broadcasted_iota(I32, (chunk, chunk), 1)
    causal = s_idx <= t_idx
    col_valid = lax.broadcasted_iota(I32, (chunk, 1), 0) < valid
    row_valid = lax.broadcasted_iota(I32, (1, chunk), 1) < valid
    gcol = gcol_ref[...]
    grow = grow_ref[...]
    neg_inf = -jnp.inf

    def gating(h):
        q = q_ref[:, h * DQK_ML:(h + 1) * DQK_ML]
        ks = k_ref[:, h * DQK_ML:(h + 1) * DQK_ML] * (DQK_ML ** -0.5)
        q_bf, ks_bf = q.astype(BF16), ks.astype(BF16)
        qk = lax.dot_general(q_bf, ks_bf, NT_DIMS, preferred_element_type=F32)

        gi, gf = GATE_LANE0 + h, GATE_LANE0 + H_ML + h
        i_col = jnp.where(col_valid, gcol[:, gi:gi + 1], neg_inf)
        i_row = jnp.where(row_valid, grow[h:h + 1, :], neg_inf)
        lf_col = jnp.where(col_valid, _log_sigmoid_pair(gcol[:, gf:gf + 1])[0], 0.0)
        lf_row = jnp.where(row_valid, _log_sigmoid_pair(grow[H_ML + h:H_ML + h + 1, :])[0], 0.0)
        bcum_col = jnp.sum(jnp.where(causal, lf_row, 0.0), axis=1, keepdims=True)
        bcum_row = jnp.sum(jnp.where(t_idx <= s_idx, lf_col, 0.0), axis=0, keepdims=True)

        log_w = jnp.where(causal, bcum_col - bcum_row + i_row, neg_inf)
        m0 = m_scr[h:h + 1, 0:1]
        m_state = bcum_col + m0
        m_new = jnp.maximum(m_state, jnp.max(log_w, axis=1, keepdims=True))
        s_mat = jnp.exp(log_w - m_new) * qk
        g = jnp.exp(m_state - m_new)
        m_end = m_new[chunk - 1:chunk, :]
        b_last = bcum_col[chunk - 1:chunk, :]
        w_end = jnp.exp(b_last - bcum_col + i_col - m_end)
        g_end = jnp.exp(b_last + m0 - m_end)
        return dict(q=q, ks=ks, q_bf=q_bf, ks_bf=ks_bf, s_mat=s_mat, g=g, m_new=m_new,
                    m_end=m_end, w_end=w_end, g_end=g_end)

    def readout(h, t):
        v = v_ref[:, h * DV_ML:(h + 1) * DV_ML]
        c0 = c_scr[h]
        n0 = n_scr[h:h + 1, :]
        num = (jnp.dot(t["s_mat"].astype(BF16), v.astype(BF16), preferred_element_type=F32)
               + t["g"] * lax.dot_general(t["q_bf"], c0.astype(BF16), NT_DIMS,
                                          preferred_element_type=F32))
        den = (jnp.sum(t["s_mat"], axis=1, keepdims=True)
               + t["g"] * jnp.sum(t["q"] * n0, axis=1, keepdims=True))
        return num / jnp.maximum(jnp.abs(den), jnp.exp(-t["m_new"]))

    def update_state(h, t):
        v = v_ref[:, h * DV_ML:(h + 1) * DV_ML]
        vw_bf = (v * t["w_end"]).astype(BF16)
        c_scr[h] = t["g_end"] * c_scr[h] + lax.dot_general(vw_bf, t["ks_bf"], TN_DIMS,
                                                           preferred_element_type=F32)
        n_scr[h:h + 1, :] = (t["g_end"] * n_scr[h:h + 1, :]
                             + jnp.sum(t["ks"] * t["w_end"], axis=0, keepdims=True))
        m_scr[h:h + 1, :] = jnp.broadcast_to(t["m_end"], (1, LANES))

    def emit(h, hh):
        hn = (hh * lax.rsqrt(jnp.mean(hh * hh, axis=-1, keepdims=True) + EPS)
              * gn_ref[:, h * DV_ML:(h + 1) * DV_ML])
        out = hn * jax.nn.sigmoid(og_ref[:, h * DV_ML:(h + 1) * DV_ML])
        h_ref[:, h * DV_ML:(h + 1) * DV_ML] = out.astype(h_ref.dtype)

    terms = [gating(h) for h in range(H_ML)]
    hhs = [readout(h, terms[h]) for h in range(H_ML)]
    for h in range(H_ML):
        update_state(h, terms[h])
    for h in range(H_ML):
        emit(h, hhs[h])

    @pl.when(ci == n_chunks - 1)
    def _():
        c_out[...] = c_scr[...]
        n_out[...] = n_scr[...]
        m_out[...] = m_scr[...]


def _mlstm(proj, gates, c0, n0, m0, norm_ml_out, *, bsz, n_chunks, chunk, valid):
    rows = bsz * n_chunks * chunk
    grow = gates[:, GATE_LANE0:].reshape(bsz, n_chunks, chunk, N_GATES).swapaxes(2, 3)
    m0_b = jnp.broadcast_to(m0[:, :, None], (bsz, H_ML, LANES)).astype(F32)
    qk_blk = ML_QK_W // LANES
    row_map = lambda b, c: (b * n_chunks + c)
    kern = functools.partial(_mlstm_kernel, chunk=chunk, valid=valid, n_chunks=n_chunks)
    state_specs = [pl.BlockSpec((None, H_ML, DV_ML, DQK_ML), lambda b, c: (b, 0, 0, 0)),
                   pl.BlockSpec((None, H_ML, DQK_ML), lambda b, c: (b, 0, 0)),
                   pl.BlockSpec((None, H_ML, LANES), lambda b, c: (b, 0, 0))]
    q_col = (3 * SB_W) // ML_QK_W
    v_col = (3 * SB_W + 2 * ML_QK_W) // ML_V_W
    h, c1, n1, m1 = pl.pallas_call(
        kern,
        out_shape=(jax.ShapeDtypeStruct((rows, ML_V_W), BF16),
                   jax.ShapeDtypeStruct((bsz, H_ML, DV_ML, DQK_ML), F32),
                   jax.ShapeDtypeStruct((bsz, H_ML, DQK_ML), F32),
                   jax.ShapeDtypeStruct((bsz, H_ML, LANES), F32)),
        grid=(bsz, n_chunks),
        in_specs=[pl.BlockSpec((chunk, ML_QK_W), lambda b, c: (row_map(b, c), q_col)),
                  pl.BlockSpec((chunk, ML_QK_W), lambda b, c: (row_map(b, c), q_col + 1)),
                  pl.BlockSpec((chunk, ML_V_W), lambda b, c: (row_map(b, c), v_col)),
                  pl.BlockSpec((chunk, ML_V_W), lambda b, c: (row_map(b, c), v_col + 1)),
                  pl.BlockSpec((chunk, LANES), lambda b, c: (row_map(b, c), 0)),
                  pl.BlockSpec((None, None, N_GATES, chunk), lambda b, c: (b, c, 0, 0))]
                 + state_specs
                 + [pl.BlockSpec((1, ML_V_W), lambda b, c: (0, 0))],
        out_specs=(pl.BlockSpec((chunk, ML_V_W), lambda b, c: (row_map(b, c), 0)),) + tuple(state_specs),
        scratch_shapes=[pltpu.VMEM((H_ML, DV_ML, DQK_ML), F32),
                        pltpu.VMEM((H_ML, DQK_ML), F32),
                        pltpu.VMEM((H_ML, LANES), F32)],
        compiler_params=_cparams(("arbitrary", "arbitrary"), 40),
        name="mlstm",
    )(proj, proj, proj, proj, gates, grow, c0, n0, m0_b, norm_ml_out.reshape(1, ML_V_W))
    del qk_blk
    return h, c1, n1, m1[:, :, 0]


def _outproj_kernel(osb_ref, hml_ref, w_ref, x_ref, gt_ref, sh_ref, sc_ref, g_ref, wr_ref, br_ref,
                    h_ref, xf_ref, idx_ref, gate_ref, cat_scr, h_scr, *, tn, n_col):
    j = pl.program_id(1)

    @pl.when(j == 0)
    def _():
        cat_scr[:, 0:SB_W] = osb_ref[...]
        cat_scr[:, SB_W:SB_W + ML_V_W] = hml_ref[...]

    mix = jnp.dot(cat_scr[...], w_ref[...].astype(BF16), preferred_element_type=F32)
    h_blk = x_ref[...] + gt_ref[...] * mix
    h_ref[...] = h_blk
    for jj in range(n_col):
        @pl.when(j == jj)
        def _(jj=jj):
            h_scr[:, jj * tn:(jj + 1) * tn] = h_blk

    @pl.when(j == n_col - 1)
    def _():
        hf = h_scr[...]
        xn = hf * lax.rsqrt(jnp.mean(hf * hf, axis=-1, keepdims=True) + EPS) * g_ref[...]
        xf = xn * (1.0 + sc_ref[...]) + sh_ref[...]
        xf_ref[...] = xf
        logits = _dot3(xf, wr_ref[...]) + br_ref[...]
        lane = lax.broadcasted_iota(I32, logits.shape, 1)
        lane_f = lane.astype(F32)
        neg_inf = -jnp.inf
        lg = jnp.where(lane < N_EXPERTS, logits, neg_inf)
        vals, idxs = [], []
        for _k in range(TOP_K):
            mx = jnp.max(lg, axis=1, keepdims=True)
            ix = jnp.min(jnp.where(lg == mx, lane_f, float(LANES)), axis=1, keepdims=True)
            vals.append(mx)
            idxs.append(ix)
            lg = jnp.where(lane_f == ix, neg_inf, lg)
        es = [jnp.exp(vk - vals[0]) for vk in vals]
        tot = es[0] + es[1] + es[2] + es[3]
        idx_out = jnp.zeros(logits.shape, F32)
        gate_out = jnp.zeros(logits.shape, F32)
        for k in range(TOP_K):
            idx_out = jnp.where(lane == k, idxs[k], idx_out)
            gate_out = jnp.where(lane == k, es[k] / tot, gate_out)
        idx_ref[...] = idx_out.astype(I32)
        gate_ref[...] = gate_out


def _outproj(o_sb, h_ml, w_out, x2d, gate_a, shift_f, scale_f, norm_ffn, wr_pad, br_pad,
             *, tm, rows_per_group):
    t, d = x2d.shape
    tn = 512
    n_col = d // tn
    kern = functools.partial(_outproj_kernel, tn=tn, n_col=n_col)
    return pl.pallas_call(
        kern,
        out_shape=(jax.ShapeDtypeStruct((t, d), F32),
                   jax.ShapeDtypeStruct((t, d), F32),
                   jax.ShapeDtypeStruct((t, LANES), I32),
                   jax.ShapeDtypeStruct((t, LANES), F32)),
        grid=(t // tm, n_col),
        in_specs=[pl.BlockSpec((tm, SB_W), lambda i, j: (i, 0)),
                  pl.BlockSpec((tm, ML_V_W), lambda i, j: (i, 0)),
                  pl.BlockSpec((SB_W + ML_V_W, tn), lambda i, j: (0, j)),
                  pl.BlockSpec((tm, tn), lambda i, j: (i, j)),
                  _mod_spec(gate_a, tm, rows_per_group, col_tile=tn),
                  _mod_spec(shift_f, tm, rows_per_group),
                  _mod_spec(scale_f, tm, rows_per_group),
                  pl.BlockSpec((1, d), lambda i, j: (0, 0)),
                  pl.BlockSpec((d, LANES), lambda i, j: (0, 0)),
                  pl.BlockSpec((1, LANES), lambda i, j: (0, 0))],
        out_specs=(pl.BlockSpec((tm, tn), lambda i, j: (i, j)),
                   pl.BlockSpec((tm, d), lambda i, j: (i, 0)),
                   pl.BlockSpec((tm, LANES), lambda i, j: (i, 0)),
                   pl.BlockSpec((tm, LANES), lambda i, j: (i, 0))),
        scratch_shapes=[pltpu.VMEM((tm, SB_W + ML_V_W), BF16),
                        pltpu.VMEM((tm, d), F32)],
        compiler_params=_cparams(("arbitrary", "arbitrary"), 48),
        name="outproj",
    )(o_sb, h_ml, w_out, x2d, gate_a, shift_f, scale_f, norm_ffn.reshape(1, d), wr_pad, br_pad)


def _scatter_kernel(pos_ref, last_ref, xp_ref, xs_ref, o_hbm, sem, *, tok_tile, n_p, n_s):
    i = pl.program_id(0)
    fill_rows = min(MOE_TILE, tok_tile)

    @pl.when(i == 0)
    def _():
        def fill_copy(e, part):
            row = last_ref[e] * MOE_TILE + part * fill_rows
            return pltpu.make_async_copy(xp_ref.at[pl.ds(0, fill_rows), :],
                                         o_hbm.at[pl.ds(row, fill_rows), :], sem)

        def fill(e, carry):
            @pl.when(last_ref[e] >= 0)
            def _():
                for part in range(MOE_TILE // fill_rows):
                    fill_copy(e, part).start()
            return carry

        def drain(e, carry):
            @pl.when(last_ref[e] >= 0)
            def _():
                for part in range(MOE_TILE // fill_rows):
                    fill_copy(e, part).wait()
            return carry

        lax.fori_loop(0, N_EXPERTS, fill, 0)
        lax.fori_loop(0, N_EXPERTS, drain, 0)

    def scatter_block(x_ref, n_rows, base):
        def token(tok, carry):
            for k in range(TOP_K):
                p = pos_ref[base + tok * TOP_K + k]
                pltpu.make_async_copy(x_ref.at[pl.ds(tok, 1), :], o_hbm.at[pl.ds(p, 1), :],
                                      sem).start()
            return carry

        lax.fori_loop(0, n_rows, token, 0, unroll=8)
        done = o_hbm.at[pl.ds(0, n_rows * TOP_K), :]
        pltpu.make_async_copy(done, done, sem).wait()

    @pl.when(i < n_p // tok_tile)
    def _():
        scatter_block(xp_ref, tok_tile, i * tok_tile * TOP_K)

    @pl.when(i == n_p // tok_tile)
    def _():
        scatter_block(xs_ref, n_s, n_p * TOP_K)


def _scatter_rows(pos, last_tile, xf_p, xf_s, n_rows):
    n_p, w = xf_p.shape
    n_s = xf_s.shape[0]
    tok_tile = next(t for t in range(512, 0, -SUBLANES) if n_p % t == 0)
    n_p_steps = n_p // tok_tile
    kern = functools.partial(_scatter_kernel, tok_tile=tok_tile, n_p=n_p, n_s=n_s)
    grid_spec = pltpu.PrefetchScalarGridSpec(
        num_scalar_prefetch=2,
        grid=(n_p_steps + 1,),
        in_specs=[pl.BlockSpec((tok_tile, w), lambda i, p, q: (jnp.minimum(i, n_p_steps - 1), 0)),
                  pl.BlockSpec((n_s, w), lambda i, p, q: (0, 0))],
        out_specs=pl.BlockSpec(memory_space=pl.ANY),
        scratch_shapes=[pltpu.SemaphoreType.DMA],
    )
    return pl.pallas_call(
        kern,
        out_shape=jax.ShapeDtypeStruct((n_rows, w), xf_p.dtype),
        grid_spec=grid_spec,
        compiler_params=_cparams(("arbitrary",), 32),
        name="moe_scatter",
    )(pos, last_tile, xf_p, xf_s)


def _moe_kernel(e_ref, start_ref, tiles_ref, x_hbm, wu_ref, bu_ref, wd_ref, bd_ref, y_hbm,
                x_stage, x_bf, acc, wu_bf, wd_perm, wd_bf, sem_in, sem_out, *, n_ff, d_model, n_items):
    del e_ref
    it = pl.program_id(0)
    f = pl.program_id(1)
    n_tiles = tiles_ref[it]
    row0 = pl.multiple_of(start_ref[it] * MOE_TILE, MOE_TILE)
    half = LANES // 2

    def tile_rows(t):
        return pl.ds(pl.multiple_of(t * MOE_TILE, MOE_TILE), MOE_TILE)

    def x_copy(item, t):
        src0 = pl.multiple_of(start_ref[item] * MOE_TILE, MOE_TILE)
        return pltpu.make_async_copy(x_hbm.at[pl.ds(src0 + t * MOE_TILE, MOE_TILE), :],
                                     x_stage.at[tile_rows(t), :], sem_in)

    def y_copy(t):
        return pltpu.make_async_copy(acc.at[tile_rows(t), :],
                                     y_hbm.at[pl.ds(row0 + t * MOE_TILE, MOE_TILE), :], sem_out)

    def for_tiles(count, fn):
        def body(t, carry):
            fn(t)
            return carry
        lax.fori_loop(0, count, body, 0)

    @pl.when(n_tiles > 0)
    def _():
        @pl.when(f == 0)
        def _():
            @pl.when(it == 0)
            def _():
                for_tiles(n_tiles, lambda t: x_copy(it, t).start())

            for_tiles(n_tiles, lambda t: x_copy(it, t).wait())
            b_dn = jnp.broadcast_to(bd_ref[...], (MOE_TILE, d_model))

            def init(t):
                x_bf[tile_rows(t), :] = x_stage[tile_rows(t), :].astype(BF16)
                acc[tile_rows(t), :] = b_dn

            for_tiles(n_tiles, init)
            nxt = jnp.minimum(it + 1, n_items - 1)
            n_next = jnp.where(it + 1 < n_items, tiles_ref[nxt], 0)
            for_tiles(n_next, lambda t: x_copy(nxt, t).start())

        for c in range(MOE_FF_TILE // LANES):
            for s in range(d_model // LANES):
                for par in range(2):
                    src = wd_ref[c * LANES + par * half:c * LANES + (par + 1) * half,
                                 s * LANES:(s + 1) * LANES]
                    wd_perm[s, pl.ds(c * LANES + par, half, stride=2), :] = src
        for s in range(d_model // LANES):
            wd_bf[:, s * LANES:(s + 1) * LANES] = wd_perm[s].astype(BF16)

        b_up = bu_ref[...]

        def span_rows(t, n):
            return pl.ds(pl.multiple_of(t * MOE_TILE, MOE_TILE), n * MOE_TILE)

        def up_proj(t, n, wu):
            return jnp.dot(x_bf[span_rows(t, n), :], wu,
                           preferred_element_type=F32) + b_up

        def activation(hu):
            even = (lax.broadcasted_iota(I32, (hu.shape[0], LANES), 1) % 2) == 0
            acts = []
            for c in range(MOE_FF_TILE // LANES):
                a_blk = hu[:, (2 * c) * LANES:(2 * c + 1) * LANES]
                b_blk = hu[:, (2 * c + 1) * LANES:(2 * c + 2) * LANES]
                gate = jnp.where(even, a_blk, pltpu.roll(b_blk, 1, 1))
                up = jnp.where(even, pltpu.roll(a_blk, LANES - 1, 1), b_blk)
                gate = jnp.minimum(gate, SWIGLU_LIMIT)
                up = jnp.clip(up, -SWIGLU_LIMIT, SWIGLU_LIMIT)
                acts.append((up + 1.0) * gate * jax.nn.sigmoid(SWIGLU_ALPHA * gate))
            return jnp.concatenate(acts, axis=1).astype(BF16)

        def down_proj(t, n, act):
            acc[span_rows(t, n), :] += jnp.dot(act, wd_bf[...], preferred_element_type=F32)

        def chains(spans):
            wu = wu_ref[...].astype(BF16)
            hus = [up_proj(t, n, wu) for t, n in spans]
            acts = [activation(hu) for hu in hus]
            for (t, n), act in zip(spans, acts):
                down_proj(t, n, act)

        def run_tiles(after_tile):
            def quad(p, carry):
                chains([(4 * p, 2), (4 * p + 2, 2)])
                for u in range(4):
                    after_tile(4 * p + u)
                return carry

            n_quads = n_tiles // 4
            lax.fori_loop(0, n_quads, quad, 0)
            rest = n_tiles - 4 * n_quads

            @pl.when(rest >= 2)
            def _():
                chains([(4 * n_quads, 1), (4 * n_quads + 1, 1)])
                after_tile(4 * n_quads)
                after_tile(4 * n_quads + 1)

            @pl.when(rest % 2 == 1)
            def _():
                chains([(n_tiles - 1, 1)])
                after_tile(n_tiles - 1)

        @pl.when(f < n_ff - 1)
        def _():
            run_tiles(lambda t: None)

        @pl.when(f == n_ff - 1)
        def _():
            run_tiles(lambda t: y_copy(t).start())
            for_tiles(n_tiles, lambda t: y_copy(t).wait())


def _moe(item_e, item_start, item_tiles, x_sorted, w_up, b_up, w_down, b_down):
    n_rows = x_sorted.shape[0]
    n_exp, d, ff2_total = w_up.shape
    d_ff = ff2_total // 2
    n_ff = d_ff // MOE_FF_TILE
    n_items = item_e.shape[0]
    item_rows = MOE_ITEM_TILES * MOE_TILE
    kern = functools.partial(_moe_kernel, n_ff=n_ff, d_model=d, n_items=n_items)

    def ff_tile(i, f, n):
        return jnp.where(n[i] > 0, f, n_ff - 1)

    grid_spec = pltpu.PrefetchScalarGridSpec(
        num_scalar_prefetch=3,
        grid=(n_items, n_ff),
        in_specs=[pl.BlockSpec(memory_space=pl.ANY),
                  pl.BlockSpec((None, d, 2 * MOE_FF_TILE),
                               lambda i, f, e, s, n: (e[i], 0, ff_tile(i, f, n))),
                  pl.BlockSpec((None, 1, 2 * MOE_FF_TILE),
                               lambda i, f, e, s, n: (e[i], 0, ff_tile(i, f, n))),
                  pl.BlockSpec((None, MOE_FF_TILE, d),
                               lambda i, f, e, s, n: (e[i], ff_tile(i, f, n), 0)),
                  pl.BlockSpec((None, 1, d), lambda i, f, e, s, n: (e[i], 0, 0))],
        out_specs=pl.BlockSpec(memory_space=pl.ANY),
        scratch_shapes=[pltpu.VMEM((item_rows, d), F32),
                        pltpu.VMEM((item_rows, d), BF16),
                        pltpu.VMEM((item_rows, d), F32),
                        pltpu.VMEM((d, 2 * MOE_FF_TILE), BF16),
                        pltpu.VMEM((d // LANES, MOE_FF_TILE, LANES), F32),
                        pltpu.VMEM((MOE_FF_TILE, d), BF16),
                        pltpu.SemaphoreType.DMA,
                        pltpu.SemaphoreType.DMA],
    )
    return pl.pallas_call(
        kern,
        out_shape=jax.ShapeDtypeStruct((n_rows, d), F32),
        grid_spec=grid_spec,
        compiler_params=_cparams(("arbitrary", "arbitrary"), 56),
        name="moe_experts",
    )(item_e, item_start, item_tiles, x_sorted, w_up,
      b_up.reshape(n_exp, 1, ff2_total), w_down, b_down.reshape(n_exp, 1, d))


def _combine_kernel(pos_ref, y_hbm, h_ref, gate_ref, gt_ref, g_ref, o_ref, buf, sem, *, tok_off):
    i = pl.program_id(0)

    def issue(step, slot):
        base = (step * COMBINE_TOK + tok_off) * TOP_K

        def token(tok, carry):
            for k in range(TOP_K):
                p = pos_ref[base + tok * TOP_K + k]
                pltpu.make_async_copy(y_hbm.at[pl.ds(p, 1), :],
                                      buf.at[slot, pl.ds(k * COMBINE_TOK + tok, 1), :],
                                      sem.at[slot]).start()
            return carry

        lax.fori_loop(0, COMBINE_TOK, token, 0, unroll=8)

    @pl.when(i == 0)
    def _():
        issue(0, 0)

    @pl.when(i + 1 < pl.num_programs(0))
    def _():
        issue(i + 1, (i + 1) % 2)

    slot = i % 2
    pltpu.make_async_copy(y_hbm.at[pl.ds(0, COMBINE_TOK * TOP_K), :], buf.at[slot],
                          sem.at[slot]).wait()
    gates = gate_ref[...]
    ffn = jnp.zeros(h_ref.shape, F32)
    for k in range(TOP_K):
        ffn = ffn + gates[:, k:k + 1] * buf[slot, k * COMBINE_TOK:(k + 1) * COMBINE_TOK, :]
    y = h_ref[...] + gt_ref[...] * ffn
    o_ref[...] = y * lax.rsqrt(jnp.mean(y * y, axis=-1, keepdims=True) + EPS) * g_ref[...]


def _combine(pos, y_rows, h, gate, gate_f, norm_final, *, tok_off, rows_per_group):
    n_tok, d = h.shape
    kern = functools.partial(_combine_kernel, tok_off=tok_off)
    grid_spec = pltpu.PrefetchScalarGridSpec(
        num_scalar_prefetch=1,
        grid=(n_tok // COMBINE_TOK,),
        in_specs=[pl.BlockSpec(memory_space=pl.ANY),
                  pl.BlockSpec((COMBINE_TOK, d), lambda i, p: (i, 0)),
                  pl.BlockSpec((COMBINE_TOK, LANES), lambda i, p: (i, 0)),
                  _mod_spec(gate_f, COMBINE_TOK, rows_per_group),
                  pl.BlockSpec((1, d), lambda i, p: (0, 0))],
        out_specs=pl.BlockSpec((COMBINE_TOK, d), lambda i, p: (i, 0)),
        scratch_shapes=[pltpu.VMEM((2, COMBINE_TOK * TOP_K, d), F32),
                        pltpu.SemaphoreType.DMA((2,))],
    )
    return pl.pallas_call(
        kern,
        out_shape=jax.ShapeDtypeStruct((n_tok, d), F32),
        grid_spec=grid_spec,
        compiler_params=_cparams(("arbitrary",), 32),
        name="moe_combine",
    )(pos, y_rows, h, gate, gate_f, norm_final.reshape(1, d))


def _routing_tables(top_idx):
    n_tok = top_idx.shape[0]
    n_assign = n_tok * TOP_K
    flat_e = top_idx.reshape(-1)
    onehot = (flat_e[:, None] == jnp.arange(N_EXPERTS, dtype=I32)[None, :]).astype(I32)
    running = jnp.cumsum(onehot, axis=0)
    rank = jnp.take_along_axis(running, flat_e[:, None], axis=1)[:, 0] - 1
    counts = running[-1]
    tiles_e = (counts + MOE_TILE - 1) // MOE_TILE
    tile_start_e = jnp.cumsum(tiles_e) - tiles_e
    pos = tile_start_e[flat_e] * MOE_TILE + rank

    n_tiles_max = -(-n_assign // MOE_TILE) + N_EXPERTS
    n_rows = n_tiles_max * MOE_TILE
    last_tile = jnp.where(tiles_e > 0, tile_start_e + tiles_e - 1, -1)

    n_items_max = (n_tiles_max + N_EXPERTS * (MOE_ITEM_TILES - 1)) // MOE_ITEM_TILES
    items_e = (tiles_e + MOE_ITEM_TILES - 1) // MOE_ITEM_TILES
    items_end = jnp.cumsum(items_e)
    it = jnp.arange(n_items_max, dtype=I32)
    e_of = jnp.minimum(jnp.sum((it[:, None] >= items_end[None, :]).astype(I32), axis=1), N_EXPERTS - 1)
    local = it - (items_end[e_of] - items_e[e_of])
    valid = it < items_end[-1]
    tiles_left = tiles_e[e_of] - local * MOE_ITEM_TILES
    item_tiles = jnp.where(valid, jnp.clip(tiles_left, 0, MOE_ITEM_TILES), 0).astype(I32)
    item_start = jnp.where(valid, tile_start_e[e_of] + local * MOE_ITEM_TILES, 0).astype(I32)
    last_e = e_of[jnp.maximum(items_end[-1] - 1, 0)]
    item_e = jnp.where(valid, e_of, last_e).astype(I32)
    return pos.astype(I32), last_tile.astype(I32), n_rows, item_e, item_start, item_tiles


def _pick_tile(n, pref):
    t = min(n, pref)
    while n % t:
        t //= 2
    return t


def kernel(x_prompt, x_sample, c_prompt, c_sample, cache_k, cache_v, page_table, state_C, state_n, state_m, w_ada, b_ada, norm_mix, norm_ffn, w_in, b_gates, sb_bias, norm_sb_out, norm_ml_out, w_out, w_router, b_router, w_up, b_up, w_down, b_down, norm_final):
    bsz, seq, d = x_prompt.shape
    dbs, dec_seq, _ = x_sample.shape
    depth = w_ada.shape[0]
    assert depth == 1
    n_p, n_s = bsz * seq, dbs * dec_seq
    n_tok = n_p + n_s

    (w_ada, b_ada, norm_mix, norm_ffn, w_in, b_gates, sb_bias, norm_sb_out, norm_ml_out, w_out,
     w_router, b_router, w_up, b_up, w_down, b_down) = [
        a[0] for a in (w_ada, b_ada, norm_mix, norm_ffn, w_in, b_gates, sb_bias, norm_sb_out,
                       norm_ml_out, w_out, w_router, b_router, w_up, b_up, w_down, b_down)]

    n_c = bsz + dbs
    n_c_pad = -(-n_c // SUBLANES) * SUBLANES
    c_all = jnp.concatenate([c_prompt, c_sample, jnp.zeros((n_c_pad - n_c, d), F32)], axis=0)
    mod = _ada(c_all, w_ada, b_ada)
    mods_p = [mod[:bsz, i * d:(i + 1) * d].reshape(bsz, 1, d) for i in range(N_MOD)]
    mods_s = [jnp.repeat(mod[bsz:n_c, i * d:(i + 1) * d], dec_seq, axis=0).reshape(1, n_s, d)
              for i in range(N_MOD)]

    w_in = w_in.T
    wg_pad = w_in[w_in.shape[0] - LANES:, :]
    bg_pad = jnp.pad(b_gates, (GATE_LANE0, 0)).reshape(1, LANES)
    wr_pad = jnp.pad(w_router, ((0, 0), (0, LANES - N_EXPERTS)))
    br_pad = jnp.pad(b_router, (0, LANES - N_EXPERTS)).reshape(1, LANES)

    xp2 = x_prompt.reshape(n_p, d)
    xs2 = x_sample.reshape(n_s, d)
    tm_p = _pick_tile(seq, 1024)

    proj_p, gates_p, k_p, v_p = _inproj(xp2, mods_p[0], mods_p[1], norm_mix, w_in, wg_pad, bg_pad,
                                        tm=tm_p, rows_per_group=seq)
    osb_p = _sb_prompt(proj_p, sb_bias, norm_sb_out, bsz=bsz, seq=seq)
    chunk_p = _pick_tile(seq, ML_CHUNK)
    hml_p, c_p, nn_p, m_p = _mlstm(
        proj_p, gates_p,
        jnp.zeros((bsz, H_ML, DV_ML, DQK_ML), F32), jnp.zeros((bsz, H_ML, DQK_ML), F32),
        jnp.zeros((bsz, H_ML), F32), norm_ml_out,
        bsz=bsz, n_chunks=seq // chunk_p, chunk=chunk_p, valid=chunk_p)

    proj_s, gates_s, k_s, v_s = _inproj(xs2, mods_s[0], mods_s[1], norm_mix, w_in, wg_pad, bg_pad,
                                        tm=n_s, rows_per_group=n_s)
    osb_s = _sb_paged(proj_s[:, :SB_W].reshape(dbs, dec_seq, SB_W),
                      k_s.reshape(dbs, dec_seq, SB_W), v_s.reshape(dbs, dec_seq, SB_W),
                      cache_k[0], cache_v[0], page_table, sb_bias, norm_sb_out)
    osb_s = (osb_s.reshape(dbs, H_SB, dec_seq, D_SB).swapaxes(1, 2)
             .reshape(n_s, SB_W).astype(BF16))
    chunk_s = -(-dec_seq // BF16_SUBLANES) * BF16_SUBLANES
    pad_rows = lambda a: jnp.pad(a.reshape(dbs, dec_seq, a.shape[-1]),
                                 ((0, 0), (0, chunk_s - dec_seq), (0, 0))).reshape(dbs * chunk_s, a.shape[-1])
    hml_s, c_s, nn_s, m_s = _mlstm(
        pad_rows(proj_s), pad_rows(gates_s), state_C[0], state_n[0], state_m[0], norm_ml_out,
        bsz=dbs, n_chunks=1, chunk=chunk_s, valid=dec_seq)
    hml_s = hml_s.reshape(dbs, chunk_s, ML_V_W)[:, :dec_seq].reshape(n_s, ML_V_W)

    tm_o = _pick_tile(seq, 512)
    h_p, xf_p, idx_p, gate_p = _outproj(
        osb_p, hml_p, w_out, xp2, mods_p[2], mods_p[3], mods_p[4], norm_ffn, wr_pad, br_pad,
        tm=tm_o, rows_per_group=seq)
    h_s, xf_s, idx_s, gate_s = _outproj(
        osb_s, hml_s, w_out, xs2, mods_s[2], mods_s[3], mods_s[4], norm_ffn, wr_pad, br_pad,
        tm=n_s, rows_per_group=n_s)
    top_idx = jnp.concatenate([idx_p[:, :TOP_K], idx_s[:, :TOP_K]], axis=0)

    pos, last_tile, n_rows, item_e, item_start, item_tiles = _routing_tables(top_idx)
    x_sorted = _scatter_rows(pos, last_tile, xf_p, xf_s, n_rows)
    y_rows = _moe(item_e, item_start, item_tiles, x_sorted, w_up, b_up, w_down, b_down)
    y_p = _combine(pos, y_rows, h_p, gate_p, mods_p[5], norm_final, tok_off=0, rows_per_group=seq)
    y_s = _combine(pos, y_rows, h_s, gate_s, mods_s[5], norm_final, tok_off=n_p, rows_per_group=n_s)

    return (y_p.reshape(bsz, seq, d), y_s.reshape(dbs, dec_seq, d),
            k_p.reshape(1, bsz, seq, H_SB, D_SB), v_p.reshape(1, bsz, seq, H_SB, D_SB),
            k_s.reshape(1, dbs, dec_seq, H_SB, D_SB), v_s.reshape(1, dbs, dec_seq, H_SB, D_SB),
            c_p[None], nn_p[None], m_p[None], c_s[None], nn_s[None], m_s[None])
```
